```python
import jax, jax.numpy as jnp
from jax import lax
import numpy as np

D_MODEL = 2048
BATCH = 2
SEQ = 4096
DEPTH = 4

GRID_W = 64
N_MLSTM_HEADS = 4
MLSTM_QK_DIM = 128
MLSTM_V_DIM = 256
MLSTM_QK_WIDTH = N_MLSTM_HEADS * MLSTM_QK_DIM
MLSTM_WIDTH = N_MLSTM_HEADS * MLSTM_V_DIM
MLSTM_CHUNK = 128
MLSTM_CONV = 3
N_DIRS = 2
N_ATTN_HEADS = 8
N_KV_HEADS = 2
ATTN_HEAD_DIM = 128
ATTN_WIDTH = N_ATTN_HEADS * ATTN_HEAD_DIM
KV_WIDTH = N_KV_HEADS * ATTN_HEAD_DIM
Q_BLOCK = 128
ROPE_THETA = 10000.0
D_FF = 5632
N_EXPERTS = 8
TOP_K = 2
D_FF_EXPERT = 7168
PLE_DIM = 256
EPS = 1e-6
SPLIT_SIZES = (MLSTM_QK_WIDTH, MLSTM_QK_WIDTH, MLSTM_WIDTH, N_DIRS * N_MLSTM_HEADS, N_DIRS * N_MLSTM_HEADS, MLSTM_WIDTH, ATTN_WIDTH, KV_WIDTH, KV_WIDTH, D_MODEL, D_MODEL)
IN_COLS = 2 * MLSTM_QK_WIDTH + 2 * MLSTM_WIDTH + 2 * N_DIRS * N_MLSTM_HEADS + ATTN_WIDTH + 2 * KV_WIDTH + 2 * D_MODEL

kernel_name = 'hybrid_mlstm_axial_gqa_moe_ple_encoder'


def rms_norm(x, g):
    xf = x.astype(jnp.float32)
    y = xf * lax.rsqrt(jnp.mean(xf * xf, axis=-1, keepdims=True) + EPS)
    return (y * g.astype(jnp.float32)).astype(x.dtype)


def split_cols(z):
    out, start = [], 0
    for s in SPLIT_SIZES:
        out.append(z[..., start:start + s])
        start += s
    return out


def centred_dwconv(x, w, b):
    k = w.shape[0]
    pad = k // 2
    s = x.shape[1]
    xp = jnp.pad(x, ((0, 0), (pad, pad), (0, 0)))
    out = xp[:, 0:s] * w[0]
    for j in range(1, k):
        out = out + xp[:, j:j + s] * w[j]
    return out + b


def mlstm_chunkwise(q, k, v, log_i, log_f):
    out_dtype = v.dtype
    b_, h_, s_, dk = q.shape
    dv = v.shape[-1]
    nc = s_ // MLSTM_CHUNK
    L = MLSTM_CHUNK

    def to_chunks(a):
        a = a.astype(jnp.float32)
        return jnp.moveaxis(a.reshape(b_, h_, nc, L, *a.shape[3:]), 2, 0)

    xs = (to_chunks(q), to_chunks(k), to_chunks(v), to_chunks(log_i), to_chunks(log_f))
    tril = jnp.tril(jnp.ones((L, L), dtype=bool))

    def step(carry, xs_c):
        C, n, m = carry
        qc, kc, vc, ic, fc = xs_c
        bcum = jnp.cumsum(fc, axis=-1)
        a_inter = bcum + m[..., None]
        dmat = bcum[..., :, None] - bcum[..., None, :] + ic[..., None, :]
        dmat = jnp.where(tril, dmat, -jnp.inf)
        m_t = jnp.maximum(a_inter, jnp.max(dmat, axis=-1))
        w_inter = jnp.exp(a_inter - m_t)
        s_qk = jnp.einsum('bhtd,bhsd->bhts', qc, kc) * jnp.exp(dmat - m_t[..., None])
        num = w_inter[..., None] * jnp.einsum('bhtd,bhde->bhte', qc, C) + jnp.einsum('bhts,bhse->bhte', s_qk, vc)
        den = w_inter * jnp.einsum('bhtd,bhd->bht', qc, n) + jnp.sum(s_qk, axis=-1)
        h = num / jnp.maximum(jnp.abs(den), jnp.exp(-m_t))[..., None]
        b_end = bcum[..., -1]
        g = b_end[..., None] - bcum + ic
        m_new = jnp.maximum(b_end + m, jnp.max(g, axis=-1))
        decay = jnp.exp(b_end + m - m_new)
        ws = jnp.exp(g - m_new[..., None])
        C_new = decay[..., None, None] * C + jnp.einsum('bhs,bhsd,bhse->bhde', ws, kc, vc)
        n_new = decay[..., None] * n + jnp.einsum('bhs,bhsd->bhd', ws, kc)
        return (C_new, n_new, m_new), h

    init = (jnp.zeros((b_, h_, dk, dv), jnp.float32), jnp.zeros((b_, h_, dk), jnp.float32), jnp.zeros((b_, h_), jnp.float32))
    _, hs = lax.scan(step, init, xs)
    return jnp.moveaxis(hs, 0, 2).reshape(b_, h_, s_, dv).astype(out_dtype)


def axial_rope_tables(seq):
    rows = seq // GRID_W
    row = jnp.broadcast_to(jnp.arange(rows, dtype=jnp.float32)[:, None], (rows, GRID_W)).reshape(seq)
    col = jnp.broadcast_to(jnp.arange(GRID_W, dtype=jnp.float32)[None, :], (rows, GRID_W)).reshape(seq)
    axis_dim = ATTN_HEAD_DIM // 2
    inv_freq = ROPE_THETA ** (-jnp.arange(0, axis_dim, 2, dtype=jnp.float32) / axis_dim)
    ang = jnp.stack([row, col], axis=-1)[..., None] * inv_freq
    return jnp.cos(ang), jnp.sin(ang)


def apply_axial_rope(x, cos, sin):
    shp = x.shape
    xf = x.astype(jnp.float32).reshape(*shp[:-1], 2, 2, ATTN_HEAD_DIM // 4)
    x1 = xf[..., 0, :]
    x2 = xf[..., 1, :]
    out = jnp.stack([x1 * cos - x2 * sin, x2 * cos + x1 * sin], axis=-2)
    return out.reshape(shp).astype(x.dtype)


def blocked_gqa(q, k, v):
    b_, kvh, g_, s_, d_ = q.shape
    nb = s_ // Q_BLOCK
    qb = jnp.moveaxis(q.reshape(b_, kvh, g_, nb, Q_BLOCK, d_), 3, 0)
    scale = d_ ** -0.5

    def one_block(qblk):
        s = jnp.einsum('bkgqd,bksd->bkgqs', qblk, k).astype(jnp.float32) * scale
        pr = jax.nn.softmax(s, axis=-1)
        return jnp.einsum('bkgqs,bksd->bkgqd', pr.astype(v.dtype), v)

    o = lax.map(one_block, qb)
    o = jnp.moveaxis(o, 0, 3).reshape(b_, kvh * g_, s_, d_)
    return o.transpose(0, 2, 1, 3).reshape(b_, s_, kvh * g_ * d_)


def swiglu(h, wg, wu, wd):
    return (jax.nn.silu(h @ wg) * (h @ wu)) @ wd


def moe_swiglu(h, router, router_b, wg, wu, wd):
    b_, s_, d_ = h.shape
    t = h.reshape(-1, d_)
    logits = (t @ router).astype(jnp.float32) + router_b.astype(jnp.float32)
    top_vals, top_idx = lax.top_k(logits, TOP_K)
    probs = jax.nn.softmax(top_vals, axis=-1)
    comb = jnp.sum(jax.nn.one_hot(top_idx, N_EXPERTS, dtype=jnp.float32) * probs[..., None], axis=1)
    out = jnp.zeros_like(t)
    for e in range(N_EXPERTS):
        out = out + comb[:, e:e + 1].astype(t.dtype) * swiglu(t, wg[e], wu[e], wd[e])
    return out.reshape(b_, s_, d_)


def setup_inputs(seed: int = 0) -> dict:
    key = jax.random.key(seed)
    ks = iter(jax.random.split(key, 40))

    def nrm(shape, scale):
        return jax.random.normal(next(ks), shape, jnp.float32) * scale

    def gain(shape):
        return 1.0 + nrm(shape, 0.02)

    L = DEPTH
    nd = (DEPTH + 1) // 2
    nm = DEPTH // 2
    D = D_MODEL
    return {
        'x': nrm((BATCH, SEQ, D), 1.0),
        'p': nrm((DEPTH, BATCH, SEQ, PLE_DIM), 1.0),
        'norm_mix_g': gain((L, D)),
        'w_in': nrm((L, D, IN_COLS), D ** -0.5),
        'conv_w': nrm((L, MLSTM_CONV, 2 * MLSTM_QK_WIDTH), MLSTM_CONV ** -0.5),
        'conv_b': nrm((L, 2 * MLSTM_QK_WIDTH), 0.02),
        'b_igate': nrm((L, N_DIRS, N_MLSTM_HEADS), 0.1),
        'b_fgate': 3.0 + 3.0 * jax.random.uniform(next(ks), (L, N_DIRS, N_MLSTM_HEADS), jnp.float32),
        'mlstm_norm_g': gain((L, N_MLSTM_HEADS, MLSTM_V_DIM)),
        'q_norm_g': gain((L, ATTN_HEAD_DIM)),
        'k_norm_g': gain((L, ATTN_HEAD_DIM)),
        'w_mlstm_up': nrm((L, MLSTM_WIDTH, D), MLSTM_WIDTH ** -0.5),
        'w_attn_up': nrm((L, ATTN_WIDTH, D), ATTN_WIDTH ** -0.5),
        'w_out': nrm((L, D, D), D ** -0.5),
        'norm_ffn_g': gain((L, D)),
        'ffn_w_gate': nrm((nd, D, D_FF), D ** -0.5),
        'ffn_w_up': nrm((nd, D, D_FF), D ** -0.5),
        'ffn_w_down': nrm((nd, D_FF, D), D_FF ** -0.5),
        'moe_router': nrm((nm, D, N_EXPERTS), D ** -0.5),
        'moe_router_b': nrm((nm, N_EXPERTS), 0.01),
        'moe_w_gate': nrm((nm, N_EXPERTS, D, D_FF_EXPERT), D ** -0.5),
        'moe_w_up': nrm((nm, N_EXPERTS, D, D_FF_EXPERT), D ** -0.5),
        'moe_w_down': nrm((nm, N_EXPERTS, D_FF_EXPERT, D), D_FF_EXPERT ** -0.5),
        'norm_ple_g': gain((L, D)),
        'w_ple_gate': nrm((L, D, D), D ** -0.5),
        'w_ple_proj': nrm((L, PLE_DIM, D), PLE_DIM ** -0.5),
    }


def reference(x, p, norm_mix_g, w_in, conv_w, conv_b, b_igate, b_fgate, mlstm_norm_g, q_norm_g, k_norm_g, w_mlstm_up, w_attn_up, w_out, norm_ffn_g, ffn_w_gate, ffn_w_up, ffn_w_down, moe_router, moe_router_b, moe_w_gate, moe_w_up, moe_w_down, norm_ple_g, w_ple_gate, w_ple_proj):
    b_, s_, _ = x.shape
    cos, sin = axial_rope_tables(s_)
    grp = N_ATTN_HEADS // N_KV_HEADS
    for i in range(DEPTH):
        h = rms_norm(x, norm_mix_g[i])
        z = h @ w_in[i]
        mq, mk, mv, mi, mf, mo, aq, ak, av, gm, ga = split_cols(z)

        qk = jax.nn.silu(centred_dwconv(jnp.concatenate([mq, mk], axis=-1), conv_w[i], conv_b[i]))
        mq, mk = qk[..., :MLSTM_QK_WIDTH], qk[..., MLSTM_QK_WIDTH:]
        q_m = mq.reshape(b_, s_, N_MLSTM_HEADS, MLSTM_QK_DIM).transpose(0, 2, 1, 3) * (MLSTM_QK_DIM ** -0.5)
        k_m = mk.reshape(b_, s_, N_MLSTM_HEADS, MLSTM_QK_DIM).transpose(0, 2, 1, 3)
        v_m = mv.reshape(b_, s_, N_MLSTM_HEADS, MLSTM_V_DIM).transpose(0, 2, 1, 3)
        log_i = (mi.reshape(b_, s_, N_DIRS, N_MLSTM_HEADS).astype(jnp.float32) + b_igate[i].astype(jnp.float32)).transpose(2, 0, 3, 1)
        log_f = jax.nn.log_sigmoid((mf.reshape(b_, s_, N_DIRS, N_MLSTM_HEADS).astype(jnp.float32) + b_fgate[i].astype(jnp.float32)).transpose(2, 0, 3, 1))
        h_fwd = mlstm_chunkwise(q_m, k_m, v_m, log_i[0], log_f[0])
        h_bwd = jnp.flip(mlstm_chunkwise(jnp.flip(q_m, 2), jnp.flip(k_m, 2), jnp.flip(v_m, 2), jnp.flip(log_i[1], -1), jnp.flip(log_f[1], -1)), 2)
        h_m = (h_fwd + h_bwd).transpose(0, 2, 1, 3)
        h_m = rms_norm(h_m, mlstm_norm_g[i]).reshape(b_, s_, MLSTM_WIDTH) * jax.nn.sigmoid(mo)

        q_a = rms_norm(aq.reshape(b_, s_, N_ATTN_HEADS, ATTN_HEAD_DIM).transpose(0, 2, 1, 3), q_norm_g[i])
        k_a = rms_norm(ak.reshape(b_, s_, N_KV_HEADS, ATTN_HEAD_DIM).transpose(0, 2, 1, 3), k_norm_g[i])
        v_a = av.reshape(b_, s_, N_KV_HEADS, ATTN_HEAD_DIM).transpose(0, 2, 1, 3)
        q_a = apply_axial_rope(q_a, cos, sin).reshape(b_, N_KV_HEADS, grp, s_, ATTN_HEAD_DIM)
        k_a = apply_axial_rope(k_a, cos, sin)
        h_a = blocked_gqa(q_a, k_a, v_a)

        y = jax.nn.sigmoid(gm) * (h_m @ w_mlstm_up[i]) + jax.nn.sigmoid(ga) * (h_a @ w_attn_up[i])
        x = x + y @ w_out[i]

        h2 = rms_norm(x, norm_ffn_g[i])
        j = i // 2
        if i % 2 == 0:
            x = x + swiglu(h2, ffn_w_gate[j], ffn_w_up[j], ffn_w_down[j])
        else:
            x = x + moe_swiglu(h2, moe_router[j], moe_router_b[j], moe_w_gate[j], moe_w_up[j], moe_w_down[j])

        h3 = rms_norm(x, norm_ple_g[i])
        x = x + jax.nn.sigmoid(h3 @ w_ple_gate[i]) * (p[i] @ w_ple_proj[i])
    return x
```

```python
import functools

import jax
import jax.numpy as jnp
from jax import lax
from jax.experimental import pallas as pl
from jax.experimental.pallas import tpu as pltpu

F32 = jnp.float32
BF16 = jnp.bfloat16

GRID_W = 64
N_MLSTM_HEADS = 4
MLSTM_QK_DIM = 128
MLSTM_V_DIM = 256
MLSTM_QK_WIDTH = N_MLSTM_HEADS * MLSTM_QK_DIM
MLSTM_WIDTH = N_MLSTM_HEADS * MLSTM_V_DIM
MLSTM_CHUNK = 128
N_DIRS = 2
N_ATTN_HEADS = 8
N_KV_HEADS = 2
ATTN_HEAD_DIM = 128
ATTN_WIDTH = N_ATTN_HEADS * ATTN_HEAD_DIM
KV_WIDTH = N_KV_HEADS * ATTN_HEAD_DIM
ROPE_THETA = 10000.0
N_EXPERTS = 8
EPS = 1e-6

LANES = 128
VMEM_LIMIT_BYTES = 56 * 2**20

N_GATE_COLS = 2 * N_DIRS * N_MLSTM_HEADS


def _col_layout(d_model):
    names = ("qk", "mv", "mo", "aq", "ak", "av", "gm", "ga", "gates")
    widths = (2 * MLSTM_QK_WIDTH, MLSTM_WIDTH, MLSTM_WIDTH, ATTN_WIDTH, KV_WIDTH, KV_WIDTH, d_model, d_model, LANES)
    off, out = 0, {}
    for n, w in zip(names, widths):
        out[n] = off
        off += w
    out["total"] = off
    return out


def _permute_w_in(w, d_model):
    sizes = (MLSTM_QK_WIDTH, MLSTM_QK_WIDTH, MLSTM_WIDTH, N_GATE_COLS // 2, N_GATE_COLS // 2, MLSTM_WIDTH,
             ATTN_WIDTH, KV_WIDTH, KV_WIDTH, d_model, d_model)
    segs, start = [], 0
    for s in sizes:
        segs.append(w[:, start:start + s])
        start += s
    mq, mk, mv, mi, mf, mo, aq, ak, av, gm, ga = segs
    pad = jnp.zeros((w.shape[0], LANES - N_GATE_COLS), w.dtype)
    return jnp.concatenate([mq, mk, mv, mo, aq, ak, av, gm, ga, mi, mf, pad], axis=1).astype(BF16)


def _cparams(semantics):
    return pltpu.CompilerParams(dimension_semantics=semantics, vmem_limit_bytes=VMEM_LIMIT_BYTES)


def _rms(x, g):
    ms = jnp.mean(x * x, axis=-1, keepdims=True)
    return x * lax.rsqrt(ms + EPS) * g


def _dot(a, b):
    return jnp.dot(a, b, preferred_element_type=F32)


def _dot_nt(a, b):
    return lax.dot_general(a, b, (((1,), (1,)), ((), ())), preferred_element_type=F32)


def _dot_tn(a, b):
    return lax.dot_general(a, b, (((0,), (0,)), ((), ())), preferred_element_type=F32)


def _norm_matmul_kernel(x_ref, g_ref, w_ref, o_ref, h_scr):
    @pl.when(pl.program_id(1) == 0)
    def _():
        h_scr[...] = _rms(x_ref[...], g_ref[...]).astype(BF16)

    o_ref[...] = _dot(h_scr[...], w_ref[...])


def _norm_matmul(x, g, w, *, tm, tn):
    t, d = x.shape
    n = w.shape[1]
    return pl.pallas_call(
        _norm_matmul_kernel,
        grid=(t // tm, pl.cdiv(n, tn)),
        in_specs=[
            pl.BlockSpec((tm, d), lambda m, j: (m, 0)),
            pl.BlockSpec((1, d), lambda m, j: (0, 0)),
            pl.BlockSpec((d, tn), lambda m, j: (0, j)),
        ],
        out_specs=pl.BlockSpec((tm, tn), lambda m, j: (m, j)),
        out_shape=jax.ShapeDtypeStruct((t, n), F32),
        scratch_shapes=[pltpu.VMEM((tm, d), BF16)],
        compiler_params=_cparams(("parallel", "arbitrary")),
        name="norm_in_proj",
    )(x, g, w)


def _conv_silu_kernel(z_ref, w_ref, b_ref, o_ref, *, q_blocks, q_scale):
    x = z_ref[0]
    s = x.shape[0]
    row = lax.broadcasted_iota(jnp.int32, x.shape, 0)
    prev = jnp.where(row == 0, 0.0, pltpu.roll(x, 1, 0))
    nxt = jnp.where(row == s - 1, 0.0, pltpu.roll(x, s - 1, 0))
    y = prev * w_ref[0:1, :] + x * w_ref[1:2, :] + nxt * w_ref[2:3, :] + b_ref[...]
    y = y * jax.nn.sigmoid(y)
    scale = jnp.where(pl.program_id(1) < q_blocks, q_scale, 1.0).astype(F32)
    o_ref[0] = (y * scale).astype(o_ref.dtype)


def _conv_silu(z3, conv_w, conv_b, *, col0, tc):
    b, s, _ = z3.shape
    width = conv_w.shape[1]
    kern = functools.partial(_conv_silu_kernel, q_blocks=MLSTM_QK_WIDTH // tc, q_scale=MLSTM_QK_DIM ** -0.5)
    return pl.pallas_call(
        kern,
        grid=(b, width // tc),
        in_specs=[
            pl.BlockSpec((1, s, tc), lambda i, j: (i, 0, col0 // tc + j)),
            pl.BlockSpec((3, tc), lambda i, j: (0, j)),
            pl.BlockSpec((1, tc), lambda i, j: (0, j)),
        ],
        out_specs=pl.BlockSpec((1, s, tc), lambda i, j: (i, 0, j)),
        out_shape=jax.ShapeDtypeStruct((b, s, width), BF16),
        compiler_params=_cparams(("parallel", "parallel")),
        name="mlstm_conv_silu",
    )(z3, conv_w, conv_b)


def _log_sigmoid(x):
    return jnp.minimum(x, 0.0) - jnp.log1p(jnp.exp(-jnp.abs(x)))


def _lane_col(x, idx):
    lane = lax.broadcasted_iota(jnp.int32, x.shape, 1)
    return jnp.sum(jnp.where(lane == idx, x, 0.0), axis=-1, keepdims=True)


def _mlstm_kernel(qkf_ref, qkb_ref, vf_ref, vb_ref, gf_ref, gb_ref, bias_ref, hf_ref, hb_ref, c_scr, m_scr):
    L = MLSTM_CHUNK
    dk, dv, nh = MLSTM_QK_DIM, MLSTM_V_DIM, N_MLSTM_HEADS

    @pl.when(pl.program_id(1) == 0)
    def _():
        c_scr[...] = jnp.zeros_like(c_scr)
        m_scr[...] = jnp.zeros_like(m_scr)

    r_i = lax.broadcasted_iota(jnp.int32, (L, L), 0)
    c_i = lax.broadcasted_iota(jnp.int32, (L, L), 1)
    lane = lax.broadcasted_iota(jnp.int32, (L, LANES), 1)
    ones_blk = jnp.where(lane == 0, 1.0, 0.0).astype(F32)

    dirs = ((qkf_ref, vf_ref, gf_ref, hf_ref), (qkb_ref, vb_ref, gb_ref, hb_ref))
    for d, (qk_ref, v_ref, g_ref, h_ref) in enumerate(dirs):
        mask = (c_i <= r_i) if d == 0 else (c_i >= r_i)
        g = g_ref[0] + bias_ref[...]
        gp = jnp.where(lane < N_GATE_COLS // 2, g, _log_sigmoid(g))
        gp_t = gp.T
        bc = jnp.dot(mask.astype(F32), gp, precision=lax.Precision.HIGHEST, preferred_element_type=F32)
        bc_t = bc.T
        end_row = L - 1 if d == 0 else 0
        for h in range(nh):
            ci = d * nh + h
            ch_i, ch_f = d * nh + h, N_GATE_COLS // 2 + d * nh + h
            i_row = gp_t[ch_i:ch_i + 1, :]
            bc_row = bc_t[ch_f:ch_f + 1, :]
            i_col = _lane_col(gp, ch_i)
            bc_col = _lane_col(bc, ch_f)
            m_prev = m_scr[ci, 0:1, 0:1]
            c_prev = c_scr[ci]

            a_col = bc_col + m_prev
            dm = jnp.where(mask, bc_col - bc_row + i_row, -jnp.inf)
            m_t = jnp.maximum(a_col, jnp.max(dm, axis=-1, keepdims=True))
            w_inter = jnp.exp(a_col - m_t)
            w_intra = jnp.exp(dm - m_t)

            q = qk_ref[0, :, h * dk:(h + 1) * dk]
            k = qk_ref[0, :, MLSTM_QK_WIDTH + h * dk:MLSTM_QK_WIDTH + (h + 1) * dk]
            v_ext = jnp.concatenate([v_ref[0, :, h * dv:(h + 1) * dv], ones_blk], axis=1)
            p = (_dot_nt(q, k) * w_intra).astype(BF16)
            num_ext = w_inter * _dot(q, c_prev.astype(BF16)) + _dot(p, v_ext.astype(BF16))
            den = num_ext[:, dv:dv + 1]
            h_ref[0, :, h * dv:(h + 1) * dv] = num_ext[:, :dv] / jnp.maximum(jnp.abs(den), jnp.exp(-m_t))

            b_end = bc_col[end_row:end_row + 1, :]
            g_col = b_end - bc_col + i_col
            m_new = jnp.maximum(b_end + m_prev, jnp.max(g_col, axis=0, keepdims=True))
            decay = jnp.exp(b_end + m_prev - m_new)
            ws = jnp.exp(g_col - m_new)
            c_scr[ci] = decay * c_prev + _dot_tn(k, (ws * v_ext).astype(BF16))
            m_scr[ci] = jnp.broadcast_to(m_new, m_scr.shape[1:])


def _mlstm_scan(qk3, z3, gate_bias, *, v_col0, gate_col0):
    b, s, _ = qk3.shape
    L = MLSTM_CHUNK
    nc = s // L
    n_chain = N_DIRS * N_MLSTM_HEADS
    vb, gb = v_col0 // MLSTM_WIDTH, gate_col0 // LANES
    qk_w = 2 * MLSTM_QK_WIDTH
    return pl.pallas_call(
        _mlstm_kernel,
        grid=(b, nc),
        in_specs=[
            pl.BlockSpec((1, L, qk_w), lambda i, c: (i, c, 0)),
            pl.BlockSpec((1, L, qk_w), lambda i, c: (i, nc - 1 - c, 0)),
            pl.BlockSpec((1, L, MLSTM_WIDTH), lambda i, c: (i, c, vb)),
            pl.BlockSpec((1, L, MLSTM_WIDTH), lambda i, c: (i, nc - 1 - c, vb)),
            pl.BlockSpec((1, L, LANES), lambda i, c: (i, c, gb)),
            pl.BlockSpec((1, L, LANES), lambda i, c: (i, nc - 1 - c, gb)),
            pl.BlockSpec((1, LANES), lambda i, c: (0, 0)),
        ],
        out_specs=[
            pl.BlockSpec((1, L, MLSTM_WIDTH), lambda i, c: (i, c, 0)),
            pl.BlockSpec((1, L, MLSTM_WIDTH), lambda i, c: (i, nc - 1 - c, 0)),
        ],
        out_shape=[jax.ShapeDtypeStruct((b, s, MLSTM_WIDTH), F32)] * 2,
        scratch_shapes=[
            pltpu.VMEM((n_chain, MLSTM_QK_DIM, MLSTM_V_DIM + LANES), F32),
            pltpu.VMEM((n_chain, 8, LANES), F32),
        ],
        compiler_params=_cparams(("parallel", "arbitrary")),
        name="mlstm_scan",
    )(qk3, qk3, z3, z3, z3, z3, gate_bias)


def _qk_prep_kernel(aq_ref, ak_ref, cos_ref, sin_ref, gq_ref, gk_ref, q_ref, k_ref):
    cos, sin = cos_ref[...], sin_ref[...]
    hd = ATTN_HEAD_DIM
    lane = lax.broadcasted_iota(jnp.int32, cos.shape, 1)
    first_half = (lane % (hd // 2)) < (hd // 4)

    def prep(x, g, scale):
        n = _rms(x, g)
        partner = jnp.where(first_half, pltpu.roll(n, hd - hd // 4, 1), pltpu.roll(n, hd // 4, 1))
        return ((n * cos + partner * sin) * scale).astype(BF16)

    for h in range(N_ATTN_HEADS):
        q_ref[0, :, h * hd:(h + 1) * hd] = prep(aq_ref[0, :, h * hd:(h + 1) * hd], gq_ref[...], hd ** -0.5)
    for h in range(N_KV_HEADS):
        k_ref[0, :, h * hd:(h + 1) * hd] = prep(ak_ref[0, :, h * hd:(h + 1) * hd], gk_ref[...], 1.0)


def _qk_prep(z3, cos_t, sin_t, gq, gk, *, aq_col0, ak_col0, ts):
    b, s, _ = z3.shape
    return pl.pallas_call(
        _qk_prep_kernel,
        grid=(b, s // ts),
        in_specs=[
            pl.BlockSpec((1, ts, ATTN_WIDTH), lambda i, j: (i, j, aq_col0 // ATTN_WIDTH)),
            pl.BlockSpec((1, ts, KV_WIDTH), lambda i, j: (i, j, ak_col0 // KV_WIDTH)),
            pl.BlockSpec((ts, ATTN_HEAD_DIM), lambda i, j: (j, 0)),
            pl.BlockSpec((ts, ATTN_HEAD_DIM), lambda i, j: (j, 0)),
            pl.BlockSpec((1, ATTN_HEAD_DIM), lambda i, j: (0, 0)),
            pl.BlockSpec((1, ATTN_HEAD_DIM), lambda i, j: (0, 0)),
        ],
        out_specs=[
            pl.BlockSpec((1, ts, ATTN_WIDTH), lambda i, j: (i, j, 0)),
            pl.BlockSpec((1, ts, KV_WIDTH), lambda i, j: (i, j, 0)),
        ],
        out_shape=[jax.ShapeDtypeStruct((b, s, ATTN_WIDTH), BF16), jax.ShapeDtypeStruct((b, s, KV_WIDTH), BF16)],
        compiler_params=_cparams(("parallel", "parallel")),
        name="attn_qk_norm_rope",
    )(z3, z3, cos_t, sin_t, gq, gk)


def _flash_kernel(q_ref, k_ref, v_ref, o_ref, m_scr, l_scr, acc_scr, *, grp):
    hd = ATTN_HEAD_DIM
    tq = q_ref.shape[1]
    kv = pl.program_id(3)

    @pl.when(kv == 0)
    def _():
        m_scr[...] = jnp.full_like(m_scr, -jnp.inf)
        l_scr[...] = jnp.zeros_like(l_scr)
        acc_scr[...] = jnp.zeros_like(acc_scr)

    q = jnp.concatenate([q_ref[0, :, g * hd:(g + 1) * hd] for g in range(grp)], axis=0)
    s = _dot_nt(q, k_ref[0])
    m_prev = m_scr[...]
    m_new = jnp.maximum(m_prev, jnp.max(s, axis=-1, keepdims=True))
    alpha = jnp.exp(m_prev - m_new)
    p = jnp.exp(s - m_new)
    l_scr[...] = alpha * l_scr[...] + jnp.sum(p, axis=-1, keepdims=True)
    acc_scr[...] = alpha * acc_scr[...] + _dot(p.astype(BF16), v_ref[0].astype(BF16))
    m_scr[...] = m_new

    @pl.when(kv == pl.num_programs(3) - 1)
    def _():
        o = acc_scr[...] / l_scr[...]
        for g in range(grp):
            o_ref[0, :, g * hd:(g + 1) * hd] = o[g * tq:(g + 1) * tq, :].astype(o_ref.dtype)


def _flash_gqa(q3, k3, z3, *, av_col0, tq, tk):
    b, s, _ = q3.shape
    hd = ATTN_HEAD_DIM
    grp = N_ATTN_HEADS // N_KV_HEADS
    vb = av_col0 // hd
    return pl.pallas_call(
        functools.partial(_flash_kernel, grp=grp),
        grid=(b, N_KV_HEADS, s // tq, s // tk),
        in_specs=[
            pl.BlockSpec((1, tq, grp * hd), lambda i, h, qi, ki: (i, qi, h)),
            pl.BlockSpec((1, tk, hd), lambda i, h, qi, ki: (i, ki, h)),
            pl.BlockSpec((1, tk, hd), lambda i, h, qi, ki: (i, ki, vb + h)),
        ],
        out_specs=pl.BlockSpec((1, tq, grp * hd), lambda i, h, qi, ki: (i, qi, h)),
        out_shape=jax.ShapeDtypeStruct((b, s, ATTN_WIDTH), BF16),
        scratch_shapes=[
            pltpu.VMEM((grp * tq, 1), F32),
            pltpu.VMEM((grp * tq, 1), F32),
            pltpu.VMEM((grp * tq, hd), F32),
        ],
        compiler_params=_cparams(("parallel", "parallel", "parallel", "arbitrary")),
        name="gqa_flash",
    )(q3, k3, z3)


def _merge_kernel(hf_ref, hb_ref, mo_ref, ng_ref, ha_ref, gm_ref, ga_ref, wm_ref, wa_ref, y_ref, hm_scr):
    dv = MLSTM_V_DIM

    @pl.when(pl.program_id(1) == 0)
    def _():
        for h in range(N_MLSTM_HEADS):
            sl = slice(h * dv, (h + 1) * dv)
            hm = _rms(hf_ref[:, sl] + hb_ref[:, sl], ng_ref[:, sl])
            hm_scr[:, sl] = (hm * jax.nn.sigmoid(mo_ref[:, sl])).astype(BF16)

    ym = _dot(hm_scr[...], wm_ref[...].astype(BF16))
    ya = _dot(ha_ref[...], wa_ref[...].astype(BF16))
    y_ref[...] = (jax.nn.sigmoid(gm_ref[...]) * ym + jax.nn.sigmoid(ga_ref[...]) * ya).astype(y_ref.dtype)


def _merge(hf, hb, z, ng, ha, wm_all, wa_all, layer, *, mo_col0, gm_col0, ga_col0, tm, tn):
    t, w = hf.shape
    d = wm_all.shape[2]
    return pl.pallas_call(
        _merge_kernel,
        grid=(t // tm, d // tn),
        in_specs=[
            pl.BlockSpec((tm, w), lambda m, j: (m, 0)),
            pl.BlockSpec((tm, w), lambda m, j: (m, 0)),
            pl.BlockSpec((tm, w), lambda m, j: (m, mo_col0 // w)),
            pl.BlockSpec((1, w), lambda m, j: (0, 0)),
            pl.BlockSpec((tm, w), lambda m, j: (m, 0)),
            pl.BlockSpec((tm, tn), lambda m, j: (m, gm_col0 // tn + j)),
            pl.BlockSpec((tm, tn), lambda m, j: (m, ga_col0 // tn + j)),
            pl.BlockSpec((None, w, tn), lambda m, j: (layer, 0, j)),
            pl.BlockSpec((None, w, tn), lambda m, j: (layer, 0, j)),
        ],
        out_specs=pl.BlockSpec((tm, tn), lambda m, j: (m, j)),
        out_shape=jax.ShapeDtypeStruct((t, d), BF16),
        scratch_shapes=[pltpu.VMEM((tm, w), BF16)],
        compiler_params=_cparams(("parallel", "arbitrary")),
        name="branch_merge",
    )(hf, hb, z, ng, ha, z, z, wm_all, wa_all)


def _out_proj_kernel(y_ref, w_ref, x_ref, o_ref):
    o_ref[...] = x_ref[...] + _dot(y_ref[...], w_ref[...].astype(BF16))


def _out_proj(y, w_all, layer, x, *, tm, tn):
    t, d = x.shape
    k = y.shape[1]
    return pl.pallas_call(
        _out_proj_kernel,
        grid=(t // tm, d // tn),
        in_specs=[
            pl.BlockSpec((tm, k), lambda m, j: (m, 0)),
            pl.BlockSpec((None, k, tn), lambda m, j: (layer, 0, j)),
            pl.BlockSpec((tm, tn), lambda m, j: (m, j)),
        ],
        out_specs=pl.BlockSpec((tm, tn), lambda m, j: (m, j)),
        out_shape=jax.ShapeDtypeStruct((t, d), F32),
        compiler_params=_cparams(("parallel", "parallel")),
        name="out_proj_residual",
    )(y, w_all, x)


def _ffn_kernel(*refs, routed):
    if routed:
        x_ref, g_ref, comb_ref, wg_ref, wu_ref, wd_ref, o_ref, h_scr = refs
    else:
        x_ref, g_ref, wg_ref, wu_ref, wd_ref, o_ref, h_scr = refs
    e, f = pl.program_id(1), pl.program_id(2)

    @pl.when((e == 0) & (f == 0))
    def _():
        x = x_ref[...]
        h_scr[...] = _rms(x, g_ref[...]).astype(BF16)
        o_ref[...] = x

    h = h_scr[...]
    a = _dot(h, wg_ref[...].astype(BF16))
    u = _dot(h, wu_ref[...].astype(BF16))
    t = a * jax.nn.sigmoid(a) * u
    if routed:
        t = t * _lane_col(comb_ref[...], e)
    o_ref[...] += _dot(t.astype(BF16), wd_ref[...].astype(BF16))


def _ffn(x, g, wg_all, wu_all, wd_all, j, comb, *, tm, tf):
    t, d = x.shape
    routed = comb is not None
    n_e = wg_all.shape[1] if routed else 1
    ff = wg_all.shape[-1]
    if routed:
        w_in_spec = pl.BlockSpec((None, None, d, tf), lambda m, e, f: (j, e, 0, f))
        w_dn_spec = pl.BlockSpec((None, None, tf, d), lambda m, e, f: (j, e, f, 0))
    else:
        w_in_spec = pl.BlockSpec((None, d, tf), lambda m, e, f: (j, 0, f))
        w_dn_spec = pl.BlockSpec((None, tf, d), lambda m, e, f: (j, f, 0))
    in_specs = [pl.BlockSpec((tm, d), lambda m, e, f: (m, 0)), pl.BlockSpec((1, d), lambda m, e, f: (0, 0))]
    args = [x, g]
    if routed:
        in_specs.append(pl.BlockSpec((tm, LANES), lambda m, e, f: (m, 0)))
        args.append(comb)
    in_specs += [w_in_spec, w_in_spec, w_dn_spec]
    args += [wg_all, wu_all, wd_all]
    return pl.pallas_call(
        functools.partial(_ffn_kernel, routed=routed),
        grid=(t // tm, n_e, ff // tf),
        in_specs=in_specs,
        out_specs=pl.BlockSpec((tm, d), lambda m, e, f: (m, 0)),
        out_shape=jax.ShapeDtypeStruct((t, d), F32),
        scratch_shapes=[pltpu.VMEM((tm, d), BF16)],
        compiler_params=_cparams(("parallel", "arbitrary", "arbitrary")),
        name="moe_swiglu" if routed else "dense_swiglu",
    )(*args)


def _router_kernel(x_ref, g_ref, w_ref, b_ref, comb_ref):
    h = _rms(x_ref[...], g_ref[...])
    logits = jnp.dot(h, w_ref[...], precision=lax.Precision.HIGHEST, preferred_element_type=F32) + b_ref[...]
    lane = lax.broadcasted_iota(jnp.int32, logits.shape, 1)
    logits = jnp.where(lane < N_EXPERTS, logits, -jnp.inf)
    v1 = jnp.max(logits, axis=-1, keepdims=True)
    i1 = jnp.min(jnp.where(logits == v1, lane, LANES), axis=-1, keepdims=True)
    rest = jnp.where(lane == i1, -jnp.inf, logits)
    v2 = jnp.max(rest, axis=-1, keepdims=True)
    i2 = jnp.min(jnp.where(rest == v2, lane, LANES), axis=-1, keepdims=True)
    e2 = jnp.exp(v2 - v1)
    p1 = 1.0 / (1.0 + e2)
    p2 = e2 / (1.0 + e2)
    comb_ref[...] = jnp.where(lane == i1, p1, 0.0) + jnp.where(lane == i2, p2, 0.0)


def _router(x, g, w_pad, b_pad, *, tm):
    t, d = x.shape
    return pl.pallas_call(
        _router_kernel,
        grid=(t // tm,),
        in_specs=[
            pl.BlockSpec((tm, d), lambda m: (m, 0)),
            pl.BlockSpec((1, d), lambda m: (0, 0)),
            pl.BlockSpec((d, LANES), lambda m: (0, 0)),
            pl.BlockSpec((1, LANES), lambda m: (0, 0)),
        ],
        out_specs=pl.BlockSpec((tm, LANES), lambda m: (m, 0)),
        out_shape=jax.ShapeDtypeStruct((t, LANES), F32),
        compiler_params=_cparams(("parallel",)),
        name="moe_router",
    )(x, g, w_pad, b_pad)


def _ple_kernel(x_ref, g_ref, wg_ref, p_ref, wp_ref, o_ref, h_scr, *, tn):
    j = pl.program_id(1)

    @pl.when(j == 0)
    def _():
        h_scr[...] = _rms(x_ref[...], g_ref[...]).astype(BF16)

    gate = jax.nn.sigmoid(_dot(h_scr[...], wg_ref[...].astype(BF16)))
    proj = _dot(p_ref[...].astype(BF16), wp_ref[...].astype(BF16))
    o_ref[...] = x_ref[:, pl.ds(pl.multiple_of(j * tn, tn), tn)] + gate * proj


def _ple(x, g, wg_all, p_all, wp_all, layer, *, tm, tn):
    t, d = x.shape
    pd = p_all.shape[-1]
    return pl.pallas_call(
        functools.partial(_ple_kernel, tn=tn),
        grid=(t // tm, d // tn),
        in_specs=[
            pl.BlockSpec((tm, d), lambda m, j: (m, 0)),
            pl.BlockSpec((1, d), lambda m, j: (0, 0)),
            pl.BlockSpec((None, d, tn), lambda m, j: (layer, 0, j)),
            pl.BlockSpec((None, tm, pd), lambda m, j: (layer, m, 0)),
            pl.BlockSpec((None, pd, tn), lambda m, j: (layer, 0, j)),
        ],
        out_specs=pl.BlockSpec((tm, tn), lambda m, j: (m, j)),
        out_shape=jax.ShapeDtypeStruct((t, d), F32),
        scratch_shapes=[pltpu.VMEM((tm, d), BF16)],
        compiler_params=_cparams(("parallel", "arbitrary")),
        name="ple_gate",
    )(x, g, wg_all, p_all, wp_all)


def _rope_tables(seq):
    rows = seq // GRID_W
    row = jnp.broadcast_to(jnp.arange(rows, dtype=F32)[:, None], (rows, GRID_W)).reshape(seq)
    col = jnp.broadcast_to(jnp.arange(GRID_W, dtype=F32)[None, :], (rows, GRID_W)).reshape(seq)
    axis_dim = ATTN_HEAD_DIM // 2
    inv_freq = ROPE_THETA ** (-jnp.arange(0, axis_dim, 2, dtype=F32) / axis_dim)
    ar, ac = row[:, None] * inv_freq, col[:, None] * inv_freq
    cos_t = jnp.concatenate([jnp.cos(ar), jnp.cos(ar), jnp.cos(ac), jnp.cos(ac)], axis=-1)
    sin_t = jnp.concatenate([-jnp.sin(ar), jnp.sin(ar), -jnp.sin(ac), jnp.sin(ac)], axis=-1)
    return cos_t, sin_t


def kernel(x, p, norm_mix_g, w_in, conv_w, conv_b, b_igate, b_fgate, mlstm_norm_g, q_norm_g, k_norm_g, w_mlstm_up, w_attn_up, w_out, norm_ffn_g, ffn_w_gate, ffn_w_up, ffn_w_down, moe_router, moe_router_b, moe_w_gate, moe_w_up, moe_w_down, norm_ple_g, w_ple_gate, w_ple_proj):
    b, s, d = x.shape
    depth = w_in.shape[0]
    t = b * s
    col = _col_layout(d)
    cos_t, sin_t = _rope_tables(s)
    xt = x.reshape(t, d)
    p_all = p.reshape(depth, t, p.shape[-1])

    tm_big, tm_half = min(1024, t), min(512, t)
    tn = min(512, d)
    tf = 256

    for i in range(depth):
        z = _norm_matmul(xt, norm_mix_g[i][None, :], _permute_w_in(w_in[i], d), tm=tm_big, tn=tn)
        z3 = z.reshape(b, s, col["total"])
        qk3 = _conv_silu(z3, conv_w[i], conv_b[i][None, :], col0=col["qk"], tc=256)
        gate_bias = jnp.concatenate(
            [b_igate[i].reshape(-1), b_fgate[i].reshape(-1), jnp.zeros((LANES - N_GATE_COLS,), F32)])[None, :]
        hf, hb = _mlstm_scan(qk3, z3, gate_bias, v_col0=col["mv"], gate_col0=col["gates"])
        q3, k3 = _qk_prep(z3, cos_t, sin_t, q_norm_g[i][None, :], k_norm_g[i][None, :],
                          aq_col0=col["aq"], ak_col0=col["ak"], ts=min(512, s))
        ha = _flash_gqa(q3, k3, z3, av_col0=col["av"], tq=min(256, s), tk=min(512, s))
        y = _merge(hf.reshape(t, MLSTM_WIDTH), hb.reshape(t, MLSTM_WIDTH), z, mlstm_norm_g[i].reshape(1, MLSTM_WIDTH),
                   ha.reshape(t, ATTN_WIDTH), w_mlstm_up, w_attn_up, i,
                   mo_col0=col["mo"], gm_col0=col["gm"], ga_col0=col["ga"], tm=tm_half, tn=tn)
        xt = _out_proj(y, w_out, i, xt, tm=tm_big, tn=tn)

        j = i // 2
        g_ffn = norm_ffn_g[i][None, :]
        if i % 2 == 0:
            xt = _ffn(xt, g_ffn, ffn_w_gate, ffn_w_up, ffn_w_down, j, None, tm=tm_big, tf=tf)
        else:
            w_pad = jnp.pad(moe_router[j], ((0, 0), (0, LANES - N_EXPERTS)))
            b_pad = jnp.pad(moe_router_b[j], (0, LANES - N_EXPERTS))[None, :]
            comb = _router(xt, g_ffn, w_pad, b_pad, tm=tm_half)
            xt = _ffn(xt, g_ffn, moe_w_gate, moe_w_up, moe_w_down, j, comb, tm=tm_big, tf=tf)

        xt = _ple(xt, norm_ple_g[i][None, :], w_ple_gate, p_all, w_ple_proj, i, tm=tm_big, tn=tn)

    return xt.reshape(b, s, d)
```

```python
import functools

import jax
import jax.numpy as jnp
from jax import lax
from jax.experimental import pallas as pl
from jax.experimental.pallas import tpu as pltpu

F32 = jnp.float32
BF16 = jnp.bfloat16

GRID_W = 64
N_MLSTM_HEADS = 4
MLSTM_QK_DIM = 128
MLSTM_V_DIM = 256
MLSTM_QK_WIDTH = N_MLSTM_HEADS * MLSTM_QK_DIM
MLSTM_WIDTH = N_MLSTM_HEADS * MLSTM_V_DIM
MLSTM_CHUNK = 128
N_DIRS = 2
N_ATTN_HEADS = 8
N_KV_HEADS = 2
ATTN_HEAD_DIM = 128
ATTN_WIDTH = N_ATTN_HEADS * ATTN_HEAD_DIM
KV_WIDTH = N_KV_HEADS * ATTN_HEAD_DIM
ROPE_THETA = 10000.0
N_EXPERTS = 8
EPS = 1e-6

LANES = 128
VMEM_LIMIT_BYTES = 56 * 2**20

N_GATE_COLS = 2 * N_DIRS * N_MLSTM_HEADS


def _col_layout(d_model):
    names = ("qk", "mv", "mo", "aq", "ak", "av", "gm", "ga", "gates")
    widths = (2 * MLSTM_QK_WIDTH, MLSTM_WIDTH, MLSTM_WIDTH, ATTN_WIDTH, KV_WIDTH, KV_WIDTH, d_model, d_model, LANES)
    off, out = 0, {}
    for n, w in zip(names, widths):
        out[n] = off
        off += w
    out["total"] = off
    return out


def _permute_w_in(w, d_model):
    sizes = (MLSTM_QK_WIDTH, MLSTM_QK_WIDTH, MLSTM_WIDTH, N_GATE_COLS // 2, N_GATE_COLS // 2, MLSTM_WIDTH,
             ATTN_WIDTH, KV_WIDTH, KV_WIDTH, d_model, d_model)
    segs, start = [], 0
    for s in sizes:
        segs.append(w[:, start:start + s])
        start += s
    mq, mk, mv, mi, mf, mo, aq, ak, av, gm, ga = segs
    pad = jnp.zeros((w.shape[0], LANES - N_GATE_COLS), w.dtype)
    return jnp.concatenate([mq, mk, mv, mo, aq, ak, av, gm, ga, mi, mf, pad], axis=1).astype(BF16)


def _cparams(semantics):
    return pltpu.CompilerParams(dimension_semantics=semantics, vmem_limit_bytes=VMEM_LIMIT_BYTES)


def _rms(x, g):
    ms = jnp.mean(x * x, axis=-1, keepdims=True)
    return x * lax.rsqrt(ms + EPS) * g


def _dot(a, b):
    return jnp.dot(a, b, preferred_element_type=F32)


def _dot_nt(a, b):
    return lax.dot_general(a, b, (((1,), (1,)), ((), ())), preferred_element_type=F32)


def _dot_tn(a, b):
    return lax.dot_general(a, b, (((0,), (0,)), ((), ())), preferred_element_type=F32)


def _norm_matmul_kernel(x_ref, g_ref, w_ref, o_ref, h_scr):
    @pl.when(pl.program_id(1) == 0)
    def _():
        h_scr[...] = _rms(x_ref[...], g_ref[...]).astype(BF16)

    o_ref[...] = _dot(h_scr[...], w_ref[...])


def _norm_matmul(x, g, w, *, tm, tn):
    t, d = x.shape
    n = w.shape[1]
    return pl.pallas_call(
        _norm_matmul_kernel,
        grid=(t // tm, pl.cdiv(n, tn)),
        in_specs=[
            pl.BlockSpec((tm, d), lambda m, j: (m, 0)),
            pl.BlockSpec((1, d), lambda m, j: (0, 0)),
            pl.BlockSpec((d, tn), lambda m, j: (0, j)),
        ],
        out_specs=pl.BlockSpec((tm, tn), lambda m, j: (m, j)),
        out_shape=jax.ShapeDtypeStruct((t, n), F32),
        scratch_shapes=[pltpu.VMEM((tm, d), BF16)],
        compiler_params=_cparams(("parallel", "arbitrary")),
        name="norm_in_proj",
    )(x, g, w)


def _conv_silu_kernel(z_ref, w_ref, b_ref, o_ref, *, q_blocks, q_scale):
    x = z_ref[0]
    s = x.shape[0]
    row = lax.broadcasted_iota(jnp.int32, x.shape, 0)
    prev = jnp.where(row == 0, 0.0, pltpu.roll(x, 1, 0))
    nxt = jnp.where(row == s - 1, 0.0, pltpu.roll(x, s - 1, 0))
    y = prev * w_ref[0:1, :] + x * w_ref[1:2, :] + nxt * w_ref[2:3, :] + b_ref[...]
    y = y * jax.nn.sigmoid(y)
    scale = jnp.where(pl.program_id(1) < q_blocks, q_scale, 1.0).astype(F32)
    o_ref[0] = (y * scale).astype(o_ref.dtype)


def _conv_silu(z3, conv_w, conv_b, *, col0, tc):
    b, s, _ = z3.shape
    width = conv_w.shape[1]
    kern = functools.partial(_conv_silu_kernel, q_blocks=MLSTM_QK_WIDTH // tc, q_scale=MLSTM_QK_DIM ** -0.5)
    return pl.pallas_call(
        kern,
        grid=(b, width // tc),
        in_specs=[
            pl.BlockSpec((1, s, tc), lambda i, j: (i, 0, col0 // tc + j)),
            pl.BlockSpec((3, tc), lambda i, j: (0, j)),
            pl.BlockSpec((1, tc), lambda i, j: (0, j)),
        ],
        out_specs=pl.BlockSpec((1, s, tc), lambda i, j: (i, 0, j)),
        out_shape=jax.ShapeDtypeStruct((b, s, width), BF16),
        compiler_params=_cparams(("parallel", "parallel")),
        name="mlstm_conv_silu",
    )(z3, conv_w, conv_b)


def _log_sigmoid(x):
    return jnp.minimum(x, 0.0) - jnp.log1p(jnp.exp(-jnp.abs(x)))


def _lane_col(x, idx):
    lane = lax.broadcasted_iota(jnp.int32, x.shape, 1)
    return jnp.sum(jnp.where(lane == idx, x, 0.0), axis=-1, keepdims=True)


def _mlstm_kernel(qkf_ref, qkb_ref, vf_ref, vb_ref, gf_ref, gb_ref, bias_ref, hf_ref, hb_ref, c_scr, m_scr):
    L = MLSTM_CHUNK
    dk, dv, nh = MLSTM_QK_DIM, MLSTM_V_DIM, N_MLSTM_HEADS

    @pl.when(pl.program_id(1) == 0)
    def _():
        c_scr[...] = jnp.zeros_like(c_scr)
        m_scr[...] = jnp.zeros_like(m_scr)

    r_i = lax.broadcasted_iota(jnp.int32, (L, L), 0)
    c_i = lax.broadcasted_iota(jnp.int32, (L, L), 1)
    lane = lax.broadcasted_iota(jnp.int32, (L, LANES), 1)
    ones_blk = jnp.where(lane == 0, 1.0, 0.0).astype(F32)

    dirs = ((qkf_ref, vf_ref, gf_ref, hf_ref), (qkb_ref, vb_ref, gb_ref, hb_ref))
    for d, (qk_ref, v_ref, g_ref, h_ref) in enumerate(dirs):
        mask = (c_i <= r_i) if d == 0 else (c_i >= r_i)
        g = g_ref[0] + bias_ref[...]
        gp = jnp.where(lane < N_GATE_COLS // 2, g, _log_sigmoid(g))
        gp_t = gp.T
        bc = jnp.dot(mask.astype(F32), gp, precision=lax.Precision.HIGHEST, preferred_element_type=F32)
        bc_t = bc.T
        end_row = L - 1 if d == 0 else 0
        for h in range(nh):
            ci = d * nh + h
            ch_i, ch_f = d * nh + h, N_GATE_COLS // 2 + d * nh + h
            i_row = gp_t[ch_i:ch_i + 1, :]
            bc_row = bc_t[ch_f:ch_f + 1, :]
            i_col = _lane_col(gp, ch_i)
            bc_col = _lane_col(bc, ch_f)
            m_prev = m_scr[ci, 0:1, 0:1]
            c_prev = c_scr[ci]

            a_col = bc_col + m_prev
            dm = jnp.where(mask, bc_col - bc_row + i_row, -jnp.inf)
            m_t = jnp.maximum(a_col, jnp.max(dm, axis=-1, keepdims=True))
            w_inter = jnp.exp(a_col - m_t)
            w_intra = jnp.exp(dm - m_t)

            q = qk_ref[0, :, h * dk:(h + 1) * dk]
            k = qk_ref[0, :, MLSTM_QK_WIDTH + h * dk:MLSTM_QK_WIDTH + (h + 1) * dk]
            v_ext = jnp.concatenate([v_ref[0, :, h * dv:(h + 1) * dv], ones_blk], axis=1)
            p = (_dot_nt(q, k) * w_intra).astype(BF16)
            num_ext = w_inter * _dot(q, c_prev.astype(BF16)) + _dot(p, v_ext.astype(BF16))
            den = num_ext[:, dv:dv + 1]
            h_ref[0, :, h * dv:(h + 1) * dv] = num_ext[:, :dv] / jnp.maximum(jnp.abs(den), jnp.exp(-m_t))

            b_end = bc_col[end_row:end_row + 1, :]
            g_col = b_end - bc_col + i_col
            m_new = jnp.maximum(b_end + m_prev, jnp.max(g_col, axis=0, keepdims=True))
            decay = jnp.exp(b_end + m_prev - m_new)
            ws = jnp.exp(g_col - m_new)
            c_scr[ci] = decay * c_prev + _dot_tn(k, (ws * v_ext).astype(BF16))
            m_scr[ci] = jnp.broadcast_to(m_new, m_scr.shape[1:])


def _mlstm_scan(qk3, z3, gate_bias, *, v_col0, gate_col0):
    b, s, _ = qk3.shape
    L = MLSTM_CHUNK
    nc = s // L
    n_chain = N_DIRS * N_MLSTM_HEADS
    vb, gb = v_col0 // MLSTM_WIDTH, gate_col0 // LANES
    qk_w = 2 * MLSTM_QK_WIDTH
    return pl.pallas_call(
        _mlstm_kernel,
        grid=(b, nc),
        in_specs=[
            pl.BlockSpec((1, L, qk_w), lambda i, c: (i, c, 0)),
            pl.BlockSpec((1, L, qk_w), lambda i, c: (i, nc - 1 - c, 0)),
            pl.BlockSpec((1, L, MLSTM_WIDTH), lambda i, c: (i, c, vb)),
            pl.BlockSpec((1, L, MLSTM_WIDTH), lambda i, c: (i, nc - 1 - c, vb)),
            pl.BlockSpec((1, L, LANES), lambda i, c: (i, c, gb)),
            pl.BlockSpec((1, L, LANES), lambda i, c: (i, nc - 1 - c, gb)),
            pl.BlockSpec((1, LANES), lambda i, c: (0, 0)),
        ],
        out_specs=[
            pl.BlockSpec((1, L, MLSTM_WIDTH), lambda i, c: (i, c, 0)),
            pl.BlockSpec((1, L, MLSTM_WIDTH), lambda i, c: (i, nc - 1 - c, 0)),
        ],
        out_shape=[jax.ShapeDtypeStruct((b, s, MLSTM_WIDTH), F32)] * 2,
        scratch_shapes=[
            pltpu.VMEM((n_chain, MLSTM_QK_DIM, MLSTM_V_DIM + LANES), F32),
            pltpu.VMEM((n_chain, 8, LANES), F32),
        ],
        compiler_params=_cparams(("parallel", "arbitrary")),
        name="mlstm_scan",
    )(qk3, qk3, z3, z3, z3, z3, gate_bias)


LOG2_E = 1.4426950408889634


def _qk_prep_kernel(aq_ref, ak_ref, av_ref, cos_ref, sin_ref, gq_ref, gk_ref, q_ref, k_ref, vt_ref):
    cos, sin = cos_ref[...], sin_ref[...]
    hd = ATTN_HEAD_DIM
    lane = lax.broadcasted_iota(jnp.int32, cos.shape, 1)
    first_half = (lane % (hd // 2)) < (hd // 4)

    def prep(x, g, scale):
        n = _rms(x, g)
        partner = jnp.where(first_half, pltpu.roll(n, hd - hd // 4, 1), pltpu.roll(n, hd // 4, 1))
        return ((n * cos + partner * sin) * scale).astype(BF16)

    for h in range(N_ATTN_HEADS):
        q_ref[0, :, h * hd:(h + 1) * hd] = prep(aq_ref[0, :, h * hd:(h + 1) * hd], gq_ref[...], hd ** -0.5 * LOG2_E)
    for h in range(N_KV_HEADS):
        k_ref[0, :, h * hd:(h + 1) * hd] = prep(ak_ref[0, :, h * hd:(h + 1) * hd], gk_ref[...], 1.0)
    vt_ref[0] = av_ref[0].T.astype(BF16)


def _qk_prep(z3, cos_t, sin_t, gq, gk, *, aq_col0, ak_col0, av_col0, ts):
    b, s, _ = z3.shape
    return pl.pallas_call(
        _qk_prep_kernel,
        grid=(b, s // ts),
        in_specs=[
            pl.BlockSpec((1, ts, ATTN_WIDTH), lambda i, j: (i, j, aq_col0 // ATTN_WIDTH)),
            pl.BlockSpec((1, ts, KV_WIDTH), lambda i, j: (i, j, ak_col0 // KV_WIDTH)),
            pl.BlockSpec((1, ts, KV_WIDTH), lambda i, j: (i, j, av_col0 // KV_WIDTH)),
            pl.BlockSpec((ts, ATTN_HEAD_DIM), lambda i, j: (j, 0)),
            pl.BlockSpec((ts, ATTN_HEAD_DIM), lambda i, j: (j, 0)),
            pl.BlockSpec((1, ATTN_HEAD_DIM), lambda i, j: (0, 0)),
            pl.BlockSpec((1, ATTN_HEAD_DIM), lambda i, j: (0, 0)),
        ],
        out_specs=[
            pl.BlockSpec((1, ts, ATTN_WIDTH), lambda i, j: (i, j, 0)),
            pl.BlockSpec((1, ts, KV_WIDTH), lambda i, j: (i, j, 0)),
            pl.BlockSpec((1, KV_WIDTH, ts), lambda i, j: (i, 0, j)),
        ],
        out_shape=[jax.ShapeDtypeStruct((b, s, ATTN_WIDTH), BF16), jax.ShapeDtypeStruct((b, s, KV_WIDTH), BF16),
                   jax.ShapeDtypeStruct((b, KV_WIDTH, s), BF16)],
        compiler_params=_cparams(("parallel", "parallel")),
        name="attn_qk_norm_rope",
    )(z3, z3, z3, cos_t, sin_t, gq, gk)


def _flash_kernel(q_ref, k_ref, vt_ref, o_ref, *, grp, tk):
    hd = ATTN_HEAD_DIM
    tq = q_ref.shape[1]
    s_len = k_ref.shape[1]
    rows = grp * tq
    q = jnp.concatenate([q_ref[0, :, g * hd:(g + 1) * hd] for g in range(grp)], axis=0)
    m = jnp.full((1, rows), -jnp.inf, F32)
    l = jnp.zeros((1, rows), F32)
    acc = jnp.zeros((hd, rows), F32)
    for c in range(s_len // tk):
        st = _dot_nt(k_ref[0, c * tk:(c + 1) * tk, :], q)
        m_new = jnp.maximum(m, jnp.max(st, axis=0, keepdims=True))
        alpha = jnp.exp2(m - m_new)
        p = jnp.exp2(st - m_new)
        l = alpha * l + jnp.sum(p, axis=0, keepdims=True)
        acc = alpha * acc + _dot(vt_ref[0, :, c * tk:(c + 1) * tk], p.astype(BF16))
        m = m_new
    o = (acc / l).T
    for g in range(grp):
        o_ref[0, :, g * hd:(g + 1) * hd] = o[g * tq:(g + 1) * tq, :].astype(o_ref.dtype)


def _flash_gqa(q3, k3, vt3, *, tq, tk):
    b, s, _ = q3.shape
    hd = ATTN_HEAD_DIM
    grp = N_ATTN_HEADS // N_KV_HEADS
    return pl.pallas_call(
        functools.partial(_flash_kernel, grp=grp, tk=tk),
        grid=(b, N_KV_HEADS, s // tq),
        in_specs=[
            pl.BlockSpec((1, tq, grp * hd), lambda i, h, qi: (i, qi, h)),
            pl.BlockSpec((1, s, hd), lambda i, h, qi: (i, 0, h)),
            pl.BlockSpec((1, hd, s), lambda i, h, qi: (i, h, 0)),
        ],
        out_specs=pl.BlockSpec((1, tq, grp * hd), lambda i, h, qi: (i, qi, h)),
        out_shape=jax.ShapeDtypeStruct((b, s, ATTN_WIDTH), BF16),
        compiler_params=_cparams(("parallel", "parallel", "parallel")),
        name="gqa_flash",
    )(q3, k3, vt3)


def _merge_kernel(hf_ref, hb_ref, mo_ref, ng_ref, ha_ref, gm_ref, ga_ref, wm_ref, wa_ref, y_ref, hm_scr):
    dv = MLSTM_V_DIM

    @pl.when(pl.program_id(1) == 0)
    def _():
        for h in range(N_MLSTM_HEADS):
            sl = slice(h * dv, (h + 1) * dv)
            hm = _rms(hf_ref[:, sl] + hb_ref[:, sl], ng_ref[:, sl])
            hm_scr[:, sl] = (hm * jax.nn.sigmoid(mo_ref[:, sl])).astype(BF16)

    ym = _dot(hm_scr[...], wm_ref[...].astype(BF16))
    ya = _dot(ha_ref[...], wa_ref[...].astype(BF16))
    y_ref[...] = (jax.nn.sigmoid(gm_ref[...]) * ym + jax.nn.sigmoid(ga_ref[...]) * ya).astype(y_ref.dtype)


def _merge(hf, hb, z, ng, ha, wm_all, wa_all, layer, *, mo_col0, gm_col0, ga_col0, tm, tn):
    t, w = hf.shape
    d = wm_all.shape[2]
    return pl.pallas_call(
        _merge_kernel,
        grid=(t // tm, d // tn),
        in_specs=[
            pl.BlockSpec((tm, w), lambda m, j: (m, 0)),
            pl.BlockSpec((tm, w), lambda m, j: (m, 0)),
            pl.BlockSpec((tm, w), lambda m, j: (m, mo_col0 // w)),
            pl.BlockSpec((1, w), lambda m, j: (0, 0)),
            pl.BlockSpec((tm, w), lambda m, j: (m, 0)),
            pl.BlockSpec((tm, tn), lambda m, j: (m, gm_col0 // tn + j)),
            pl.BlockSpec((tm, tn), lambda m, j: (m, ga_col0 // tn + j)),
            pl.BlockSpec((None, w, tn), lambda m, j: (layer, 0, j)),
            pl.BlockSpec((None, w, tn), lambda m, j: (layer, 0, j)),
        ],
        out_specs=pl.BlockSpec((tm, tn), lambda m, j: (m, j)),
        out_shape=jax.ShapeDtypeStruct((t, d), BF16),
        scratch_shapes=[pltpu.VMEM((tm, w), BF16)],
        compiler_params=_cparams(("parallel", "arbitrary")),
        name="branch_merge",
    )(hf, hb, z, ng, ha, z, z, wm_all, wa_all)


def _out_proj_kernel(y_ref, w_ref, x_ref, o_ref):
    o_ref[...] = x_ref[...] + _dot(y_ref[...], w_ref[...].astype(BF16))


def _out_proj(y, w_all, layer, x, *, tm, tn):
    t, d = x.shape
    k = y.shape[1]
    return pl.pallas_call(
        _out_proj_kernel,
        grid=(t // tm, d // tn),
        in_specs=[
            pl.BlockSpec((tm, k), lambda m, j: (m, 0)),
            pl.BlockSpec((None, k, tn), lambda m, j: (layer, 0, j)),
            pl.BlockSpec((tm, tn), lambda m, j: (m, j)),
        ],
        out_specs=pl.BlockSpec((tm, tn), lambda m, j: (m, j)),
        out_shape=jax.ShapeDtypeStruct((t, d), F32),
        compiler_params=_cparams(("parallel", "parallel")),
        name="out_proj_residual",
    )(y, w_all, x)


def _swiglu_partial(h, wg_ref, wu_ref, wd_ref):
    a = _dot(h, wg_ref[...].astype(BF16))
    u = _dot(h, wu_ref[...].astype(BF16))
    t = a * jax.nn.sigmoid(a) * u
    return _dot(t.astype(BF16), wd_ref[...].astype(BF16))


def _ffn_kernel(x_ref, g_ref, wg_ref, wu_ref, wd_ref, o_ref, h_scr):
    @pl.when(pl.program_id(1) == 0)
    def _():
        x = x_ref[...]
        h_scr[...] = _rms(x, g_ref[...]).astype(BF16)
        o_ref[...] = x

    o_ref[...] += _swiglu_partial(h_scr[...], wg_ref, wu_ref, wd_ref)


def _ffn(x, g, wg_all, wu_all, wd_all, j, *, tm, tf):
    t, d = x.shape
    ff = wg_all.shape[-1]
    w_in_spec = pl.BlockSpec((None, d, tf), lambda m, f: (j, 0, f))
    return pl.pallas_call(
        _ffn_kernel,
        grid=(t // tm, ff // tf),
        in_specs=[
            pl.BlockSpec((tm, d), lambda m, f: (m, 0)),
            pl.BlockSpec((1, d), lambda m, f: (0, 0)),
            w_in_spec,
            w_in_spec,
            pl.BlockSpec((None, tf, d), lambda m, f: (j, f, 0)),
        ],
        out_specs=pl.BlockSpec((tm, d), lambda m, f: (m, 0)),
        out_shape=jax.ShapeDtypeStruct((t, d), F32),
        scratch_shapes=[pltpu.VMEM((tm, d), BF16)],
        compiler_params=_cparams(("parallel", "arbitrary")),
        name="dense_swiglu",
    )(x, g, wg_all, wu_all, wd_all)


MOE_ROW_TILE = 1024
MOE_SUB_TILE = 256
META_E1, META_E2, META_P1, META_P2, META_R1, META_R2 = range(6)


def _router_kernel(x_ref, g_ref, w_ref, b_ref, meta_ref, cnt_ref, carry_scr):
    @pl.when(pl.program_id(0) == 0)
    def _():
        carry_scr[...] = jnp.zeros_like(carry_scr)

    h = _rms(x_ref[...], g_ref[...])
    logits = jnp.dot(h, w_ref[...], precision=lax.Precision.HIGHEST, preferred_element_type=F32) + b_ref[...]
    tb = logits.shape[0]
    lane = lax.broadcasted_iota(jnp.int32, logits.shape, 1)
    logits = jnp.where(lane < N_EXPERTS, logits, -jnp.inf)
    v1 = jnp.max(logits, axis=-1, keepdims=True)
    i1 = jnp.min(jnp.where(logits == v1, lane, LANES), axis=-1, keepdims=True)
    rest = jnp.where(lane == i1, -jnp.inf, logits)
    v2 = jnp.max(rest, axis=-1, keepdims=True)
    i2 = jnp.min(jnp.where(rest == v2, lane, LANES), axis=-1, keepdims=True)
    e2 = jnp.exp(v2 - v1)
    p1 = 1.0 / (1.0 + e2)
    p2 = e2 / (1.0 + e2)
    sel = ((lane == i1) | (lane == i2)).astype(F32)
    r_i = lax.broadcasted_iota(jnp.int32, (tb, tb), 0)
    c_i = lax.broadcasted_iota(jnp.int32, (tb, tb), 1)
    earlier = (c_i < r_i).astype(BF16)
    rank = carry_scr[0:1, :] + _dot(earlier, sel.astype(BF16))
    r1 = jnp.sum(jnp.where(lane == i1, rank, 0.0), axis=-1, keepdims=True)
    r2 = jnp.sum(jnp.where(lane == i2, rank, 0.0), axis=-1, keepdims=True)
    meta = jnp.zeros_like(logits)
    for idx, val in ((META_E1, i1.astype(F32)), (META_E2, i2.astype(F32)), (META_P1, p1), (META_P2, p2),
                     (META_R1, r1), (META_R2, r2)):
        meta = jnp.where(lane == idx, val, meta)
    meta_ref[...] = meta
    total = carry_scr[0:1, :] + jnp.sum(sel, axis=0, keepdims=True)
    carry_scr[...] = jnp.broadcast_to(total, carry_scr.shape)
    cnt_ref[...] = jnp.broadcast_to(total, cnt_ref.shape)


def _router(x, g, w_pad, b_pad, *, tm):
    t, d = x.shape
    return pl.pallas_call(
        _router_kernel,
        grid=(t // tm,),
        in_specs=[
            pl.BlockSpec((tm, d), lambda m: (m, 0)),
            pl.BlockSpec((1, d), lambda m: (0, 0)),
            pl.BlockSpec((d, LANES), lambda m: (0, 0)),
            pl.BlockSpec((1, LANES), lambda m: (0, 0)),
        ],
        out_specs=[pl.BlockSpec((tm, LANES), lambda m: (m, 0)), pl.BlockSpec((8, LANES), lambda m: (0, 0))],
        out_shape=[jax.ShapeDtypeStruct((t, LANES), F32), jax.ShapeDtypeStruct((8, LANES), F32)],
        scratch_shapes=[pltpu.VMEM((8, LANES), F32)],
        compiler_params=_cparams(("arbitrary",)),
        name="moe_router",
    )(x, g, w_pad, b_pad)


def _moe_plan(meta, cnt, t):
    tile, sub = MOE_ROW_TILE, MOE_SUB_TILE
    n_tiles = 2 * t // tile + N_EXPERTS
    e1 = meta[:, META_E1].astype(jnp.int32)
    e2 = meta[:, META_E2].astype(jnp.int32)
    r1 = meta[:, META_R1].astype(jnp.int32)
    r2 = meta[:, META_R2].astype(jnp.int32)
    counts = cnt[0, :N_EXPERTS].astype(jnp.int32)
    tiles_e = (counts + tile - 1) // tile
    tile_end = jnp.cumsum(tiles_e)
    tile_start = tile_end - tiles_e
    row_start = tile_start * tile
    dest = jnp.concatenate([row_start[e1] + r1, row_start[e2] + r2])
    m = jnp.arange(n_tiles, dtype=jnp.int32)
    live = m < tile_end[-1]
    last_live = jnp.maximum(tile_end[-1] - 1, 0)
    m_eff = jnp.where(live, m, last_live)
    tile_expert = jnp.minimum(jnp.searchsorted(tile_end, m_eff, side="right"), N_EXPERTS - 1).astype(jnp.int32)
    rows_left = counts[tile_expert] - (m_eff - tile_start[tile_expert]) * tile
    n_sub = jnp.where(live, jnp.clip((rows_left + sub - 1) // sub, 0, tile // sub), 0).astype(jnp.int32)
    block = m_eff.astype(jnp.int32)
    return dest.astype(jnp.int32), tile_expert, n_sub, block, n_tiles


def _row_copy_loop(n_rows, make_copies):
    def start(r, carry):
        for cp in make_copies(r):
            cp.start()
        return carry

    def wait(r, carry):
        for cp in make_copies(r):
            cp.wait()
        return carry

    lax.fori_loop(0, n_rows, start, 0)
    lax.fori_loop(0, n_rows, wait, 0)


def _dispatch_kernel(dest_ref, x_hbm, xs_in_hbm, xs_hbm, sem, *, tb, t):
    del xs_in_hbm
    base = pl.program_id(0) * tb

    def copies(r):
        tok = base + r
        src = x_hbm.at[pl.ds(tok, 1)]
        return [pltpu.make_async_copy(src, xs_hbm.at[pl.ds(dest_ref[k * t + tok], 1)], sem) for k in range(2)]

    _row_copy_loop(tb, copies)


def _dispatch(x, dest, n_rows, *, tb):
    t, d = x.shape
    xs0 = jnp.zeros((n_rows, d), F32)
    return pl.pallas_call(
        functools.partial(_dispatch_kernel, tb=tb, t=t),
        grid_spec=pltpu.PrefetchScalarGridSpec(
            num_scalar_prefetch=1,
            grid=(t // tb,),
            in_specs=[pl.BlockSpec(memory_space=pl.ANY), pl.BlockSpec(memory_space=pl.ANY)],
            out_specs=pl.BlockSpec(memory_space=pl.ANY),
            scratch_shapes=[pltpu.SemaphoreType.DMA(())],
        ),
        out_shape=jax.ShapeDtypeStruct((n_rows, d), F32),
        input_output_aliases={2: 0},
        compiler_params=_cparams(("arbitrary",)),
        name="moe_dispatch",
    )(dest, x, xs0)


def _expert_kernel(te_ref, ns_ref, blk_ref, x_ref, g_ref, wg_ref, wu_ref, wd_ref, o_ref, h_scr, *, n_f):
    del te_ref, blk_ref
    m, f = pl.program_id(0), pl.program_id(1)
    sub = MOE_SUB_TILE

    @pl.when(f == 0)
    def _():
        h_scr[...] = _rms(x_ref[...], g_ref[...]).astype(BF16)
        o_ref[...] = jnp.zeros_like(o_ref)

    for sb in range(o_ref.shape[0] // sub):
        @pl.when(sb < ns_ref[m])
        def _():
            rows = slice(sb * sub, (sb + 1) * sub)
            o_ref[rows, :] += _swiglu_partial(h_scr[rows, :], wg_ref, wu_ref, wd_ref)


def _experts(xs, g, wg_all, wu_all, wd_all, j, tile_expert, n_sub, block, n_tiles, *, tf):
    n_rows, d = xs.shape
    tile = MOE_ROW_TILE
    ff = wg_all.shape[-1]
    n_f = ff // tf

    def f_eff(m, f, ns):
        return jnp.where(ns[m] > 0, f, n_f - 1)

    w_in_spec = pl.BlockSpec((None, None, d, tf), lambda m, f, te, ns, blk: (j, te[m], 0, f_eff(m, f, ns)))
    w_dn_spec = pl.BlockSpec((None, None, tf, d), lambda m, f, te, ns, blk: (j, te[m], f_eff(m, f, ns), 0))
    return pl.pallas_call(
        functools.partial(_expert_kernel, n_f=n_f),
        grid_spec=pltpu.PrefetchScalarGridSpec(
            num_scalar_prefetch=3,
            grid=(n_tiles, n_f),
            in_specs=[
                pl.BlockSpec((tile, d), lambda m, f, te, ns, blk: (blk[m], 0)),
                pl.BlockSpec((1, d), lambda m, f, te, ns, blk: (0, 0)),
                w_in_spec,
                w_in_spec,
                w_dn_spec,
            ],
            out_specs=pl.BlockSpec((tile, d), lambda m, f, te, ns, blk: (m, 0)),
            scratch_shapes=[pltpu.VMEM((tile, d), BF16)],
        ),
        out_shape=jax.ShapeDtypeStruct((n_rows, d), F32),
        compiler_params=_cparams(("arbitrary", "arbitrary")),
        name="moe_experts",
    )(tile_expert, n_sub, block, xs, g, wg_all, wu_all, wd_all)


def _combine_kernel(dest_ref, x_ref, meta_ref, ys_hbm, o_ref, buf, sem, *, tb, t):
    base = pl.program_id(0) * tb

    def copies(r):
        tok = base + r
        return [pltpu.make_async_copy(ys_hbm.at[pl.ds(dest_ref[k * t + tok], 1)], buf.at[k, pl.ds(r, 1)], sem)
                for k in range(2)]

    _row_copy_loop(tb, copies)
    meta = meta_ref[...]
    o_ref[...] = x_ref[...] + _lane_col(meta, META_P1) * buf[0] + _lane_col(meta, META_P2) * buf[1]


def _combine(x, meta, ys, dest, *, tb):
    t, d = x.shape
    return pl.pallas_call(
        functools.partial(_combine_kernel, tb=tb, t=t),
        grid_spec=pltpu.PrefetchScalarGridSpec(
            num_scalar_prefetch=1,
            grid=(t // tb,),
            in_specs=[
                pl.BlockSpec((tb, d), lambda i, dst: (i, 0)),
                pl.BlockSpec((tb, LANES), lambda i, dst: (i, 0)),
                pl.BlockSpec(memory_space=pl.ANY),
            ],
            out_specs=pl.BlockSpec((tb, d), lambda i, dst: (i, 0)),
            scratch_shapes=[pltpu.VMEM((2, tb, d), F32), pltpu.SemaphoreType.DMA(())],
        ),
        out_shape=jax.ShapeDtypeStruct((t, d), F32),
        compiler_params=_cparams(("arbitrary",)),
        name="moe_combine",
    )(dest, x, meta, ys)


def _moe(x, g, router_w, router_b, wg_all, wu_all, wd_all, j, *, tb_route, tf):
    t, d = x.shape
    w_pad = jnp.pad(router_w, ((0, 0), (0, LANES - N_EXPERTS)))
    b_pad = jnp.pad(router_b, (0, LANES - N_EXPERTS))[None, :]
    meta, cnt = _router(x, g, w_pad, b_pad, tm=tb_route)
    dest, tile_expert, n_sub, block, n_tiles = _moe_plan(meta, cnt, t)
    n_rows = n_tiles * MOE_ROW_TILE
    xs = _dispatch(x, dest, n_rows, tb=min(256, t))
    ys = _experts(xs, g, wg_all, wu_all, wd_all, j, tile_expert, n_sub, block, n_tiles, tf=tf)
    return _combine(x, meta, ys, dest, tb=min(256, t))


def _ple_kernel(x_ref, g_ref, wg_ref, p_ref, wp_ref, o_ref, h_scr, *, tn):
    j = pl.program_id(1)

    @pl.when(j == 0)
    def _():
        h_scr[...] = _rms(x_ref[...], g_ref[...]).astype(BF16)

    gate = jax.nn.sigmoid(_dot(h_scr[...], wg_ref[...].astype(BF16)))
    proj = _dot(p_ref[...].astype(BF16), wp_ref[...].astype(BF16))
    o_ref[...] = x_ref[:, pl.ds(pl.multiple_of(j * tn, tn), tn)] + gate * proj


def _ple(x, g, wg_all, p_all, wp_all, layer, *, tm, tn):
    t, d = x.shape
    pd = p_all.shape[-1]
    return pl.pallas_call(
        functools.partial(_ple_kernel, tn=tn),
        grid=(t // tm, d // tn),
        in_specs=[
            pl.BlockSpec((tm, d), lambda m, j: (m, 0)),
            pl.BlockSpec((1, d), lambda m, j: (0, 0)),
            pl.BlockSpec((None, d, tn), lambda m, j: (layer, 0, j)),
            pl.BlockSpec((None, tm, pd), lambda m, j: (layer, m, 0)),
            pl.BlockSpec((None, pd, tn), lambda m, j: (layer, 0, j)),
        ],
        out_specs=pl.BlockSpec((tm, tn), lambda m, j: (m, j)),
        out_shape=jax.ShapeDtypeStruct((t, d), F32),
        scratch_shapes=[pltpu.VMEM((tm, d), BF16)],
        compiler_params=_cparams(("parallel", "arbitrary")),
        name="ple_gate",
    )(x, g, wg_all, p_all, wp_all)


def _rope_tables(seq):
    rows = seq // GRID_W
    row = jnp.broadcast_to(jnp.arange(rows, dtype=F32)[:, None], (rows, GRID_W)).reshape(seq)
    col = jnp.broadcast_to(jnp.arange(GRID_W, dtype=F32)[None, :], (rows, GRID_W)).reshape(seq)
    axis_dim = ATTN_HEAD_DIM // 2
    inv_freq = ROPE_THETA ** (-jnp.arange(0, axis_dim, 2, dtype=F32) / axis_dim)
    ar, ac = row[:, None] * inv_freq, col[:, None] * inv_freq
    cos_t = jnp.concatenate([jnp.cos(ar), jnp.cos(ar), jnp.cos(ac), jnp.cos(ac)], axis=-1)
    sin_t = jnp.concatenate([-jnp.sin(ar), jnp.sin(ar), -jnp.sin(ac), jnp.sin(ac)], axis=-1)
    return cos_t, sin_t


def kernel(x, p, norm_mix_g, w_in, conv_w, conv_b, b_igate, b_fgate, mlstm_norm_g, q_norm_g, k_norm_g, w_mlstm_up, w_attn_up, w_out, norm_ffn_g, ffn_w_gate, ffn_w_up, ffn_w_down, moe_router, moe_router_b, moe_w_gate, moe_w_up, moe_w_down, norm_ple_g, w_ple_gate, w_ple_proj):
    b, s, d = x.shape
    depth = w_in.shape[0]
    t = b * s
    col = _col_layout(d)
    cos_t, sin_t = _rope_tables(s)
    xt = x.reshape(t, d)
    p_all = p.reshape(depth, t, p.shape[-1])

    tm_big, tm_half = min(1024, t), min(512, t)
    tn = min(512, d)
    tf = 256

    for i in range(depth):
        z = _norm_matmul(xt, norm_mix_g[i][None, :], _permute_w_in(w_in[i], d), tm=tm_big, tn=tn)
        z3 = z.reshape(b, s, col["total"])
        qk3 = _conv_silu(z3, conv_w[i], conv_b[i][None, :], col0=col["qk"], tc=256)
        gate_bias = jnp.concatenate(
            [b_igate[i].reshape(-1), b_fgate[i].reshape(-1), jnp.zeros((LANES - N_GATE_COLS,), F32)])[None, :]
        hf, hb = _mlstm_scan(qk3, z3, gate_bias, v_col0=col["mv"], gate_col0=col["gates"])
        q3, k3, vt3 = _qk_prep(z3, cos_t, sin_t, q_norm_g[i][None, :], k_norm_g[i][None, :],
                               aq_col0=col["aq"], ak_col0=col["ak"], av_col0=col["av"], ts=min(512, s))
        ha = _flash_gqa(q3, k3, vt3, tq=min(256, s), tk=min(512, s))
        y = _merge(hf.reshape(t, MLSTM_WIDTH), hb.reshape(t, MLSTM_WIDTH), z, mlstm_norm_g[i].reshape(1, MLSTM_WIDTH),
                   ha.reshape(t, ATTN_WIDTH), w_mlstm_up, w_attn_up, i,
                   mo_col0=col["mo"], gm_col0=col["gm"], ga_col0=col["ga"], tm=tm_half, tn=tn)
        xt = _out_proj(y, w_out, i, xt, tm=tm_big, tn=tn)

        j = i // 2
        g_ffn = norm_ffn_g[i][None, :]
        if i % 2 == 0:
            xt = _ffn(xt, g_ffn, ffn_w_gate, ffn_w_up, ffn_w_down, j, tm=tm_big, tf=tf)
        else:
            xt = _moe(xt, g_ffn, moe_router[j], moe_router_b[j], moe_w_gate, moe_w_up, moe_w_down, j,
                      tb_route=tm_half, tf=tf)

        xt = _ple(xt, norm_ple_g[i][None, :], w_ple_gate, p_all, w_ple_proj, i, tm=tm_big, tn=tn)

    return xt.reshape(b, s, d)
```

```python
import functools

import jax
import jax.numpy as jnp
from jax import lax
from jax.experimental import pallas as pl
from jax.experimental.pallas import tpu as pltpu

F32 = jnp.float32
BF16 = jnp.bfloat16

GRID_W = 64
N_MLSTM_HEADS = 4
MLSTM_QK_DIM = 128
MLSTM_V_DIM = 256
MLSTM_QK_WIDTH = N_MLSTM_HEADS * MLSTM_QK_DIM
MLSTM_WIDTH = N_MLSTM_HEADS * MLSTM_V_DIM
MLSTM_CHUNK = 128
N_DIRS = 2
N_ATTN_HEADS = 8
N_KV_HEADS = 2
ATTN_HEAD_DIM = 128
ATTN_WIDTH = N_ATTN_HEADS * ATTN_HEAD_DIM
KV_WIDTH = N_KV_HEADS * ATTN_HEAD_DIM
ROPE_THETA = 10000.0
N_EXPERTS = 8
EPS = 1e-6

LANES = 128
VMEM_LIMIT_BYTES = 56 * 2**20

N_GATE_COLS = 2 * N_DIRS * N_MLSTM_HEADS


def _col_layout(d_model):
    names = ("qk", "mv", "mo", "aq", "ak", "av", "gm", "ga", "gates")
    widths = (2 * MLSTM_QK_WIDTH, MLSTM_WIDTH, MLSTM_WIDTH, ATTN_WIDTH, KV_WIDTH, KV_WIDTH, d_model, d_model, LANES)
    off, out = 0, {}
    for n, w in zip(names, widths):
        out[n] = off
        off += w
    out["total"] = off
    return out


N_HEAD_COLS = 2 * MLSTM_QK_WIDTH + MLSTM_WIDTH


def _tail_w_in(w):
    gates = w[:, N_HEAD_COLS:N_HEAD_COLS + N_GATE_COLS]
    pad = jnp.zeros((w.shape[0], LANES - N_GATE_COLS), w.dtype)
    return jnp.concatenate([w[:, N_HEAD_COLS + N_GATE_COLS:], gates, pad], axis=1).astype(BF16)


def _cparams(semantics):
    return pltpu.CompilerParams(dimension_semantics=semantics, vmem_limit_bytes=VMEM_LIMIT_BYTES)


def _rms(x, g):
    ms = jnp.mean(x * x, axis=-1, keepdims=True)
    return x * lax.rsqrt(ms + EPS) * g


def _dot(a, b):
    return jnp.dot(a, b, preferred_element_type=F32)


def _dot_nt(a, b):
    return lax.dot_general(a, b, (((1,), (1,)), ((), ())), preferred_element_type=F32)


def _dot_tn(a, b):
    return lax.dot_general(a, b, (((0,), (0,)), ((), ())), preferred_element_type=F32)


def _norm_matmul_kernel(x_ref, g_ref, wh_ref, wt_ref, o_ref, h_scr, *, head_blocks):
    j = pl.program_id(1)

    @pl.when(j == 0)
    def _():
        h_scr[...] = _rms(x_ref[...], g_ref[...]).astype(BF16)

    @pl.when(j < head_blocks)
    def _():
        o_ref[...] = _dot(h_scr[...], wh_ref[...].astype(BF16))

    @pl.when(j >= head_blocks)
    def _():
        o_ref[...] = _dot(h_scr[...], wt_ref[...])


def _norm_matmul(x, g, w_all, layer, w_tail, *, tm, tn):
    t, d = x.shape
    head_blocks = N_HEAD_COLS // tn
    tail_blocks = pl.cdiv(w_tail.shape[1], tn)
    return pl.pallas_call(
        functools.partial(_norm_matmul_kernel, head_blocks=head_blocks),
        grid=(t // tm, head_blocks + tail_blocks),
        in_specs=[
            pl.BlockSpec((tm, d), lambda m, j: (m, 0)),
            pl.BlockSpec((1, d), lambda m, j: (0, 0)),
            pl.BlockSpec((None, d, tn), lambda m, j: (layer, 0, jnp.minimum(j, head_blocks - 1))),
            pl.BlockSpec((d, tn), lambda m, j: (0, jnp.maximum(j - head_blocks, 0))),
        ],
        out_specs=pl.BlockSpec((tm, tn), lambda m, j: (m, j)),
        out_shape=jax.ShapeDtypeStruct((t, N_HEAD_COLS + w_tail.shape[1]), F32),
        scratch_shapes=[pltpu.VMEM((tm, d), BF16)],
        compiler_params=_cparams(("parallel", "arbitrary")),
        name="norm_in_proj",
    )(x, g, w_all, w_tail)


def _conv_silu_kernel(z_ref, w_ref, b_ref, o_ref, *, q_blocks, q_scale):
    x = z_ref[0]
    s = x.shape[0]
    row = lax.broadcasted_iota(jnp.int32, x.shape, 0)
    prev = jnp.where(row == 0, 0.0, pltpu.roll(x, 1, 0))
    nxt = jnp.where(row == s - 1, 0.0, pltpu.roll(x, s - 1, 0))
    y = prev * w_ref[0:1, :] + x * w_ref[1:2, :] + nxt * w_ref[2:3, :] + b_ref[...]
    y = y * jax.nn.sigmoid(y)
    scale = jnp.where(pl.program_id(1) < q_blocks, q_scale, 1.0).astype(F32)
    o_ref[0] = (y * scale).astype(o_ref.dtype)


def _conv_silu(z3, conv_w, conv_b, *, col0, tc):
    b, s, _ = z3.shape
    width = conv_w.shape[1]
    kern = functools.partial(_conv_silu_kernel, q_blocks=MLSTM_QK_WIDTH // tc, q_scale=MLSTM_QK_DIM ** -0.5)
    return pl.pallas_call(
        kern,
        grid=(b, width // tc),
        in_specs=[
            pl.BlockSpec((1, s, tc), lambda i, j: (i, 0, col0 // tc + j)),
            pl.BlockSpec((3, tc), lambda i, j: (0, j)),
            pl.BlockSpec((1, tc), lambda i, j: (0, j)),
        ],
        out_specs=pl.BlockSpec((1, s, tc), lambda i, j: (i, 0, j)),
        out_shape=jax.ShapeDtypeStruct((b, s, width), BF16),
        compiler_params=_cparams(("parallel", "parallel")),
        name="mlstm_conv_silu",
    )(z3, conv_w, conv_b)


def _log_sigmoid(x):
    return jnp.minimum(x, 0.0) - jnp.log1p(jnp.exp(-jnp.abs(x)))


def _lane_col(x, idx):
    lane = lax.broadcasted_iota(jnp.int32, x.shape, 1)
    return jnp.sum(jnp.where(lane == idx, x, 0.0), axis=-1, keepdims=True)


def _mlstm_kernel(qkf_ref, qkb_ref, vf_ref, vb_ref, gf_ref, gb_ref, bias_ref, hf_ref, hb_ref, c_scr, m_scr):
    L = MLSTM_CHUNK
    dk, dv, nh = MLSTM_QK_DIM, MLSTM_V_DIM, N_MLSTM_HEADS

    @pl.when(pl.program_id(1) == 0)
    def _():
        c_scr[...] = jnp.zeros_like(c_scr)
        m_scr[...] = jnp.zeros_like(m_scr)

    r_i = lax.broadcasted_iota(jnp.int32, (L, L), 0)
    c_i = lax.broadcasted_iota(jnp.int32, (L, L), 1)
    lane = lax.broadcasted_iota(jnp.int32, (L, LANES), 1)
    ones_blk = jnp.where(lane == 0, 1.0, 0.0).astype(F32)

    dirs = ((qkf_ref, vf_ref, gf_ref, hf_ref), (qkb_ref, vb_ref, gb_ref, hb_ref))
    for d, (qk_ref, v_ref, g_ref, h_ref) in enumerate(dirs):
        mask = (c_i <= r_i) if d == 0 else (c_i >= r_i)
        g = g_ref[0] + bias_ref[...]
        gp = jnp.where(lane < N_GATE_COLS // 2, g, _log_sigmoid(g))
        gp_t = gp.T
        bc = jnp.dot(mask.astype(F32), gp, precision=lax.Precision.HIGHEST, preferred_element_type=F32)
        bc_t = bc.T
        end_row = L - 1 if d == 0 else 0
        for h in range(nh):
            ci = d * nh + h
            ch_i, ch_f = d * nh + h, N_GATE_COLS // 2 + d * nh + h
            i_row = gp_t[ch_i:ch_i + 1, :]
            bc_row = bc_t[ch_f:ch_f + 1, :]
            i_col = _lane_col(gp, ch_i)
            bc_col = _lane_col(bc, ch_f)
            m_prev = m_scr[ci, 0:1, 0:1]
            c_prev = c_scr[ci]

            a_col = bc_col + m_prev
            dm = jnp.where(mask, bc_col - bc_row + i_row, -jnp.inf)
            m_t = jnp.maximum(a_col, jnp.max(dm, axis=-1, keepdims=True))
            w_inter = jnp.exp(a_col - m_t)
            w_intra = jnp.exp(dm - m_t)

            q = qk_ref[0, :, h * dk:(h + 1) * dk]
            k = qk_ref[0, :, MLSTM_QK_WIDTH + h * dk:MLSTM_QK_WIDTH + (h + 1) * dk]
            v_ext = jnp.concatenate([v_ref[0, :, h * dv:(h + 1) * dv], ones_blk], axis=1)
            p = (_dot_nt(q, k) * w_intra).astype(BF16)
            num_ext = w_inter * _dot(q, c_prev.astype(BF16)) + _dot(p, v_ext.astype(BF16))
            den = num_ext[:, dv:dv + 1]
            h_ref[0, :, h * dv:(h + 1) * dv] = num_ext[:, :dv] / jnp.maximum(jnp.abs(den), jnp.exp(-m_t))

            b_end = bc_col[end_row:end_row + 1, :]
            g_col = b_end - bc_col + i_col
            m_new = jnp.maximum(b_end + m_prev, jnp.max(g_col, axis=0, keepdims=True))
            decay = jnp.exp(b_end + m_prev - m_new)
            ws = jnp.exp(g_col - m_new)
            c_scr[ci] = decay * c_prev + _dot_tn(k, (ws * v_ext).astype(BF16))
            m_scr[ci] = jnp.broadcast_to(m_new, m_scr.shape[1:])


def _mlstm_scan(qk3, z3, gate_bias, *, v_col0, gate_col0):
    b, s, _ = qk3.shape
    L = MLSTM_CHUNK
    nc = s // L
    n_chain = N_DIRS * N_MLSTM_HEADS
    vb, gb = v_col0 // MLSTM_WIDTH, gate_col0 // LANES
    qk_w = 2 * MLSTM_QK_WIDTH
    return pl.pallas_call(
        _mlstm_kernel,
        grid=(b, nc),
        in_specs=[
            pl.BlockSpec((1, L, qk_w), lambda i, c: (i, c, 0)),
            pl.BlockSpec((1, L, qk_w), lambda i, c: (i, nc - 1 - c, 0)),
            pl.BlockSpec((1, L, MLSTM_WIDTH), lambda i, c: (i, c, vb)),
            pl.BlockSpec((1, L, MLSTM_WIDTH), lambda i, c: (i, nc - 1 - c, vb)),
            pl.BlockSpec((1, L, LANES), lambda i, c: (i, c, gb)),
            pl.BlockSpec((1, L, LANES), lambda i, c: (i, nc - 1 - c, gb)),
            pl.BlockSpec((1, LANES), lambda i, c: (0, 0)),
        ],
        out_specs=[
            pl.BlockSpec((1, L, MLSTM_WIDTH), lambda i, c: (i, c, 0)),
            pl.BlockSpec((1, L, MLSTM_WIDTH), lambda i, c: (i, nc - 1 - c, 0)),
        ],
        out_shape=[jax.ShapeDtypeStruct((b, s, MLSTM_WIDTH), F32)] * 2,
        scratch_shapes=[
            pltpu.VMEM((n_chain, MLSTM_QK_DIM, MLSTM_V_DIM + LANES), F32),
            pltpu.VMEM((n_chain, 8, LANES), F32),
        ],
        compiler_params=_cparams(("parallel", "arbitrary")),
        name="mlstm_scan",
    )(qk3, qk3, z3, z3, z3, z3, gate_bias)


LOG2_E = 1.4426950408889634


def _qk_prep_kernel(aq_ref, ak_ref, av_ref, cos_ref, sin_ref, gq_ref, gk_ref, q_ref, k_ref, vt_ref):
    cos, sin = cos_ref[...], sin_ref[...]
    hd = ATTN_HEAD_DIM
    lane = lax.broadcasted_iota(jnp.int32, cos.shape, 1)
    first_half = (lane % (hd // 2)) < (hd // 4)

    def prep(x, g, scale):
        n = _rms(x, g)
        partner = jnp.where(first_half, pltpu.roll(n, hd - hd // 4, 1), pltpu.roll(n, hd // 4, 1))
        return ((n * cos + partner * sin) * scale).astype(BF16)

    for h in range(N_ATTN_HEADS):
        q_ref[0, :, h * hd:(h + 1) * hd] = prep(aq_ref[0, :, h * hd:(h + 1) * hd], gq_ref[...], hd ** -0.5 * LOG2_E)
    for h in range(N_KV_HEADS):
        k_ref[0, :, h * hd:(h + 1) * hd] = prep(ak_ref[0, :, h * hd:(h + 1) * hd], gk_ref[...], 1.0)
    vt_ref[0] = av_ref[0].T.astype(BF16)


def _qk_prep(z3, cos_t, sin_t, gq, gk, *, aq_col0, ak_col0, av_col0, ts):
    b, s, _ = z3.shape
    return pl.pallas_call(
        _qk_prep_kernel,
        grid=(b, s // ts),
        in_specs=[
            pl.BlockSpec((1, ts, ATTN_WIDTH), lambda i, j: (i, j, aq_col0 // ATTN_WIDTH)),
            pl.BlockSpec((1, ts, KV_WIDTH), lambda i, j: (i, j, ak_col0 // KV_WIDTH)),
            pl.BlockSpec((1, ts, KV_WIDTH), lambda i, j: (i, j, av_col0 // KV_WIDTH)),
            pl.BlockSpec((ts, ATTN_HEAD_DIM), lambda i, j: (j, 0)),
            pl.BlockSpec((ts, ATTN_HEAD_DIM), lambda i, j: (j, 0)),
            pl.BlockSpec((1, ATTN_HEAD_DIM), lambda i, j: (0, 0)),
            pl.BlockSpec((1, ATTN_HEAD_DIM), lambda i, j: (0, 0)),
        ],
        out_specs=[
            pl.BlockSpec((1, ts, ATTN_WIDTH), lambda i, j: (i, j, 0)),
            pl.BlockSpec((1, ts, KV_WIDTH), lambda i, j: (i, j, 0)),
            pl.BlockSpec((1, KV_WIDTH, ts), lambda i, j: (i, 0, j)),
        ],
        out_shape=[jax.ShapeDtypeStruct((b, s, ATTN_WIDTH), BF16), jax.ShapeDtypeStruct((b, s, KV_WIDTH), BF16),
                   jax.ShapeDtypeStruct((b, KV_WIDTH, s), BF16)],
        compiler_params=_cparams(("parallel", "parallel")),
        name="attn_qk_norm_rope",
    )(z3, z3, z3, cos_t, sin_t, gq, gk)


def _flash_kernel(q_ref, k_ref, vt_ref, o_ref, *, grp, tk):
    hd = ATTN_HEAD_DIM
    tq = q_ref.shape[1]
    s_len = k_ref.shape[1]
    rows = grp * tq
    q = jnp.concatenate([q_ref[0, :, g * hd:(g + 1) * hd] for g in range(grp)], axis=0)
    m = jnp.full((1, rows), -jnp.inf, F32)
    l = jnp.zeros((1, rows), F32)
    acc = jnp.zeros((hd, rows), F32)
    for c in range(s_len // tk):
        st = _dot_nt(k_ref[0, c * tk:(c + 1) * tk, :], q)
        m_new = jnp.maximum(m, jnp.max(st, axis=0, keepdims=True))
        alpha = jnp.exp2(m - m_new)
        p = jnp.exp2(st - m_new)
        l = alpha * l + jnp.sum(p, axis=0, keepdims=True)
        acc = alpha * acc + _dot(vt_ref[0, :, c * tk:(c + 1) * tk], p.astype(BF16))
        m = m_new
    o = (acc / l).T
    for g in range(grp):
        o_ref[0, :, g * hd:(g + 1) * hd] = o[g * tq:(g + 1) * tq, :].astype(o_ref.dtype)


def _flash_gqa(q3, k3, vt3, *, tq, tk):
    b, s, _ = q3.shape
    hd = ATTN_HEAD_DIM
    grp = N_ATTN_HEADS // N_KV_HEADS
    return pl.pallas_call(
        functools.partial(_flash_kernel, grp=grp, tk=tk),
        grid=(b, N_KV_HEADS, s // tq),
        in_specs=[
            pl.BlockSpec((1, tq, grp * hd), lambda i, h, qi: (i, qi, h)),
            pl.BlockSpec((1, s, hd), lambda i, h, qi: (i, 0, h)),
            pl.BlockSpec((1, hd, s), lambda i, h, qi: (i, h, 0)),
        ],
        out_specs=pl.BlockSpec((1, tq, grp * hd), lambda i, h, qi: (i, qi, h)),
        out_shape=jax.ShapeDtypeStruct((b, s, ATTN_WIDTH), BF16),
        compiler_params=_cparams(("parallel", "parallel", "parallel")),
        name="gqa_flash",
    )(q3, k3, vt3)


def _mlstm_out_kernel(hf_ref, hb_ref, mo_ref, ng_ref, hm_ref):
    dv = MLSTM_V_DIM
    for h in range(N_MLSTM_HEADS):
        sl = slice(h * dv, (h + 1) * dv)
        hm = _rms(hf_ref[:, sl] + hb_ref[:, sl], ng_ref[:, sl])
        hm_ref[:, sl] = (hm * jax.nn.sigmoid(mo_ref[:, sl])).astype(BF16)


def _mlstm_out(hf, hb, z, ng, *, mo_col0, tm):
    t, w = hf.shape
    row_spec = pl.BlockSpec((tm, w), lambda m: (m, 0))
    return pl.pallas_call(
        _mlstm_out_kernel,
        grid=(t // tm,),
        in_specs=[row_spec, row_spec, pl.BlockSpec((tm, w), lambda m: (m, mo_col0 // w)),
                  pl.BlockSpec((1, w), lambda m: (0, 0))],
        out_specs=row_spec,
        out_shape=jax.ShapeDtypeStruct((t, w), BF16),
        compiler_params=_cparams(("parallel",)),
        name="mlstm_out_norm",
    )(hf, hb, z, ng)


def _merge_kernel(hm_ref, ha_ref, gm_ref, ga_ref, wm_ref, wa_ref, y_ref):
    ym = _dot(hm_ref[...], wm_ref[...].astype(BF16))
    ya = _dot(ha_ref[...], wa_ref[...].astype(BF16))
    y_ref[...] = (jax.nn.sigmoid(gm_ref[...]) * ym + jax.nn.sigmoid(ga_ref[...]) * ya).astype(y_ref.dtype)


def _merge(hm, ha, z, wm_all, wa_all, layer, *, gm_col0, ga_col0, tm, tn):
    t, w = hm.shape
    d = wm_all.shape[2]
    act_spec = pl.BlockSpec((tm, w), lambda j, m: (m, 0))
    w_spec = pl.BlockSpec((None, w, tn), lambda j, m: (layer, 0, j))
    return pl.pallas_call(
        _merge_kernel,
        grid=(d // tn, t // tm),
        in_specs=[
            act_spec,
            act_spec,
            pl.BlockSpec((tm, tn), lambda j, m: (m, gm_col0 // tn + j)),
            pl.BlockSpec((tm, tn), lambda j, m: (m, ga_col0 // tn + j)),
            w_spec,
            w_spec,
        ],
        out_specs=pl.BlockSpec((tm, tn), lambda j, m: (m, j)),
        out_shape=jax.ShapeDtypeStruct((t, d), BF16),
        compiler_params=_cparams(("parallel", "parallel")),
        name="branch_merge",
    )(hm, ha, z, z, wm_all, wa_all)


def _out_proj_kernel(y_ref, w_ref, x_ref, o_ref):
    o_ref[...] = x_ref[...] + _dot(y_ref[...], w_ref[...].astype(BF16))


def _out_proj(y, w_all, layer, x, *, tm, tn):
    t, d = x.shape
    k = y.shape[1]
    return pl.pallas_call(
        _out_proj_kernel,
        grid=(t // tm, d // tn),
        in_specs=[
            pl.BlockSpec((tm, k), lambda m, j: (m, 0)),
            pl.BlockSpec((None, k, tn), lambda m, j: (layer, 0, j)),
            pl.BlockSpec((tm, tn), lambda m, j: (m, j)),
        ],
        out_specs=pl.BlockSpec((tm, tn), lambda m, j: (m, j)),
        out_shape=jax.ShapeDtypeStruct((t, d), F32),
        compiler_params=_cparams(("parallel", "parallel")),
        name="out_proj_residual",
    )(y, w_all, x)


def _swiglu_partial(h, wg_ref, wu_ref, wd_ref):
    a = _dot(h, wg_ref[...].astype(BF16))
    u = _dot(h, wu_ref[...].astype(BF16))
    t = a * jax.nn.sigmoid(a) * u
    return _dot(t.astype(BF16), wd_ref[...].astype(BF16))


def _ffn_kernel(x_ref, g_ref, wg_ref, wu_ref, wd_ref, o_ref, h_scr):
    @pl.when(pl.program_id(1) == 0)
    def _():
        x = x_ref[...]
        h_scr[...] = _rms(x, g_ref[...]).astype(BF16)
        o_ref[...] = x

    o_ref[...] += _swiglu_partial(h_scr[...], wg_ref, wu_ref, wd_ref)


def _ffn(x, g, wg_all, wu_all, wd_all, j, *, tm, tf):
    t, d = x.shape
    ff = wg_all.shape[-1]
    w_in_spec = pl.BlockSpec((None, d, tf), lambda m, f: (j, 0, f))
    return pl.pallas_call(
        _ffn_kernel,
        grid=(t // tm, ff // tf),
        in_specs=[
            pl.BlockSpec((tm, d), lambda m, f: (m, 0)),
            pl.BlockSpec((1, d), lambda m, f: (0, 0)),
            w_in_spec,
            w_in_spec,
            pl.BlockSpec((None, tf, d), lambda m, f: (j, f, 0)),
        ],
        out_specs=pl.BlockSpec((tm, d), lambda m, f: (m, 0)),
        out_shape=jax.ShapeDtypeStruct((t, d), F32),
        scratch_shapes=[pltpu.VMEM((tm, d), BF16)],
        compiler_params=_cparams(("parallel", "arbitrary")),
        name="dense_swiglu",
    )(x, g, wg_all, wu_all, wd_all)


MOE_ROW_TILE = 1024
MOE_SUB_TILE = 512
META_E1, META_E2, META_P1, META_P2, META_R1, META_R2 = range(6)


def _router_kernel(x_ref, g_ref, w_ref, b_ref, meta_ref, cnt_ref, carry_scr):
    @pl.when(pl.program_id(0) == 0)
    def _():
        carry_scr[...] = jnp.zeros_like(carry_scr)

    h = _rms(x_ref[...], g_ref[...])
    logits = jnp.dot(h, w_ref[...], precision=lax.Precision.HIGHEST, preferred_element_type=F32) + b_ref[...]
    tb = logits.shape[0]
    lane = lax.broadcasted_iota(jnp.int32, logits.shape, 1)
    logits = jnp.where(lane < N_EXPERTS, logits, -jnp.inf)
    v1 = jnp.max(logits, axis=-1, keepdims=True)
    i1 = jnp.min(jnp.where(logits == v1, lane, LANES), axis=-1, keepdims=True)
    rest = jnp.where(lane == i1, -jnp.inf, logits)
    v2 = jnp.max(rest, axis=-1, keepdims=True)
    i2 = jnp.min(jnp.where(rest == v2, lane, LANES), axis=-1, keepdims=True)
    e2 = jnp.exp(v2 - v1)
    p1 = 1.0 / (1.0 + e2)
    p2 = e2 / (1.0 + e2)
    sel = ((lane == i1) | (lane == i2)).astype(F32)
    r_i = lax.broadcasted_iota(jnp.int32, (tb, tb), 0)
    c_i = lax.broadcasted_iota(jnp.int32, (tb, tb), 1)
    earlier = (c_i < r_i).astype(BF16)
    rank = carry_scr[0:1, :] + _dot(earlier, sel.astype(BF16))
    r1 = jnp.sum(jnp.where(lane == i1, rank, 0.0), axis=-1, keepdims=True)
    r2 = jnp.sum(jnp.where(lane == i2, rank, 0.0), axis=-1, keepdims=True)
    meta = jnp.zeros_like(logits)
    for idx, val in ((META_E1, i1.astype(F32)), (META_E2, i2.astype(F32)), (META_P1, p1), (META_P2, p2),
                     (META_R1, r1), (META_R2, r2)):
        meta = jnp.where(lane == idx, val, meta)
    meta_ref[...] = meta
    total = carry_scr[0:1, :] + jnp.sum(sel, axis=0, keepdims=True)
    carry_scr[...] = jnp.broadcast_to(total, carry_scr.shape)
    cnt_ref[...] = jnp.broadcast_to(total, cnt_ref.shape)


def _router(x, g, w_pad, b_pad, *, tm):
    t, d = x.shape
    return pl.pallas_call(
        _router_kernel,
        grid=(t // tm,),
        in_specs=[
            pl.BlockSpec((tm, d), lambda m: (m, 0)),
            pl.BlockSpec((1, d), lambda m: (0, 0)),
            pl.BlockSpec((d, LANES), lambda m: (0, 0)),
            pl.BlockSpec((1, LANES), lambda m: (0, 0)),
        ],
        out_specs=[pl.BlockSpec((tm, LANES), lambda m: (m, 0)), pl.BlockSpec((8, LANES), lambda m: (0, 0))],
        out_shape=[jax.ShapeDtypeStruct((t, LANES), F32), jax.ShapeDtypeStruct((8, LANES), F32)],
        scratch_shapes=[pltpu.VMEM((8, LANES), F32)],
        compiler_params=_cparams(("arbitrary",)),
        name="moe_router",
    )(x, g, w_pad, b_pad)


def _moe_plan(meta, cnt, t):
    tile, sub = MOE_ROW_TILE, MOE_SUB_TILE
    n_tiles = 2 * t // tile + N_EXPERTS
    e1 = meta[:, META_E1].astype(jnp.int32)
    e2 = meta[:, META_E2].astype(jnp.int32)
    r1 = meta[:, META_R1].astype(jnp.int32)
    r2 = meta[:, META_R2].astype(jnp.int32)
    counts = cnt[0, :N_EXPERTS].astype(jnp.int32)
    tiles_e = (counts + tile - 1) // tile
    tile_end = jnp.cumsum(tiles_e)
    tile_start = tile_end - tiles_e
    row_start = tile_start * tile
    dest = jnp.concatenate([row_start[e1] + r1, row_start[e2] + r2]).astype(jnp.int32)
    tok = jnp.arange(t, dtype=jnp.int32)
    src = jnp.zeros((n_tiles * tile,), jnp.int32).at[dest].set(jnp.concatenate([tok, tok]))
    m = jnp.arange(n_tiles, dtype=jnp.int32)
    live = m < tile_end[-1]
    m_eff = jnp.where(live, m, jnp.maximum(tile_end[-1] - 1, 0))
    tile_expert = jnp.minimum(jnp.searchsorted(tile_end, m_eff, side="right"), N_EXPERTS - 1).astype(jnp.int32)
    rows_left = counts[tile_expert] - (m_eff - tile_start[tile_expert]) * tile
    n_sub = jnp.where(live, jnp.clip((rows_left + sub - 1) // sub, 0, tile // sub), 0).astype(jnp.int32)
    return dest, src, tile_expert, n_sub, n_tiles


def _expert_kernel(te_ref, ns_ref, src_ref, x_hbm, g_ref, wg_ref, wu_ref, wd_ref, o_ref, xbuf, h_scr, sem):
    del te_ref
    m, f = pl.program_id(0), pl.program_id(1)
    tile, sub = MOE_ROW_TILE, MOE_SUB_TILE
    n_live = ns_ref[m]
    subs = [(sb, slice(sb * sub, (sb + 1) * sub)) for sb in range(tile // sub)]

    @pl.when(f == 0)
    def _():
        for sb, rows in subs:
            @pl.when(sb < n_live)
            def _():
                def start(r, carry):
                    row = sb * sub + r
                    pltpu.make_async_copy(
                        x_hbm.at[pl.ds(src_ref[m * tile + row], 1)], xbuf.at[pl.ds(row, 1)], sem).start()
                    return carry

                lax.fori_loop(0, sub, start, 0, unroll=8)

        for sb, rows in subs:
            @pl.when(sb < n_live)
            def _():
                pltpu.make_async_copy(x_hbm.at[pl.ds(0, sub)], xbuf.at[rows], sem).wait()

        for sb, rows in subs:
            @pl.when(sb < n_live)
            def _():
                h_scr[rows, :] = _rms(xbuf[rows, :], g_ref[...]).astype(BF16)

        o_ref[...] = jnp.zeros_like(o_ref)

    for sb, rows in subs:
        @pl.when(sb < n_live)
        def _():
            o_ref[rows, :] += _swiglu_partial(h_scr[rows, :], wg_ref, wu_ref, wd_ref)


def _experts(x, g, wg_all, wu_all, wd_all, j, tile_expert, n_sub, src, n_tiles, *, tf):
    t, d = x.shape
    tile = MOE_ROW_TILE
    ff = wg_all.shape[-1]
    n_f = ff // tf

    def f_eff(m, f, ns):
        return jnp.where(ns[m] > 0, f, n_f - 1)

    w_in_spec = pl.BlockSpec((None, None, d, tf), lambda m, f, te, ns, sr: (j, te[m], 0, f_eff(m, f, ns)))
    w_dn_spec = pl.BlockSpec((None, None, tf, d), lambda m, f, te, ns, sr: (j, te[m], f_eff(m, f, ns), 0))
    return pl.pallas_call(
        _expert_kernel,
        grid_spec=pltpu.PrefetchScalarGridSpec(
            num_scalar_prefetch=3,
            grid=(n_tiles, n_f),
            in_specs=[
                pl.BlockSpec(memory_space=pl.ANY),
                pl.BlockSpec((1, d), lambda m, f, te, ns, sr: (0, 0)),
                w_in_spec,
                w_in_spec,
                w_dn_spec,
            ],
            out_specs=pl.BlockSpec((tile, d), lambda m, f, te, ns, sr: (m, 0)),
            scratch_shapes=[pltpu.VMEM((tile, d), F32), pltpu.VMEM((tile, d), BF16), pltpu.SemaphoreType.DMA(())],
        ),
        out_shape=jax.ShapeDtypeStruct((n_tiles * tile, d), F32),
        compiler_params=_cparams(("arbitrary", "arbitrary")),
        name="moe_experts",
    )(tile_expert, n_sub, src, x, g, wg_all, wu_all, wd_all)


def _combine_kernel(dest_ref, x_ref, meta_ref, ys_hbm, o_ref, buf, sem, *, tb, t):
    base = pl.program_id(0) * tb

    def start(r, carry):
        for k in range(2):
            pltpu.make_async_copy(
                ys_hbm.at[pl.ds(dest_ref[k * t + base + r], 1)], buf.at[k, pl.ds(r, 1)], sem).start()
        return carry

    lax.fori_loop(0, tb, start, 0, unroll=8)
    for k in range(2):
        pltpu.make_async_copy(ys_hbm.at[pl.ds(0, tb)], buf.at[k], sem).wait()
    meta = meta_ref[...]
    o_ref[...] = x_ref[...] + _lane_col(meta, META_P1) * buf[0] + _lane_col(meta, META_P2) * buf[1]


def _combine(x, meta, ys, dest, *, tb):
    t, d = x.shape
    return pl.pallas_call(
        functools.partial(_combine_kernel, tb=tb, t=t),
        grid_spec=pltpu.PrefetchScalarGridSpec(
            num_scalar_prefetch=1,
            grid=(t // tb,),
            in_specs=[
                pl.BlockSpec((tb, d), lambda i, dst: (i, 0)),
                pl.BlockSpec((tb, LANES), lambda i, dst: (i, 0)),
                pl.BlockSpec(memory_space=pl.ANY),
            ],
            out_specs=pl.BlockSpec((tb, d), lambda i, dst: (i, 0)),
            scratch_shapes=[pltpu.VMEM((2, tb, d), F32), pltpu.SemaphoreType.DMA(())],
        ),
        out_shape=jax.ShapeDtypeStruct((t, d), F32),
        compiler_params=_cparams(("arbitrary",)),
        name="moe_combine",
    )(dest, x, meta, ys)


def _moe(x, g, router_w, router_b, wg_all, wu_all, wd_all, j, *, tb_route, tf):
    t, d = x.shape
    w_pad = jnp.pad(router_w, ((0, 0), (0, LANES - N_EXPERTS)))
    b_pad = jnp.pad(router_b, (0, LANES - N_EXPERTS))[None, :]
    meta, cnt = _router(x, g, w_pad, b_pad, tm=tb_route)
    dest, src, tile_expert, n_sub, n_tiles = _moe_plan(meta, cnt, t)
    ys = _experts(x, g, wg_all, wu_all, wd_all, j, tile_expert, n_sub, src, n_tiles, tf=tf)
    return _combine(x, meta, ys, dest, tb=min(256, t))


def _ple_kernel(x_ref, g_ref, wg_ref, p_ref, wp_ref, o_ref, h_scr, *, tn):
    j = pl.program_id(1)

    @pl.when(j == 0)
    def _():
        h_scr[...] = _rms(x_ref[...], g_ref[...]).astype(BF16)

    gate = jax.nn.sigmoid(_dot(h_scr[...], wg_ref[...].astype(BF16)))
    proj = _dot(p_ref[...].astype(BF16), wp_ref[...].astype(BF16))
    o_ref[...] = x_ref[:, pl.ds(pl.multiple_of(j * tn, tn), tn)] + gate * proj


def _ple(x, g, wg_all, p_all, wp_all, layer, *, tm, tn):
    t, d = x.shape
    pd = p_all.shape[-1]
    return pl.pallas_call(
        functools.partial(_ple_kernel, tn=tn),
        grid=(t // tm, d // tn),
        in_specs=[
            pl.BlockSpec((tm, d), lambda m, j: (m, 0)),
            pl.BlockSpec((1, d), lambda m, j: (0, 0)),
            pl.BlockSpec((None, d, tn), lambda m, j: (layer, 0, j)),
            pl.BlockSpec((None, tm, pd), lambda m, j: (layer, m, 0)),
            pl.BlockSpec((None, pd, tn), lambda m, j: (layer, 0, j)),
        ],
        out_specs=pl.BlockSpec((tm, tn), lambda m, j: (m, j)),
        out_shape=jax.ShapeDtypeStruct((t, d), F32),
        scratch_shapes=[pltpu.VMEM((tm, d), BF16)],
        compiler_params=_cparams(("parallel", "arbitrary")),
        name="ple_gate",
    )(x, g, wg_all, p_all, wp_all)


def _rope_tables(seq):
    rows = seq // GRID_W
    row = jnp.broadcast_to(jnp.arange(rows, dtype=F32)[:, None], (rows, GRID_W)).reshape(seq)
    col = jnp.broadcast_to(jnp.arange(GRID_W, dtype=F32)[None, :], (rows, GRID_W)).reshape(seq)
    axis_dim = ATTN_HEAD_DIM // 2
    inv_freq = ROPE_THETA ** (-jnp.arange(0, axis_dim, 2, dtype=F32) / axis_dim)
    ar, ac = row[:, None] * inv_freq, col[:, None] * inv_freq
    cos_t = jnp.concatenate([jnp.cos(ar), jnp.cos(ar), jnp.cos(ac), jnp.cos(ac)], axis=-1)
    sin_t = jnp.concatenate([-jnp.sin(ar), jnp.sin(ar), -jnp.sin(ac), jnp.sin(ac)], axis=-1)
    return cos_t, sin_t


def kernel(x, p, norm_mix_g, w_in, conv_w, conv_b, b_igate, b_fgate, mlstm_norm_g, q_norm_g, k_norm_g, w_mlstm_up, w_attn_up, w_out, norm_ffn_g, ffn_w_gate, ffn_w_up, ffn_w_down, moe_router, moe_router_b, moe_w_gate, moe_w_up, moe_w_down, norm_ple_g, w_ple_gate, w_ple_proj):
    b, s, d = x.shape
    depth = w_in.shape[0]
    t = b * s
    col = _col_layout(d)
    cos_t, sin_t = _rope_tables(s)
    xt = x.reshape(t, d)
    p_all = p.reshape(depth, t, p.shape[-1])

    tm_big, tm_half = min(1024, t), min(512, t)
    tn = min(512, d)
    tf = 256

    for i in range(depth):
        z = _norm_matmul(xt, norm_mix_g[i][None, :], w_in, i, _tail_w_in(w_in[i]), tm=tm_big, tn=tn)
        z3 = z.reshape(b, s, col["total"])
        qk3 = _conv_silu(z3, conv_w[i], conv_b[i][None, :], col0=col["qk"], tc=256)
        gate_bias = jnp.concatenate(
            [b_igate[i].reshape(-1), b_fgate[i].reshape(-1), jnp.zeros((LANES - N_GATE_COLS,), F32)])[None, :]
        hf, hb = _mlstm_scan(qk3, z3, gate_bias, v_col0=col["mv"], gate_col0=col["gates"])
        q3, k3, vt3 = _qk_prep(z3, cos_t, sin_t, q_norm_g[i][None, :], k_norm_g[i][None, :],
                               aq_col0=col["aq"], ak_col0=col["ak"], av_col0=col["av"], ts=min(512, s))
        ha = _flash_gqa(q3, k3, vt3, tq=min(256, s), tk=min(512, s))
        hm = _mlstm_out(hf.reshape(t, MLSTM_WIDTH), hb.reshape(t, MLSTM_WIDTH), z,
                        mlstm_norm_g[i].reshape(1, MLSTM_WIDTH), mo_col0=col["mo"], tm=tm_half)
        y = _merge(hm, ha.reshape(t, ATTN_WIDTH), z, w_mlstm_up, w_attn_up, i,
                   gm_col0=col["gm"], ga_col0=col["ga"], tm=tm_big, tn=tn)
        xt = _out_proj(y, w_out, i, xt, tm=tm_big, tn=tn)

        j = i // 2
        g_ffn = norm_ffn_g[i][None, :]
        if i % 2 == 0:
            xt = _ffn(xt, g_ffn, ffn_w_gate, ffn_w_up, ffn_w_down, j, tm=tm_big, tf=tf)
        else:
            xt = _moe(xt, g_ffn, moe_router[j], moe_router_b[j], moe_w_gate, moe_w_up, moe_w_down, j,
                      tb_route=tm_half, tf=tf)

        xt = _ple(xt, norm_ple_g[i][None, :], w_ple_gate, p_all, w_ple_proj, i, tm=tm_big, tn=tn)

    return xt.reshape(b, s, d)
```

```python
import functools

import jax
import jax.numpy as jnp
from jax import lax
from jax.experimental import pallas as pl
from jax.experimental.pallas import tpu as pltpu

F32 = jnp.float32
BF16 = jnp.bfloat16

GRID_W = 64
N_MLSTM_HEADS = 4
MLSTM_QK_DIM = 128
MLSTM_V_DIM = 256
MLSTM_QK_WIDTH = N_MLSTM_HEADS * MLSTM_QK_DIM
MLSTM_WIDTH = N_MLSTM_HEADS * MLSTM_V_DIM
MLSTM_CHUNK = 128
N_DIRS = 2
N_ATTN_HEADS = 8
N_KV_HEADS = 2
ATTN_HEAD_DIM = 128
ATTN_WIDTH = N_ATTN_HEADS * ATTN_HEAD_DIM
KV_WIDTH = N_KV_HEADS * ATTN_HEAD_DIM
ROPE_THETA = 10000.0
N_EXPERTS = 8
EPS = 1e-6

LANES = 128
VMEM_LIMIT_BYTES = 56 * 2**20

N_GATE_COLS = 2 * N_DIRS * N_MLSTM_HEADS


def _col_layout(d_model):
    names = ("qk", "mv", "mo", "aq", "ak", "av", "gm", "ga", "gates")
    widths = (2 * MLSTM_QK_WIDTH, MLSTM_WIDTH, MLSTM_WIDTH, ATTN_WIDTH, KV_WIDTH, KV_WIDTH, d_model, d_model, LANES)
    off, out = 0, {}
    for n, w in zip(names, widths):
        out[n] = off
        off += w
    out["total"] = off
    return out


N_HEAD_COLS = 2 * MLSTM_QK_WIDTH + MLSTM_WIDTH


def _cparams(semantics):
    return pltpu.CompilerParams(dimension_semantics=semantics, vmem_limit_bytes=VMEM_LIMIT_BYTES)


def _rms(x, g):
    ms = jnp.mean(x * x, axis=-1, keepdims=True)
    return x * lax.rsqrt(ms + EPS) * g


def _dot(a, b):
    return jnp.dot(a, b, preferred_element_type=F32)


def _dot_nt(a, b):
    return lax.dot_general(a, b, (((1,), (1,)), ((), ())), preferred_element_type=F32)


def _dot_tn(a, b):
    return lax.dot_general(a, b, (((0,), (0,)), ((), ())), preferred_element_type=F32)


def _norm_matmul_kernel(x_ref, g_ref, wm_ref, wn_ref, wg_ref, o_ref, h_scr, *, head_blocks, tail_blocks):
    j = pl.program_id(1)
    ng = N_GATE_COLS

    @pl.when(j == 0)
    def _():
        h_scr[...] = _rms(x_ref[...], g_ref[...]).astype(BF16)

    @pl.when(j < head_blocks)
    def _():
        o_ref[...] = _dot_nt(h_scr[...], wm_ref[...].astype(BF16))

    @pl.when((j >= head_blocks) & (j < head_blocks + tail_blocks))
    def _():
        w = jnp.concatenate([wm_ref[ng:, :], wn_ref[...]], axis=0).astype(BF16)
        o_ref[...] = _dot_nt(h_scr[...], w)

    @pl.when(j == head_blocks + tail_blocks)
    def _():
        pad = jnp.zeros((wm_ref.shape[0] - ng, wm_ref.shape[1]), F32)
        w = jnp.concatenate([wg_ref[...], pad], axis=0).astype(BF16)
        o_ref[...] = _dot_nt(h_scr[...], w)


def _norm_matmul(x, g, wt_all, layer, *, tm, tn):
    t, d = x.shape
    n_cols = wt_all.shape[1]
    ng = N_GATE_COLS
    tail = n_cols - N_HEAD_COLS - ng
    head_blocks, tail_blocks = N_HEAD_COLS // tn, tail // tn
    assert tail % tn == 0 and N_HEAD_COLS % tn == 0 and n_cols % ng == 0
    last_main = head_blocks + tail_blocks - 1
    return pl.pallas_call(
        functools.partial(_norm_matmul_kernel, head_blocks=head_blocks, tail_blocks=tail_blocks),
        grid=(t // tm, head_blocks + tail_blocks + 1),
        in_specs=[
            pl.BlockSpec((tm, d), lambda m, j: (m, 0)),
            pl.BlockSpec((1, d), lambda m, j: (0, 0)),
            pl.BlockSpec((None, tn, d), lambda m, j: (layer, jnp.minimum(j, last_main), 0)),
            pl.BlockSpec((None, ng, d), lambda m, j: (layer, jnp.minimum((j + 1) * (tn // ng), n_cols // ng - 1), 0)),
            pl.BlockSpec((None, ng, d), lambda m, j: (layer, N_HEAD_COLS // ng, 0)),
        ],
        out_specs=pl.BlockSpec((tm, tn), lambda m, j: (m, j)),
        out_shape=jax.ShapeDtypeStruct((t, N_HEAD_COLS + tail + LANES), F32),
        scratch_shapes=[pltpu.VMEM((tm, d), BF16)],
        compiler_params=_cparams(("parallel", "arbitrary")),
        name="norm_in_proj",
    )(x, g, wt_all, wt_all, wt_all)


def _conv_silu_kernel(z_ref, w_ref, b_ref, o_ref, *, q_blocks, q_scale):
    x = z_ref[0]
    s = x.shape[0]
    row = lax.broadcasted_iota(jnp.int32, x.shape, 0)
    prev = jnp.where(row == 0, 0.0, pltpu.roll(x, 1, 0))
    nxt = jnp.where(row == s - 1, 0.0, pltpu.roll(x, s - 1, 0))
    y = prev * w_ref[0:1, :] + x * w_ref[1:2, :] + nxt * w_ref[2:3, :] + b_ref[...]
    y = y * jax.nn.sigmoid(y)
    scale = jnp.where(pl.program_id(1) < q_blocks, q_scale, 1.0).astype(F32)
    o_ref[0] = (y * scale).astype(o_ref.dtype)


def _conv_silu(z3, conv_w, conv_b, *, col0, tc):
    b, s, _ = z3.shape
    width = conv_w.shape[1]
    kern = functools.partial(_conv_silu_kernel, q_blocks=MLSTM_QK_WIDTH // tc, q_scale=MLSTM_QK_DIM ** -0.5)
    return pl.pallas_call(
        kern,
        grid=(b, width // tc),
        in_specs=[
            pl.BlockSpec((1, s, tc), lambda i, j: (i, 0, col0 // tc + j)),
            pl.BlockSpec((3, tc), lambda i, j: (0, j)),
            pl.BlockSpec((1, tc), lambda i, j: (0, j)),
        ],
        out_specs=pl.BlockSpec((1, s, tc), lambda i, j: (i, 0, j)),
        out_shape=jax.ShapeDtypeStruct((b, s, width), BF16),
        compiler_params=_cparams(("parallel", "parallel")),
        name="mlstm_conv_silu",
    )(z3, conv_w, conv_b)


def _log_sigmoid(x):
    return jnp.minimum(x, 0.0) - jnp.log1p(jnp.exp(-jnp.abs(x)))


def _lane_col(x, idx):
    lane = lax.broadcasted_iota(jnp.int32, x.shape, 1)
    return jnp.sum(jnp.where(lane == idx, x, 0.0), axis=-1, keepdims=True)


def _mlstm_kernel(qkf_ref, qkb_ref, vf_ref, vb_ref, gf_ref, gb_ref, bias_ref, hf_ref, hb_ref, c_scr, m_scr):
    L = MLSTM_CHUNK
    dk, dv, nh = MLSTM_QK_DIM, MLSTM_V_DIM, N_MLSTM_HEADS

    @pl.when(pl.program_id(1) == 0)
    def _():
        c_scr[...] = jnp.zeros_like(c_scr)
        m_scr[...] = jnp.zeros_like(m_scr)

    r_i = lax.broadcasted_iota(jnp.int32, (L, L), 0)
    c_i = lax.broadcasted_iota(jnp.int32, (L, L), 1)
    lane = lax.broadcasted_iota(jnp.int32, (L, LANES), 1)
    ones_blk = jnp.where(lane == 0, 1.0, 0.0).astype(F32)

    dirs = ((qkf_ref, vf_ref, gf_ref, hf_ref), (qkb_ref, vb_ref, gb_ref, hb_ref))
    for d, (qk_ref, v_ref, g_ref, h_ref) in enumerate(dirs):
        mask = (c_i <= r_i) if d == 0 else (c_i >= r_i)
        g = g_ref[0] + bias_ref[...]
        gp = jnp.where(lane < N_GATE_COLS // 2, g, _log_sigmoid(g))
        gp_t = gp.T
        bc = jnp.dot(mask.astype(F32), gp, precision=lax.Precision.HIGHEST, preferred_element_type=F32)
        bc_t = bc.T
        end_row = L - 1 if d == 0 else 0
        for h in range(nh):
            ci = d * nh + h
            ch_i, ch_f = d * nh + h, N_GATE_COLS // 2 + d * nh + h
            i_row = gp_t[ch_i:ch_i + 1, :]
            bc_row = bc_t[ch_f:ch_f + 1, :]
            i_col = _lane_col(gp, ch_i)
            bc_col = _lane_col(bc, ch_f)
            m_prev = m_scr[ci, 0:1, 0:1]
            c_prev = c_scr[ci]

            a_col = bc_col + m_prev
            dm = jnp.where(mask, bc_col - bc_row + i_row, -jnp.inf)
            m_t = jnp.maximum(a_col, jnp.max(dm, axis=-1, keepdims=True))
            w_inter = jnp.exp(a_col - m_t)
            w_intra = jnp.exp(dm - m_t)

            q = qk_ref[0, :, h * dk:(h + 1) * dk]
            k = qk_ref[0, :, MLSTM_QK_WIDTH + h * dk:MLSTM_QK_WIDTH + (h + 1) * dk]
            v_ext = jnp.concatenate([v_ref[0, :, h * dv:(h + 1) * dv], ones_blk], axis=1)
            p = (_dot_nt(q, k) * w_intra).astype(BF16)
            num_ext = w_inter * _dot(q, c_prev.astype(BF16)) + _dot(p, v_ext.astype(BF16))
            den = num_ext[:, dv:dv + 1]
            h_ref[0, :, h * dv:(h + 1) * dv] = num_ext[:, :dv] / jnp.maximum(jnp.abs(den), jnp.exp(-m_t))

            b_end = bc_col[end_row:end_row + 1, :]
            g_col = b_end - bc_col + i_col
            m_new = jnp.maximum(b_end + m_prev, jnp.max(g_col, axis=0, keepdims=True))
            decay = jnp.exp(b_end + m_prev - m_new)
            ws = jnp.exp(g_col - m_new)
            c_scr[ci] = decay * c_prev + _dot_tn(k, (ws * v_ext).astype(BF16))
            m_scr[ci] = jnp.broadcast_to(m_new, m_scr.shape[1:])


def _mlstm_scan(qk3, z3, gate_bias, *, v_col0, gate_col0):
    b, s, _ = qk3.shape
    L = MLSTM_CHUNK
    nc = s // L
    n_chain = N_DIRS * N_MLSTM_HEADS
    vb, gb = v_col0 // MLSTM_WIDTH, gate_col0 // LANES
    qk_w = 2 * MLSTM_QK_WIDTH
    return pl.pallas_call(
        _mlstm_kernel,
        grid=(b, nc),
        in_specs=[
            pl.BlockSpec((1, L, qk_w), lambda i, c: (i, c, 0)),
            pl.BlockSpec((1, L, qk_w), lambda i, c: (i, nc - 1 - c, 0)),
            pl.BlockSpec((1, L, MLSTM_WIDTH), lambda i, c: (i, c, vb)),
            pl.BlockSpec((1, L, MLSTM_WIDTH), lambda i, c: (i, nc - 1 - c, vb)),
            pl.BlockSpec((1, L, LANES), lambda i, c: (i, c, gb)),
            pl.BlockSpec((1, L, LANES), lambda i, c: (i, nc - 1 - c, gb)),
            pl.BlockSpec((1, LANES), lambda i, c: (0, 0)),
        ],
        out_specs=[
            pl.BlockSpec((1, L, MLSTM_WIDTH), lambda i, c: (i, c, 0)),
            pl.BlockSpec((1, L, MLSTM_WIDTH), lambda i, c: (i, nc - 1 - c, 0)),
        ],
        out_shape=[jax.ShapeDtypeStruct((b, s, MLSTM_WIDTH), F32)] * 2,
        scratch_shapes=[
            pltpu.VMEM((n_chain, MLSTM_QK_DIM, MLSTM_V_DIM + LANES), F32),
            pltpu.VMEM((n_chain, 8, LANES), F32),
        ],
        compiler_params=_cparams(("parallel", "arbitrary")),
        name="mlstm_scan",
    )(qk3, qk3, z3, z3, z3, z3, gate_bias)


LOG2_E = 1.4426950408889634


def _qk_prep_kernel(aq_ref, ak_ref, av_ref, cos_ref, sin_ref, gq_ref, gk_ref, q_ref, k_ref, vt_ref):
    cos, sin = cos_ref[...], sin_ref[...]
    hd = ATTN_HEAD_DIM
    lane = lax.broadcasted_iota(jnp.int32, cos.shape, 1)
    first_half = (lane % (hd // 2)) < (hd // 4)

    def prep(x, g, scale):
        n = _rms(x, g)
        partner = jnp.where(first_half, pltpu.roll(n, hd - hd // 4, 1), pltpu.roll(n, hd // 4, 1))
        return ((n * cos + partner * sin) * scale).astype(BF16)

    for h in range(N_ATTN_HEADS):
        q_ref[0, :, h * hd:(h + 1) * hd] = prep(aq_ref[0, :, h * hd:(h + 1) * hd], gq_ref[...], hd ** -0.5 * LOG2_E)
    for h in range(N_KV_HEADS):
        k_ref[0, :, h * hd:(h + 1) * hd] = prep(ak_ref[0, :, h * hd:(h + 1) * hd], gk_ref[...], 1.0)
    vt_ref[0] = av_ref[0].T.astype(BF16)


def _qk_prep(z3, cos_t, sin_t, gq, gk, *, aq_col0, ak_col0, av_col0, ts):
    b, s, _ = z3.shape
    return pl.pallas_call(
        _qk_prep_kernel,
        grid=(b, s // ts),
        in_specs=[
            pl.BlockSpec((1, ts, ATTN_WIDTH), lambda i, j: (i, j, aq_col0 // ATTN_WIDTH)),
            pl.BlockSpec((1, ts, KV_WIDTH), lambda i, j: (i, j, ak_col0 // KV_WIDTH)),
            pl.BlockSpec((1, ts, KV_WIDTH), lambda i, j: (i, j, av_col0 // KV_WIDTH)),
            pl.BlockSpec((ts, ATTN_HEAD_DIM), lambda i, j: (j, 0)),
            pl.BlockSpec((ts, ATTN_HEAD_DIM), lambda i, j: (j, 0)),
            pl.BlockSpec((1, ATTN_HEAD_DIM), lambda i, j: (0, 0)),
            pl.BlockSpec((1, ATTN_HEAD_DIM), lambda i, j: (0, 0)),
        ],
        out_specs=[
            pl.BlockSpec((1, ts, ATTN_WIDTH), lambda i, j: (i, j, 0)),
            pl.BlockSpec((1, ts, KV_WIDTH), lambda i, j: (i, j, 0)),
            pl.BlockSpec((1, KV_WIDTH, ts), lambda i, j: (i, 0, j)),
        ],
        out_shape=[jax.ShapeDtypeStruct((b, s, ATTN_WIDTH), BF16), jax.ShapeDtypeStruct((b, s, KV_WIDTH), BF16),
                   jax.ShapeDtypeStruct((b, KV_WIDTH, s), BF16)],
        compiler_params=_cparams(("parallel", "parallel")),
        name="attn_qk_norm_rope",
    )(z3, z3, z3, cos_t, sin_t, gq, gk)


def _flash_kernel(q_ref, k_ref, vt_ref, o_ref, *, grp, tk):
    hd = ATTN_HEAD_DIM
    tq = q_ref.shape[1]
    s_len = k_ref.shape[1]
    rows = grp * tq
    q = jnp.concatenate([q_ref[0, :, g * hd:(g + 1) * hd] for g in range(grp)], axis=0)
    m = jnp.full((1, rows), -jnp.inf, F32)
    l = jnp.zeros((1, rows), F32)
    acc = jnp.zeros((hd, rows), F32)
    for c in range(s_len // tk):
        st = _dot_nt(k_ref[0, c * tk:(c + 1) * tk, :], q)
        m_new = jnp.maximum(m, jnp.max(st, axis=0, keepdims=True))
        alpha = jnp.exp2(m - m_new)
        p = jnp.exp2(st - m_new)
        l = alpha * l + jnp.sum(p, axis=0, keepdims=True)
        acc = alpha * acc + _dot(vt_ref[0, :, c * tk:(c + 1) * tk], p.astype(BF16))
        m = m_new
    o = (acc / l).T
    for g in range(grp):
        o_ref[0, :, g * hd:(g + 1) * hd] = o[g * tq:(g + 1) * tq, :].astype(o_ref.dtype)


def _flash_gqa(q3, k3, vt3, *, tq, tk):
    b, s, _ = q3.shape
    hd = ATTN_HEAD_DIM
    grp = N_ATTN_HEADS // N_KV_HEADS
    return pl.pallas_call(
        functools.partial(_flash_kernel, grp=grp, tk=tk),
        grid=(b, N_KV_HEADS, s // tq),
        in_specs=[
            pl.BlockSpec((1, tq, grp * hd), lambda i, h, qi: (i, qi, h)),
            pl.BlockSpec((1, s, hd), lambda i, h, qi: (i, 0, h)),
            pl.BlockSpec((1, hd, s), lambda i, h, qi: (i, h, 0)),
        ],
        out_specs=pl.BlockSpec((1, tq, grp * hd), lambda i, h, qi: (i, qi, h)),
        out_shape=jax.ShapeDtypeStruct((b, s, ATTN_WIDTH), BF16),
        compiler_params=_cparams(("parallel", "parallel", "parallel")),
        name="gqa_flash",
    )(q3, k3, vt3)


def _mlstm_out_kernel(hf_ref, hb_ref, mo_ref, ng_ref, hm_ref):
    dv = MLSTM_V_DIM
    for h in range(N_MLSTM_HEADS):
        sl = slice(h * dv, (h + 1) * dv)
        hm = _rms(hf_ref[:, sl] + hb_ref[:, sl], ng_ref[:, sl])
        hm_ref[:, sl] = (hm * jax.nn.sigmoid(mo_ref[:, sl])).astype(BF16)


def _mlstm_out(hf, hb, z, ng, *, mo_col0, tm):
    t, w = hf.shape
    row_spec = pl.BlockSpec((tm, w), lambda m: (m, 0))
    return pl.pallas_call(
        _mlstm_out_kernel,
        grid=(t // tm,),
        in_specs=[row_spec, row_spec, pl.BlockSpec((tm, w), lambda m: (m, mo_col0 // w)),
                  pl.BlockSpec((1, w), lambda m: (0, 0))],
        out_specs=row_spec,
        out_shape=jax.ShapeDtypeStruct((t, w), BF16),
        compiler_params=_cparams(("parallel",)),
        name="mlstm_out_norm",
    )(hf, hb, z, ng)


def _merge_kernel(hm_ref, ha_ref, gm_ref, ga_ref, wm_ref, wa_ref, y_ref):
    ym = _dot(hm_ref[...], wm_ref[...].astype(BF16))
    ya = _dot(ha_ref[...], wa_ref[...].astype(BF16))
    y_ref[...] = (jax.nn.sigmoid(gm_ref[...]) * ym + jax.nn.sigmoid(ga_ref[...]) * ya).astype(y_ref.dtype)


def _merge(hm, ha, z, wm_all, wa_all, layer, *, gm_col0, ga_col0, tm, tn):
    t, w = hm.shape
    d = wm_all.shape[2]
    act_spec = pl.BlockSpec((tm, w), lambda j, m: (m, 0))
    w_spec = pl.BlockSpec((None, w, tn), lambda j, m: (layer, 0, j))
    return pl.pallas_call(
        _merge_kernel,
        grid=(d // tn, t // tm),
        in_specs=[
            act_spec,
            act_spec,
            pl.BlockSpec((tm, tn), lambda j, m: (m, gm_col0 // tn + j)),
            pl.BlockSpec((tm, tn), lambda j, m: (m, ga_col0 // tn + j)),
            w_spec,
            w_spec,
        ],
        out_specs=pl.BlockSpec((tm, tn), lambda j, m: (m, j)),
        out_shape=jax.ShapeDtypeStruct((t, d), BF16),
        compiler_params=_cparams(("parallel", "parallel")),
        name="branch_merge",
    )(hm, ha, z, z, wm_all, wa_all)


def _out_proj_kernel(y_ref, w_ref, x_ref, o_ref):
    o_ref[...] = x_ref[...] + _dot(y_ref[...], w_ref[...].astype(BF16))


def _out_proj(y, w_all, layer, x, *, tm, tn):
    t, d = x.shape
    k = y.shape[1]
    return pl.pallas_call(
        _out_proj_kernel,
        grid=(t // tm, d // tn),
        in_specs=[
            pl.BlockSpec((tm, k), lambda m, j: (m, 0)),
            pl.BlockSpec((None, k, tn), lambda m, j: (layer, 0, j)),
            pl.BlockSpec((tm, tn), lambda m, j: (m, j)),
        ],
        out_specs=pl.BlockSpec((tm, tn), lambda m, j: (m, j)),
        out_shape=jax.ShapeDtypeStruct((t, d), F32),
        compiler_params=_cparams(("parallel", "parallel")),
        name="out_proj_residual",
    )(y, w_all, x)


def _swiglu_partial(h, wg_ref, wu_ref, wd_ref):
    a = _dot(h, wg_ref[...].astype(BF16))
    u = _dot(h, wu_ref[...].astype(BF16))
    t = a * jax.nn.sigmoid(a) * u
    return _dot(t.astype(BF16), wd_ref[...].astype(BF16))


def _ffn_kernel(x_ref, g_ref, wg_ref, wu_ref, wd_ref, o_ref, h_scr):
    @pl.when(pl.program_id(1) == 0)
    def _():
        x = x_ref[...]
        h_scr[...] = _rms(x, g_ref[...]).astype(BF16)
        o_ref[...] = x

    o_ref[...] += _swiglu_partial(h_scr[...], wg_ref, wu_ref, wd_ref)


def _ffn(x, g, wg_all, wu_all, wd_all, j, *, tm, tf):
    t, d = x.shape
    ff = wg_all.shape[-1]
    w_in_spec = pl.BlockSpec((None, d, tf), lambda m, f: (j, 0, f))
    return pl.pallas_call(
        _ffn_kernel,
        grid=(t // tm, ff // tf),
        in_specs=[
            pl.BlockSpec((tm, d), lambda m, f: (m, 0)),
            pl.BlockSpec((1, d), lambda m, f: (0, 0)),
            w_in_spec,
            w_in_spec,
            pl.BlockSpec((None, tf, d), lambda m, f: (j, f, 0)),
        ],
        out_specs=pl.BlockSpec((tm, d), lambda m, f: (m, 0)),
        out_shape=jax.ShapeDtypeStruct((t, d), F32),
        scratch_shapes=[pltpu.VMEM((tm, d), BF16)],
        compiler_params=_cparams(("parallel", "arbitrary")),
        name="dense_swiglu",
    )(x, g, wg_all, wu_all, wd_all)


MOE_ROW_TILE = 2560
MOE_SUB_TILE = 512
META_E1, META_E2, META_P1, META_P2, META_R1, META_R2 = range(6)


def _router_kernel(x_ref, g_ref, w_ref, b_ref, meta_ref, cnt_ref, carry_scr):
    @pl.when(pl.program_id(0) == 0)
    def _():
        carry_scr[...] = jnp.zeros_like(carry_scr)

    h = _rms(x_ref[...], g_ref[...])
    logits = jnp.dot(h, w_ref[...], precision=lax.Precision.HIGHEST, preferred_element_type=F32) + b_ref[...]
    tb = logits.shape[0]
    lane = lax.broadcasted_iota(jnp.int32, logits.shape, 1)
    logits = jnp.where(lane < N_EXPERTS, logits, -jnp.inf)
    v1 = jnp.max(logits, axis=-1, keepdims=True)
    i1 = jnp.min(jnp.where(logits == v1, lane, LANES), axis=-1, keepdims=True)
    rest = jnp.where(lane == i1, -jnp.inf, logits)
    v2 = jnp.max(rest, axis=-1, keepdims=True)
    i2 = jnp.min(jnp.where(rest == v2, lane, LANES), axis=-1, keepdims=True)
    e2 = jnp.exp(v2 - v1)
    p1 = 1.0 / (1.0 + e2)
    p2 = e2 / (1.0 + e2)
    sel = ((lane == i1) | (lane == i2)).astype(F32)
    r_i = lax.broadcasted_iota(jnp.int32, (tb, tb), 0)
    c_i = lax.broadcasted_iota(jnp.int32, (tb, tb), 1)
    earlier = (c_i < r_i).astype(BF16)
    rank = carry_scr[0:1, :] + _dot(earlier, sel.astype(BF16))
    r1 = jnp.sum(jnp.where(lane == i1, rank, 0.0), axis=-1, keepdims=True)
    r2 = jnp.sum(jnp.where(lane == i2, rank, 0.0), axis=-1, keepdims=True)
    meta = jnp.zeros_like(logits)
    for idx, val in ((META_E1, i1.astype(F32)), (META_E2, i2.astype(F32)), (META_P1, p1), (META_P2, p2),
                     (META_R1, r1), (META_R2, r2)):
        meta = jnp.where(lane == idx, val, meta)
    meta_ref[...] = meta
    total = carry_scr[0:1, :] + jnp.sum(sel, axis=0, keepdims=True)
    carry_scr[...] = jnp.broadcast_to(total, carry_scr.shape)
    cnt_ref[...] = jnp.broadcast_to(total, cnt_ref.shape)


def _router(x, g, w_pad, b_pad, *, tm):
    t, d = x.shape
    return pl.pallas_call(
        _router_kernel,
        grid=(t // tm,),
        in_specs=[
            pl.BlockSpec((tm, d), lambda m: (m, 0)),
            pl.BlockSpec((1, d), lambda m: (0, 0)),
            pl.BlockSpec((d, LANES), lambda m: (0, 0)),
            pl.BlockSpec((1, LANES), lambda m: (0, 0)),
        ],
        out_specs=[pl.BlockSpec((tm, LANES), lambda m: (m, 0)), pl.BlockSpec((8, LANES), lambda m: (0, 0))],
        out_shape=[jax.ShapeDtypeStruct((t, LANES), F32), jax.ShapeDtypeStruct((8, LANES), F32)],
        scratch_shapes=[pltpu.VMEM((8, LANES), F32)],
        compiler_params=_cparams(("arbitrary",)),
        name="moe_router",
    )(x, g, w_pad, b_pad)


def _moe_plan(meta, cnt, t):
    tile, sub = MOE_ROW_TILE, MOE_SUB_TILE
    n_tiles = 2 * t // tile + N_EXPERTS
    e1 = meta[:, META_E1].astype(jnp.int32)
    e2 = meta[:, META_E2].astype(jnp.int32)
    r1 = meta[:, META_R1].astype(jnp.int32)
    r2 = meta[:, META_R2].astype(jnp.int32)
    counts = cnt[0, :N_EXPERTS].astype(jnp.int32)
    tiles_e = (counts + tile - 1) // tile
    tile_end = jnp.cumsum(tiles_e)
    tile_start = tile_end - tiles_e
    row_start = tile_start * tile
    dest = jnp.concatenate([row_start[e1] + r1, row_start[e2] + r2]).astype(jnp.int32)
    m = jnp.arange(n_tiles, dtype=jnp.int32)
    live = m < tile_end[-1]
    m_eff = jnp.where(live, m, jnp.maximum(tile_end[-1] - 1, 0))
    tile_expert = jnp.minimum(jnp.searchsorted(tile_end, m_eff, side="right"), N_EXPERTS - 1).astype(jnp.int32)
    rows_left = counts[tile_expert] - (m_eff - tile_start[tile_expert]) * tile
    n_sub = jnp.where(live, jnp.clip((rows_left + sub - 1) // sub, 0, tile // sub), 0).astype(jnp.int32)
    sub_base = (jnp.cumsum(n_sub) - n_sub).astype(jnp.int32)
    tok = jnp.arange(t, dtype=jnp.int32)
    src_pos = sub_base[dest // tile] * sub + dest % tile
    n_src = (2 * t // sub + N_EXPERTS) * sub
    src = jnp.zeros((n_src,), jnp.int32).at[src_pos].set(jnp.concatenate([tok, tok]))
    return dest, src, tile_expert, n_sub, sub_base, n_tiles


def _expert_kernel(te_ref, ns_ref, sbase_ref, src_ref, x_hbm, g_ref, wg_ref, wu_ref, wd_ref, ys_hbm,
                   xbuf, h_scr, acc, gsem, zsem, osem):
    del te_ref
    m, f = pl.program_id(0), pl.program_id(1)
    n_m, n_f = pl.num_programs(0), pl.num_programs(1)
    tile, sub = MOE_ROW_TILE, MOE_SUB_TILE
    n_live = ns_ref[m]
    subs = [(sb, slice(sb * sub, (sb + 1) * sub)) for sb in range(tile // sub)]

    def gather_start(sb, slot):
        def start(r, carry):
            tok = src_ref[(sbase_ref[m] + sb) * sub + r]
            pltpu.make_async_copy(x_hbm.at[pl.ds(tok, 1)], xbuf.at[slot, pl.ds(r, 1)], gsem.at[slot]).start()
            return carry

        lax.fori_loop(0, sub, start, 0, unroll=8)

    def result_copy(sb, rows, tile_idx):
        return pltpu.make_async_copy(acc.at[rows], ys_hbm.at[pl.ds(tile_idx * tile + sb * sub, sub)], osem)

    def zero_copy(sb):
        return pltpu.make_async_copy(xbuf.at[0], ys_hbm.at[pl.ds(m * tile + sb * sub, sub)], zsem)

    @pl.when(f == 0)
    def _():
        @pl.when(n_live > 0)
        def _():
            gather_start(0, 0)

        for sb, rows in subs:
            slot = sb % 2
            if sb + 1 < len(subs):
                @pl.when(sb + 1 < n_live)
                def _():
                    gather_start(sb + 1, 1 - slot)

            @pl.when(sb < n_live)
            def _():
                pltpu.make_async_copy(x_hbm.at[pl.ds(0, sub)], xbuf.at[slot], gsem.at[slot]).wait()
                h_scr[rows, :] = _rms(xbuf[slot], g_ref[...]).astype(BF16)

        @pl.when(m > 0)
        def _():
            for sb, rows in subs:
                @pl.when(sb < ns_ref[m - 1])
                def _():
                    result_copy(sb, rows, m - 1).wait()

        for sb, rows in subs:
            @pl.when(sb < n_live)
            def _():
                acc[rows, :] = jnp.zeros((sub, acc.shape[1]), F32)

    @pl.when(f == 1)
    def _():
        xbuf[0] = jnp.zeros(xbuf.shape[1:], F32)
        for sb, rows in subs:
            @pl.when(sb >= n_live)
            def _():
                zero_copy(sb).start()

    for sb, rows in subs:
        @pl.when(sb < n_live)
        def _():
            acc[rows, :] += _swiglu_partial(h_scr[rows, :], wg_ref, wu_ref, wd_ref)

    @pl.when(f == n_f - 1)
    def _():
        for sb, rows in subs:
            @pl.when(sb < n_live)
            def _():
                result_copy(sb, rows, m).start()

            @pl.when(sb >= n_live)
            def _():
                zero_copy(sb).wait()

        @pl.when(m == n_m - 1)
        def _():
            for sb, rows in subs:
                @pl.when(sb < n_live)
                def _():
                    result_copy(sb, rows, m).wait()


def _experts(x, g, wg_all, wu_all, wd_all, j, tile_expert, n_sub, sub_base, src, n_tiles, *, tf):
    t, d = x.shape
    tile = MOE_ROW_TILE
    ff = wg_all.shape[-1]
    n_f = ff // tf

    def f_eff(m, f, ns):
        return jnp.where(ns[m] > 0, f, n_f - 1)

    w_in_spec = pl.BlockSpec((None, None, d, tf), lambda m, f, te, ns, sbase, sr: (j, te[m], 0, f_eff(m, f, ns)))
    w_dn_spec = pl.BlockSpec((None, None, tf, d), lambda m, f, te, ns, sbase, sr: (j, te[m], f_eff(m, f, ns), 0))
    return pl.pallas_call(
        _expert_kernel,
        grid_spec=pltpu.PrefetchScalarGridSpec(
            num_scalar_prefetch=4,
            grid=(n_tiles, n_f),
            in_specs=[
                pl.BlockSpec(memory_space=pl.ANY),
                pl.BlockSpec((1, d), lambda m, f, te, ns, sbase, sr: (0, 0)),
                w_in_spec,
                w_in_spec,
                w_dn_spec,
            ],
            out_specs=pl.BlockSpec(memory_space=pl.ANY),
            scratch_shapes=[
                pltpu.VMEM((2, MOE_SUB_TILE, d), F32),
                pltpu.VMEM((tile, d), BF16),
                pltpu.VMEM((tile, d), F32),
                pltpu.SemaphoreType.DMA((2,)),
                pltpu.SemaphoreType.DMA(()),
                pltpu.SemaphoreType.DMA(()),
            ],
        ),
        out_shape=jax.ShapeDtypeStruct((n_tiles * tile, d), F32),
        compiler_params=_cparams(("arbitrary", "arbitrary")),
        name="moe_experts",
    )(tile_expert, n_sub, sub_base, src, x, g, wg_all, wu_all, wd_all)


def _combine_kernel(dest_ref, x_ref, meta_ref, ys_hbm, o_ref, buf, sem, *, tb, t):
    base = pl.program_id(0) * tb

    def start(r, carry):
        for k in range(2):
            pltpu.make_async_copy(
                ys_hbm.at[pl.ds(dest_ref[k * t + base + r], 1)], buf.at[k, pl.ds(r, 1)], sem).start()
        return carry

    lax.fori_loop(0, tb, start, 0, unroll=8)
    for k in range(2):
        pltpu.make_async_copy(ys_hbm.at[pl.ds(0, tb)], buf.at[k], sem).wait()
    meta = meta_ref[...]
    o_ref[...] = x_ref[...] + _lane_col(meta, META_P1) * buf[0] + _lane_col(meta, META_P2) * buf[1]


def _combine(x, meta, ys, dest, *, tb):
    t, d = x.shape
    return pl.pallas_call(
        functools.partial(_combine_kernel, tb=tb, t=t),
        grid_spec=pltpu.PrefetchScalarGridSpec(
            num_scalar_prefetch=1,
            grid=(t // tb,),
            in_specs=[
                pl.BlockSpec((tb, d), lambda i, dst: (i, 0)),
                pl.BlockSpec((tb, LANES), lambda i, dst: (i, 0)),
                pl.BlockSpec(memory_space=pl.ANY),
            ],
            out_specs=pl.BlockSpec((tb, d), lambda i, dst: (i, 0)),
            scratch_shapes=[pltpu.VMEM((2, tb, d), F32), pltpu.SemaphoreType.DMA(())],
        ),
        out_shape=jax.ShapeDtypeStruct((t, d), F32),
        compiler_params=_cparams(("arbitrary",)),
        name="moe_combine",
    )(dest, x, meta, ys)


def _moe(x, g, router_w, router_b, wg_all, wu_all, wd_all, j, *, tb_route, tf):
    t, d = x.shape
    w_pad = jnp.pad(router_w, ((0, 0), (0, LANES - N_EXPERTS)))
    b_pad = jnp.pad(router_b, (0, LANES - N_EXPERTS))[None, :]
    meta, cnt = _router(x, g, w_pad, b_pad, tm=tb_route)
    dest, src, tile_expert, n_sub, sub_base, n_tiles = _moe_plan(meta, cnt, t)
    ys = _experts(x, g, wg_all, wu_all, wd_all, j, tile_expert, n_sub, sub_base, src, n_tiles, tf=tf)
    return _combine(x, meta, ys, dest, tb=min(256, t))


def _ple_kernel(x_ref, g_ref, wg_ref, p_ref, wp_ref, o_ref, h_scr, *, tn):
    j = pl.program_id(1)

    @pl.when(j == 0)
    def _():
        h_scr[...] = _rms(x_ref[...], g_ref[...]).astype(BF16)

    gate = jax.nn.sigmoid(_dot(h_scr[...], wg_ref[...].astype(BF16)))
    proj = _dot(p_ref[...].astype(BF16), wp_ref[...].astype(BF16))
    o_ref[...] = x_ref[:, pl.ds(pl.multiple_of(j * tn, tn), tn)] + gate * proj


def _ple(x, g, wg_all, p_all, wp_all, layer, *, tm, tn):
    t, d = x.shape
    pd = p_all.shape[-1]
    return pl.pallas_call(
        functools.partial(_ple_kernel, tn=tn),
        grid=(t // tm, d // tn),
        in_specs=[
            pl.BlockSpec((tm, d), lambda m, j: (m, 0)),
            pl.BlockSpec((1, d), lambda m, j: (0, 0)),
            pl.BlockSpec((None, d, tn), lambda m, j: (layer, 0, j)),
            pl.BlockSpec((None, tm, pd), lambda m, j: (layer, m, 0)),
            pl.BlockSpec((None, pd, tn), lambda m, j: (layer, 0, j)),
        ],
        out_specs=pl.BlockSpec((tm, tn), lambda m, j: (m, j)),
        out_shape=jax.ShapeDtypeStruct((t, d), F32),
        scratch_shapes=[pltpu.VMEM((tm, d), BF16)],
        compiler_params=_cparams(("parallel", "arbitrary")),
        name="ple_gate",
    )(x, g, wg_all, p_all, wp_all)


def _rope_tables(seq):
    rows = seq // GRID_W
    row = jnp.broadcast_to(jnp.arange(rows, dtype=F32)[:, None], (rows, GRID_W)).reshape(seq)
    col = jnp.broadcast_to(jnp.arange(GRID_W, dtype=F32)[None, :], (rows, GRID_W)).reshape(seq)
    axis_dim = ATTN_HEAD_DIM // 2
    inv_freq = ROPE_THETA ** (-jnp.arange(0, axis_dim, 2, dtype=F32) / axis_dim)
    ar, ac = row[:, None] * inv_freq, col[:, None] * inv_freq
    cos_t = jnp.concatenate([jnp.cos(ar), jnp.cos(ar), jnp.cos(ac), jnp.cos(ac)], axis=-1)
    sin_t = jnp.concatenate([-jnp.sin(ar), jnp.sin(ar), -jnp.sin(ac), jnp.sin(ac)], axis=-1)
    return cos_t, sin_t


def kernel(x, p, norm_mix_g, w_in, conv_w, conv_b, b_igate, b_fgate, mlstm_norm_g, q_norm_g, k_norm_g, w_mlstm_up, w_attn_up, w_out, norm_ffn_g, ffn_w_gate, ffn_w_up, ffn_w_down, moe_router, moe_router_b, moe_w_gate, moe_w_up, moe_w_down, norm_ple_g, w_ple_gate, w_ple_proj):
    b, s, d = x.shape
    depth = w_in.shape[0]
    t = b * s
    col = _col_layout(d)
    cos_t, sin_t = _rope_tables(s)
    xt = x.reshape(t, d)
    p_all = p.reshape(depth, t, p.shape[-1])
    w_in_t = jnp.swapaxes(w_in, 1, 2)

    tm_big, tm_half = min(1024, t), min(512, t)
    tn = min(512, d)
    tf = 256

    for i in range(depth):
        z = _norm_matmul(xt, norm_mix_g[i][None, :], w_in_t, i, tm=tm_big, tn=tn)
        z3 = z.reshape(b, s, col["total"])
        qk3 = _conv_silu(z3, conv_w[i], conv_b[i][None, :], col0=col["qk"], tc=256)
        gate_bias = jnp.concatenate(
            [b_igate[i].reshape(-1), b_fgate[i].reshape(-1), jnp.zeros((LANES - N_GATE_COLS,), F32)])[None, :]
        hf, hb = _mlstm_scan(qk3, z3, gate_bias, v_col0=col["mv"], gate_col0=col["gates"])
        q3, k3, vt3 = _qk_prep(z3, cos_t, sin_t, q_norm_g[i][None, :], k_norm_g[i][None, :],
                               aq_col0=col["aq"], ak_col0=col["ak"], av_col0=col["av"], ts=min(512, s))
        ha = _flash_gqa(q3, k3, vt3, tq=min(256, s), tk=min(512, s))
        hm = _mlstm_out(hf.reshape(t, MLSTM_WIDTH), hb.reshape(t, MLSTM_WIDTH), z,
                        mlstm_norm_g[i].reshape(1, MLSTM_WIDTH), mo_col0=col["mo"], tm=tm_half)
        y = _merge(hm, ha.reshape(t, ATTN_WIDTH), z, w_mlstm_up, w_attn_up, i,
                   gm_col0=col["gm"], ga_col0=col["ga"], tm=tm_big, tn=tn)
        xt = _out_proj(y, w_out, i, xt, tm=tm_big, tn=tn)

        j = i // 2
        g_ffn = norm_ffn_g[i][None, :]
        if i % 2 == 0:
            xt = _ffn(xt, g_ffn, ffn_w_gate, ffn_w_up, ffn_w_down, j, tm=tm_big, tf=tf)
        else:
            xt = _moe(xt, g_ffn, moe_router[j], moe_router_b[j], moe_w_gate, moe_w_up, moe_w_down, j,
                      tb_route=tm_half, tf=tf)

        xt = _ple(xt, norm_ple_g[i][None, :], w_ple_gate, p_all, w_ple_proj, i, tm=tm_big, tn=tn)

    return xt.reshape(b, s, d)
```

```python
import functools

import jax
import jax.numpy as jnp
from jax import lax
from jax.experimental import pallas as pl
from jax.experimental.pallas import tpu as pltpu

F32 = jnp.float32
BF16 = jnp.bfloat16

GRID_W = 64
N_MLSTM_HEADS = 4
MLSTM_QK_DIM = 128
MLSTM_V_DIM = 256
MLSTM_QK_WIDTH = N_MLSTM_HEADS * MLSTM_QK_DIM
MLSTM_WIDTH = N_MLSTM_HEADS * MLSTM_V_DIM
MLSTM_CHUNK = 128
N_DIRS = 2
N_ATTN_HEADS = 8
N_KV_HEADS = 2
ATTN_HEAD_DIM = 128
ATTN_WIDTH = N_ATTN_HEADS * ATTN_HEAD_DIM
KV_WIDTH = N_KV_HEADS * ATTN_HEAD_DIM
ROPE_THETA = 10000.0
N_EXPERTS = 8
EPS = 1e-6

LANES = 128
VMEM_LIMIT_BYTES = 56 * 2**20

N_GATE_COLS = 2 * N_DIRS * N_MLSTM_HEADS


def _col_layout(d_model):
    names = ("qk", "mv", "mo", "aq", "ak", "av", "gm", "ga", "gates")
    widths = (2 * MLSTM_QK_WIDTH, MLSTM_WIDTH, MLSTM_WIDTH, ATTN_WIDTH, KV_WIDTH, KV_WIDTH, d_model, d_model, LANES)
    off, out = 0, {}
    for n, w in zip(names, widths):
        out[n] = off
        off += w
    out["total"] = off
    return out


N_HEAD_COLS = 2 * MLSTM_QK_WIDTH + MLSTM_WIDTH


def _cparams(semantics):
    return pltpu.CompilerParams(dimension_semantics=semantics, vmem_limit_bytes=VMEM_LIMIT_BYTES)


def _rms(x, g):
    ms = jnp.mean(x * x, axis=-1, keepdims=True)
    return x * lax.rsqrt(ms + EPS) * g


def _dot(a, b):
    return jnp.dot(a, b, preferred_element_type=F32)


def _dot_nt(a, b):
    return lax.dot_general(a, b, (((1,), (1,)), ((), ())), preferred_element_type=F32)


def _dot_tn(a, b):
    return lax.dot_general(a, b, (((0,), (0,)), ((), ())), preferred_element_type=F32)


def _rms_cast_kernel(x_ref, g_ref, h_ref):
    h_ref[...] = _rms(x_ref[...], g_ref[...]).astype(BF16)


def _rms_cast(x, g, *, tm):
    t, d = x.shape
    row_spec = pl.BlockSpec((tm, d), lambda m: (m, 0))
    return pl.pallas_call(
        _rms_cast_kernel,
        grid=(t // tm,),
        in_specs=[row_spec, pl.BlockSpec((1, d), lambda m: (0, 0))],
        out_specs=row_spec,
        out_shape=jax.ShapeDtypeStruct((t, d), BF16),
        compiler_params=_cparams(("parallel",)),
        name="input_rms_norm",
    )(x, g)


def _in_proj_kernel(h_ref, wm_ref, wn_ref, wg_ref, o_ref, *, head_blocks, tail_blocks):
    j = pl.program_id(1)
    ng = N_GATE_COLS

    @pl.when(j < head_blocks)
    def _():
        o_ref[...] = _dot_nt(h_ref[...], wm_ref[...].astype(BF16))

    @pl.when((j >= head_blocks) & (j < head_blocks + tail_blocks))
    def _():
        w = jnp.concatenate([wm_ref[ng:, :], wn_ref[...]], axis=0).astype(BF16)
        o_ref[...] = _dot_nt(h_ref[...], w)

    @pl.when(j == head_blocks + tail_blocks)
    def _():
        pad = jnp.zeros((wm_ref.shape[0] - ng, wm_ref.shape[1]), F32)
        w = jnp.concatenate([wg_ref[...], pad], axis=0).astype(BF16)
        o_ref[...] = _dot_nt(h_ref[...], w)


def _in_proj(h, wt_all, layer, *, tm, tn):
    t, d = h.shape
    n_cols = wt_all.shape[1]
    ng = N_GATE_COLS
    tail = n_cols - N_HEAD_COLS - ng
    head_blocks, tail_blocks = N_HEAD_COLS // tn, tail // tn
    assert tail % tn == 0 and N_HEAD_COLS % tn == 0 and n_cols % ng == 0
    last_main = head_blocks + tail_blocks - 1
    return pl.pallas_call(
        functools.partial(_in_proj_kernel, head_blocks=head_blocks, tail_blocks=tail_blocks),
        grid=(t // tm, head_blocks + tail_blocks + 1),
        in_specs=[
            pl.BlockSpec((tm, d), lambda m, j: (m, 0)),
            pl.BlockSpec((None, tn, d), lambda m, j: (layer, jnp.minimum(j, last_main), 0)),
            pl.BlockSpec((None, ng, d), lambda m, j: (layer, jnp.minimum((j + 1) * (tn // ng), n_cols // ng - 1), 0)),
            pl.BlockSpec((None, ng, d), lambda m, j: (layer, N_HEAD_COLS // ng, 0)),
        ],
        out_specs=pl.BlockSpec((tm, tn), lambda m, j: (m, j)),
        out_shape=jax.ShapeDtypeStruct((t, N_HEAD_COLS + tail + LANES), F32),
        compiler_params=_cparams(("parallel", "parallel")),
        name="in_proj",
    )(h, wt_all, wt_all, wt_all)


def _conv_silu_kernel(z_ref, w_ref, b_ref, o_ref, *, q_blocks, q_scale):
    x = z_ref[0]
    s = x.shape[0]
    row = lax.broadcasted_iota(jnp.int32, x.shape, 0)
    prev = jnp.where(row == 0, 0.0, pltpu.roll(x, 1, 0))
    nxt = jnp.where(row == s - 1, 0.0, pltpu.roll(x, s - 1, 0))
    y = prev * w_ref[0:1, :] + x * w_ref[1:2, :] + nxt * w_ref[2:3, :] + b_ref[...]
    y = y * jax.nn.sigmoid(y)
    scale = jnp.where(pl.program_id(1) < q_blocks, q_scale, 1.0).astype(F32)
    o_ref[0] = (y * scale).astype(o_ref.dtype)


def _conv_silu(z3, conv_w, conv_b, *, col0, tc):
    b, s, _ = z3.shape
    width = conv_w.shape[1]
    kern = functools.partial(_conv_silu_kernel, q_blocks=MLSTM_QK_WIDTH // tc, q_scale=MLSTM_QK_DIM ** -0.5)
    return pl.pallas_call(
        kern,
        grid=(b, width // tc),
        in_specs=[
            pl.BlockSpec((1, s, tc), lambda i, j: (i, 0, col0 // tc + j)),
            pl.BlockSpec((3, tc), lambda i, j: (0, j)),
            pl.BlockSpec((1, tc), lambda i, j: (0, j)),
        ],
        out_specs=pl.BlockSpec((1, s, tc), lambda i, j: (i, 0, j)),
        out_shape=jax.ShapeDtypeStruct((b, s, width), BF16),
        compiler_params=_cparams(("parallel", "parallel")),
        name="mlstm_conv_silu",
    )(z3, conv_w, conv_b)


def _log_sigmoid(x):
    return jnp.minimum(x, 0.0) - jnp.log1p(jnp.exp(-jnp.abs(x)))


def _lane_col(x, idx):
    lane = lax.broadcasted_iota(jnp.int32, x.shape, 1)
    return jnp.sum(jnp.where(lane == idx, x, 0.0), axis=-1, keepdims=True)


def _mlstm_kernel(qkf_ref, qkb_ref, vf_ref, vb_ref, gf_ref, gb_ref, bias_ref, hf_ref, hb_ref, c_scr, m_scr):
    L = MLSTM_CHUNK
    dk, dv, nh = MLSTM_QK_DIM, MLSTM_V_DIM, N_MLSTM_HEADS

    @pl.when(pl.program_id(1) == 0)
    def _():
        c_scr[...] = jnp.zeros_like(c_scr)
        m_scr[...] = jnp.zeros_like(m_scr)

    r_i = lax.broadcasted_iota(jnp.int32, (L, L), 0)
    c_i = lax.broadcasted_iota(jnp.int32, (L, L), 1)
    lane = lax.broadcasted_iota(jnp.int32, (L, LANES), 1)
    ones_blk = jnp.where(lane == 0, 1.0, 0.0).astype(F32)

    dirs = ((qkf_ref, vf_ref, gf_ref, hf_ref), (qkb_ref, vb_ref, gb_ref, hb_ref))
    for d, (qk_ref, v_ref, g_ref, h_ref) in enumerate(dirs):
        mask = (c_i <= r_i) if d == 0 else (c_i >= r_i)
        g = g_ref[0] + bias_ref[...]
        gp = jnp.where(lane < N_GATE_COLS // 2, g, _log_sigmoid(g))
        gp_t = gp.T
        bc = jnp.dot(mask.astype(F32), gp, precision=lax.Precision.HIGHEST, preferred_element_type=F32)
        bc_t = bc.T
        end_row = L - 1 if d == 0 else 0
        for h in range(nh):
            ci = d * nh + h
            ch_i, ch_f = d * nh + h, N_GATE_COLS // 2 + d * nh + h
            i_row = gp_t[ch_i:ch_i + 1, :]
            bc_row = bc_t[ch_f:ch_f + 1, :]
            i_col = _lane_col(gp, ch_i)
            bc_col = _lane_col(bc, ch_f)
            m_prev = m_scr[ci, 0:1, 0:1]
            c_prev = c_scr[ci]

            a_col = bc_col + m_prev
            dm = jnp.where(mask, bc_col - bc_row + i_row, -jnp.inf)
            m_t = jnp.maximum(a_col, jnp.max(dm, axis=-1, keepdims=True))
            w_inter = jnp.exp(a_col - m_t)
            w_intra = jnp.exp(dm - m_t)

            q = qk_ref[0, :, h * dk:(h + 1) * dk]
            k = qk_ref[0, :, MLSTM_QK_WIDTH + h * dk:MLSTM_QK_WIDTH + (h + 1) * dk]
            v_ext = jnp.concatenate([v_ref[0, :, h * dv:(h + 1) * dv], ones_blk], axis=1)
            p = (_dot_nt(q, k) * w_intra).astype(BF16)
            num_ext = w_inter * _dot(q, c_prev.astype(BF16)) + _dot(p, v_ext.astype(BF16))
            den = num_ext[:, dv:dv + 1]
            h_ref[0, :, h * dv:(h + 1) * dv] = num_ext[:, :dv] / jnp.maximum(jnp.abs(den), jnp.exp(-m_t))

            b_end = bc_col[end_row:end_row + 1, :]
            g_col = b_end - bc_col + i_col
            m_new = jnp.maximum(b_end + m_prev, jnp.max(g_col, axis=0, keepdims=True))
            decay = jnp.exp(b_end + m_prev - m_new)
            ws = jnp.exp(g_col - m_new)
            c_scr[ci] = decay * c_prev + _dot_tn(k, (ws * v_ext).astype(BF16))
            m_scr[ci] = jnp.broadcast_to(m_new, m_scr.shape[1:])


def _mlstm_scan(qk3, z3, gate_bias, *, v_col0, gate_col0):
    b, s, _ = qk3.shape
    L = MLSTM_CHUNK
    nc = s // L
    n_chain = N_DIRS * N_MLSTM_HEADS
    vb, gb = v_col0 // MLSTM_WIDTH, gate_col0 // LANES
    qk_w = 2 * MLSTM_QK_WIDTH
    return pl.pallas_call(
        _mlstm_kernel,
        grid=(b, nc),
        in_specs=[
            pl.BlockSpec((1, L, qk_w), lambda i, c: (i, c, 0)),
            pl.BlockSpec((1, L, qk_w), lambda i, c: (i, nc - 1 - c, 0)),
            pl.BlockSpec((1, L, MLSTM_WIDTH), lambda i, c: (i, c, vb)),
            pl.BlockSpec((1, L, MLSTM_WIDTH), lambda i, c: (i, nc - 1 - c, vb)),
            pl.BlockSpec((1, L, LANES), lambda i, c: (i, c, gb)),
            pl.BlockSpec((1, L, LANES), lambda i, c: (i, nc - 1 - c, gb)),
            pl.BlockSpec((1, LANES), lambda i, c: (0, 0)),
        ],
        out_specs=[
            pl.BlockSpec((1, L, MLSTM_WIDTH), lambda i, c: (i, c, 0)),
            pl.BlockSpec((1, L, MLSTM_WIDTH), lambda i, c: (i, nc - 1 - c, 0)),
        ],
        out_shape=[jax.ShapeDtypeStruct((b, s, MLSTM_WIDTH), F32)] * 2,
        scratch_shapes=[
            pltpu.VMEM((n_chain, MLSTM_QK_DIM, MLSTM_V_DIM + LANES), F32),
            pltpu.VMEM((n_chain, 8, LANES), F32),
        ],
        compiler_params=_cparams(("parallel", "arbitrary")),
        name="mlstm_scan",
    )(qk3, qk3, z3, z3, z3, z3, gate_bias)


LOG2_E = 1.4426950408889634


def _qk_prep_kernel(aq_ref, ak_ref, av_ref, cos_ref, sin_ref, gq_ref, gk_ref, q_ref, k_ref, vt_ref):
    cos, sin = cos_ref[...], sin_ref[...]
    hd = ATTN_HEAD_DIM
    lane = lax.broadcasted_iota(jnp.int32, cos.shape, 1)
    first_half = (lane % (hd // 2)) < (hd // 4)

    def prep(x, g, scale):
        n = _rms(x, g)
        partner = jnp.where(first_half, pltpu.roll(n, hd - hd // 4, 1), pltpu.roll(n, hd // 4, 1))
        return ((n * cos + partner * sin) * scale).astype(BF16)

    for h in range(N_ATTN_HEADS):
        q_ref[0, :, h * hd:(h + 1) * hd] = prep(aq_ref[0, :, h * hd:(h + 1) * hd], gq_ref[...], hd ** -0.5 * LOG2_E)
    for h in range(N_KV_HEADS):
        k_ref[0, :, h * hd:(h + 1) * hd] = prep(ak_ref[0, :, h * hd:(h + 1) * hd], gk_ref[...], 1.0)
    vt_ref[0] = av_ref[0].T.astype(BF16)


def _qk_prep(z3, cos_t, sin_t, gq, gk, *, aq_col0, ak_col0, av_col0, ts):
    b, s, _ = z3.shape
    return pl.pallas_call(
        _qk_prep_kernel,
        grid=(b, s // ts),
        in_specs=[
            pl.BlockSpec((1, ts, ATTN_WIDTH), lambda i, j: (i, j, aq_col0 // ATTN_WIDTH)),
            pl.BlockSpec((1, ts, KV_WIDTH), lambda i, j: (i, j, ak_col0 // KV_WIDTH)),
            pl.BlockSpec((1, ts, KV_WIDTH), lambda i, j: (i, j, av_col0 // KV_WIDTH)),
            pl.BlockSpec((ts, ATTN_HEAD_DIM), lambda i, j: (j, 0)),
            pl.BlockSpec((ts, ATTN_HEAD_DIM), lambda i, j: (j, 0)),
            pl.BlockSpec((1, ATTN_HEAD_DIM), lambda i, j: (0, 0)),
            pl.BlockSpec((1, ATTN_HEAD_DIM), lambda i, j: (0, 0)),
        ],
        out_specs=[
            pl.BlockSpec((1, ts, ATTN_WIDTH), lambda i, j: (i, j, 0)),
            pl.BlockSpec((1, ts, KV_WIDTH), lambda i, j: (i, j, 0)),
            pl.BlockSpec((1, KV_WIDTH, ts), lambda i, j: (i, 0, j)),
        ],
        out_shape=[jax.ShapeDtypeStruct((b, s, ATTN_WIDTH), BF16), jax.ShapeDtypeStruct((b, s, KV_WIDTH), BF16),
                   jax.ShapeDtypeStruct((b, KV_WIDTH, s), BF16)],
        compiler_params=_cparams(("parallel", "parallel")),
        name="attn_qk_norm_rope",
    )(z3, z3, z3, cos_t, sin_t, gq, gk)


SOFTMAX_SHIFT_LIMIT = 60.0


def _flash_kernel(bound_ref, q_ref, k_ref, vt_ref, o_ref, *, grp, tk):
    hd = ATTN_HEAD_DIM
    tq = q_ref.shape[1]
    s_len = k_ref.shape[1]
    rows = grp * tq
    bound = bound_ref[0, 0]
    q = jnp.concatenate([q_ref[0, :, g * hd:(g + 1) * hd] for g in range(grp)], axis=0)
    chunks = [slice(c * tk, (c + 1) * tk) for c in range(s_len // tk)]

    def store(acc, l):
        o = (acc / l).T
        for g in range(grp):
            o_ref[0, :, g * hd:(g + 1) * hd] = o[g * tq:(g + 1) * tq, :].astype(o_ref.dtype)

    @pl.when(bound <= SOFTMAX_SHIFT_LIMIT)
    def _():
        l = jnp.zeros((1, rows), F32)
        acc = jnp.zeros((hd, rows), F32)
        for ck in chunks:
            p = jnp.exp2(_dot_nt(k_ref[0, ck, :], q) - bound)
            l = l + jnp.sum(p, axis=0, keepdims=True)
            acc = acc + _dot(vt_ref[0, :, ck], p.astype(BF16))
        store(acc, l)

    @pl.when(bound > SOFTMAX_SHIFT_LIMIT)
    def _():
        m = jnp.full((1, rows), -jnp.inf, F32)
        l = jnp.zeros((1, rows), F32)
        acc = jnp.zeros((hd, rows), F32)
        for ck in chunks:
            st = _dot_nt(k_ref[0, ck, :], q)
            m_new = jnp.maximum(m, jnp.max(st, axis=0, keepdims=True))
            alpha = jnp.exp2(m - m_new)
            p = jnp.exp2(st - m_new)
            l = alpha * l + jnp.sum(p, axis=0, keepdims=True)
            acc = alpha * acc + _dot(vt_ref[0, :, ck], p.astype(BF16))
            m = m_new
        store(acc, l)


def _score_bound(gq, gk):
    scale = ATTN_HEAD_DIM ** -0.5 * LOG2_E
    return (1.02 * ATTN_HEAD_DIM * scale * jnp.max(jnp.abs(gq)) * jnp.max(jnp.abs(gk))).reshape(1, 1).astype(F32)


def _flash_gqa(q3, k3, vt3, bound, *, tq, tk):
    b, s, _ = q3.shape
    hd = ATTN_HEAD_DIM
    grp = N_ATTN_HEADS // N_KV_HEADS
    return pl.pallas_call(
        functools.partial(_flash_kernel, grp=grp, tk=tk),
        grid=(b, N_KV_HEADS, s // tq),
        in_specs=[
            pl.BlockSpec(memory_space=pltpu.SMEM),
            pl.BlockSpec((1, tq, grp * hd), lambda i, h, qi: (i, qi, h)),
            pl.BlockSpec((1, s, hd), lambda i, h, qi: (i, 0, h)),
            pl.BlockSpec((1, hd, s), lambda i, h, qi: (i, h, 0)),
        ],
        out_specs=pl.BlockSpec((1, tq, grp * hd), lambda i, h, qi: (i, qi, h)),
        out_shape=jax.ShapeDtypeStruct((b, s, ATTN_WIDTH), BF16),
        compiler_params=_cparams(("parallel", "parallel", "parallel")),
        name="gqa_flash",
    )(bound, q3, k3, vt3)


def _mlstm_out_kernel(hf_ref, hb_ref, mo_ref, ng_ref, hm_ref):
    dv = MLSTM_V_DIM
    for h in range(N_MLSTM_HEADS):
        sl = slice(h * dv, (h + 1) * dv)
        hm = _rms(hf_ref[:, sl] + hb_ref[:, sl], ng_ref[:, sl])
        hm_ref[:, sl] = (hm * jax.nn.sigmoid(mo_ref[:, sl])).astype(BF16)


def _mlstm_out(hf, hb, z, ng, *, mo_col0, tm):
    t, w = hf.shape
    row_spec = pl.BlockSpec((tm, w), lambda m: (m, 0))
    return pl.pallas_call(
        _mlstm_out_kernel,
        grid=(t // tm,),
        in_specs=[row_spec, row_spec, pl.BlockSpec((tm, w), lambda m: (m, mo_col0 // w)),
                  pl.BlockSpec((1, w), lambda m: (0, 0))],
        out_specs=row_spec,
        out_shape=jax.ShapeDtypeStruct((t, w), BF16),
        compiler_params=_cparams(("parallel",)),
        name="mlstm_out_norm",
    )(hf, hb, z, ng)


def _merge_kernel(hm_ref, ha_ref, gm_ref, ga_ref, wm_ref, wa_ref, y_ref, wmb_scr, wab_scr):
    @pl.when(pl.program_id(1) == 0)
    def _():
        wmb_scr[...] = wm_ref[...].astype(BF16)
        wab_scr[...] = wa_ref[...].astype(BF16)

    ym = _dot(hm_ref[...], wmb_scr[...])
    ya = _dot(ha_ref[...], wab_scr[...])
    y_ref[...] = (jax.nn.sigmoid(gm_ref[...]) * ym + jax.nn.sigmoid(ga_ref[...]) * ya).astype(y_ref.dtype)


def _merge(hm, ha, z, wm_all, wa_all, layer, *, gm_col0, ga_col0, tm, tn):
    t, w = hm.shape
    d = wm_all.shape[2]
    act_spec = pl.BlockSpec((tm, w), lambda j, m: (m, 0))
    w_spec = pl.BlockSpec((None, w, tn), lambda j, m: (layer, 0, j))
    return pl.pallas_call(
        _merge_kernel,
        grid=(d // tn, t // tm),
        in_specs=[
            act_spec,
            act_spec,
            pl.BlockSpec((tm, tn), lambda j, m: (m, gm_col0 // tn + j)),
            pl.BlockSpec((tm, tn), lambda j, m: (m, ga_col0 // tn + j)),
            w_spec,
            w_spec,
        ],
        out_specs=pl.BlockSpec((tm, tn), lambda j, m: (m, j)),
        out_shape=jax.ShapeDtypeStruct((t, d), BF16),
        scratch_shapes=[pltpu.VMEM((w, tn), BF16), pltpu.VMEM((w, tn), BF16)],
        compiler_params=_cparams(("parallel", "arbitrary")),
        name="branch_merge",
    )(hm, ha, z, z, wm_all, wa_all)


def _resident_weight_spec(shape, index):
    return pl.BlockSpec(shape, lambda *_: index, pipeline_mode=pl.Buffered(1))


def _out_proj_kernel(y_ref, w_ref, x_ref, o_ref, wb_scr):
    @pl.when(pl.program_id(0) == 0)
    def _():
        wb_scr[...] = w_ref[...].astype(BF16)

    o_ref[...] = x_ref[...] + _dot(y_ref[...], wb_scr[...])


def _out_proj(y, w_all, layer, x, *, tm):
    t, d = x.shape
    k = y.shape[1]
    return pl.pallas_call(
        _out_proj_kernel,
        grid=(t // tm,),
        in_specs=[
            pl.BlockSpec((tm, k), lambda m: (m, 0)),
            _resident_weight_spec((None, k, d), (layer, 0, 0)),
            pl.BlockSpec((tm, d), lambda m: (m, 0)),
        ],
        out_specs=pl.BlockSpec((tm, d), lambda m: (m, 0)),
        out_shape=jax.ShapeDtypeStruct((t, d), F32),
        scratch_shapes=[pltpu.VMEM((k, d), BF16)],
        compiler_params=_cparams(("arbitrary",)),
        name="out_proj_residual",
    )(y, w_all, x)


def _swiglu_partial(h, wg_ref, wu_ref, wd_ref):
    a = _dot(h, wg_ref[...].astype(BF16))
    u = _dot(h, wu_ref[...].astype(BF16))
    t = a * jax.nn.sigmoid(a) * u
    return _dot(t.astype(BF16), wd_ref[...].astype(BF16))


def _ffn_kernel(x_ref, g_ref, wg_ref, wu_ref, wd_ref, o_ref, h_scr):
    @pl.when(pl.program_id(1) == 0)
    def _():
        x = x_ref[...]
        h_scr[...] = _rms(x, g_ref[...]).astype(BF16)
        o_ref[...] = x

    o_ref[...] += _swiglu_partial(h_scr[...], wg_ref, wu_ref, wd_ref)


def _ffn(x, g, wg_all, wu_all, wd_all, j, *, tm, tf):
    t, d = x.shape
    ff = wg_all.shape[-1]
    w_in_spec = pl.BlockSpec((None, d, tf), lambda m, f: (j, 0, f))
    return pl.pallas_call(
        _ffn_kernel,
        grid=(t // tm, ff // tf),
        in_specs=[
            pl.BlockSpec((tm, d), lambda m, f: (m, 0)),
            pl.BlockSpec((1, d), lambda m, f: (0, 0)),
            w_in_spec,
            w_in_spec,
            pl.BlockSpec((None, tf, d), lambda m, f: (j, f, 0)),
        ],
        out_specs=pl.BlockSpec((tm, d), lambda m, f: (m, 0)),
        out_shape=jax.ShapeDtypeStruct((t, d), F32),
        scratch_shapes=[pltpu.VMEM((tm, d), BF16)],
        compiler_params=_cparams(("parallel", "arbitrary")),
        name="dense_swiglu",
    )(x, g, wg_all, wu_all, wd_all)


MOE_ROW_TILE = 2560
MOE_SUB_TILE = 512
META_E1, META_E2, META_P1, META_P2, META_R1, META_R2 = range(6)


def _router_kernel(x_ref, g_ref, w_ref, b_ref, meta_ref, cnt_ref, carry_scr):
    @pl.when(pl.program_id(0) == 0)
    def _():
        carry_scr[...] = jnp.zeros_like(carry_scr)

    h = _rms(x_ref[...], g_ref[...])
    logits = jnp.dot(h, w_ref[...], precision=lax.Precision.HIGHEST, preferred_element_type=F32) + b_ref[...]
    tb = logits.shape[0]
    lane = lax.broadcasted_iota(jnp.int32, logits.shape, 1)
    logits = jnp.where(lane < N_EXPERTS, logits, -jnp.inf)
    v1 = jnp.max(logits, axis=-1, keepdims=True)
    i1 = jnp.min(jnp.where(logits == v1, lane, LANES), axis=-1, keepdims=True)
    rest = jnp.where(lane == i1, -jnp.inf, logits)
    v2 = jnp.max(rest, axis=-1, keepdims=True)
    i2 = jnp.min(jnp.where(rest == v2, lane, LANES), axis=-1, keepdims=True)
    e2 = jnp.exp(v2 - v1)
    p1 = 1.0 / (1.0 + e2)
    p2 = e2 / (1.0 + e2)
    sel = ((lane == i1) | (lane == i2)).astype(F32)
    r_i = lax.broadcasted_iota(jnp.int32, (tb, tb), 0)
    c_i = lax.broadcasted_iota(jnp.int32, (tb, tb), 1)
    earlier = (c_i < r_i).astype(BF16)
    rank = carry_scr[0:1, :] + _dot(earlier, sel.astype(BF16))
    r1 = jnp.sum(jnp.where(lane == i1, rank, 0.0), axis=-1, keepdims=True)
    r2 = jnp.sum(jnp.where(lane == i2, rank, 0.0), axis=-1, keepdims=True)
    meta = jnp.zeros_like(logits)
    for idx, val in ((META_E1, i1.astype(F32)), (META_E2, i2.astype(F32)), (META_P1, p1), (META_P2, p2),
                     (META_R1, r1), (META_R2, r2)):
        meta = jnp.where(lane == idx, val, meta)
    meta_ref[...] = meta
    total = carry_scr[0:1, :] + jnp.sum(sel, axis=0, keepdims=True)
    carry_scr[...] = jnp.broadcast_to(total, carry_scr.shape)
    cnt_ref[...] = jnp.broadcast_to(total, cnt_ref.shape)


def _router(x, g, w_pad, b_pad, *, tm):
    t, d = x.shape
    return pl.pallas_call(
        _router_kernel,
        grid=(t // tm,),
        in_specs=[
            pl.BlockSpec((tm, d), lambda m: (m, 0)),
            pl.BlockSpec((1, d), lambda m: (0, 0)),
            pl.BlockSpec((d, LANES), lambda m: (0, 0)),
            pl.BlockSpec((1, LANES), lambda m: (0, 0)),
        ],
        out_specs=[pl.BlockSpec((tm, LANES), lambda m: (m, 0)), pl.BlockSpec((8, LANES), lambda m: (0, 0))],
        out_shape=[jax.ShapeDtypeStruct((t, LANES), F32), jax.ShapeDtypeStruct((8, LANES), F32)],
        scratch_shapes=[pltpu.VMEM((8, LANES), F32)],
        compiler_params=_cparams(("arbitrary",)),
        name="moe_router",
    )(x, g, w_pad, b_pad)


def _moe_plan(meta, cnt, t):
    tile, sub = MOE_ROW_TILE, MOE_SUB_TILE
    n_tiles = 2 * t // tile + N_EXPERTS
    e1 = meta[:, META_E1].astype(jnp.int32)
    e2 = meta[:, META_E2].astype(jnp.int32)
    r1 = meta[:, META_R1].astype(jnp.int32)
    r2 = meta[:, META_R2].astype(jnp.int32)
    counts = cnt[0, :N_EXPERTS].astype(jnp.int32)
    tiles_e = (counts + tile - 1) // tile
    tile_end = jnp.cumsum(tiles_e)
    tile_start = tile_end - tiles_e
    row_start = tile_start * tile
    dest = jnp.concatenate([row_start[e1] + r1, row_start[e2] + r2]).astype(jnp.int32)
    m = jnp.arange(n_tiles, dtype=jnp.int32)
    live = m < tile_end[-1]
    m_eff = jnp.where(live, m, jnp.maximum(tile_end[-1] - 1, 0))
    tile_expert = jnp.minimum(jnp.searchsorted(tile_end, m_eff, side="right"), N_EXPERTS - 1).astype(jnp.int32)
    rows_left = counts[tile_expert] - (m_eff - tile_start[tile_expert]) * tile
    n_sub = jnp.where(live, jnp.clip((rows_left + sub - 1) // sub, 0, tile // sub), 0).astype(jnp.int32)
    sub_base = (jnp.cumsum(n_sub) - n_sub).astype(jnp.int32)
    tok = jnp.arange(t, dtype=jnp.int32)
    src_pos = sub_base[dest // tile] * sub + dest % tile
    n_src = (2 * t // sub + N_EXPERTS) * sub
    src = jnp.zeros((n_src,), jnp.int32).at[src_pos].set(jnp.concatenate([tok, tok]))
    return dest, src, tile_expert, n_sub, sub_base, n_tiles


def _expert_kernel(te_ref, ns_ref, sbase_ref, src_ref, x_hbm, g_ref, wg_ref, wu_ref, wd_ref, ys_hbm,
                   xbuf, h_scr, acc, gsem, zsem, osem):
    del te_ref
    m, f = pl.program_id(0), pl.program_id(1)
    n_m, n_f = pl.num_programs(0), pl.num_programs(1)
    tile, sub = MOE_ROW_TILE, MOE_SUB_TILE
    n_live = ns_ref[m]
    subs = [(sb, slice(sb * sub, (sb + 1) * sub)) for sb in range(tile // sub)]

    def gather_start(sb, slot):
        def start(r, carry):
            tok = src_ref[(sbase_ref[m] + sb) * sub + r]
            pltpu.make_async_copy(x_hbm.at[pl.ds(tok, 1)], xbuf.at[slot, pl.ds(r, 1)], gsem.at[slot]).start()
            return carry

        lax.fori_loop(0, sub, start, 0, unroll=8)

    def result_copy(sb, rows, tile_idx):
        return pltpu.make_async_copy(acc.at[rows], ys_hbm.at[pl.ds(tile_idx * tile + sb * sub, sub)], osem)

    def zero_copy(sb):
        return pltpu.make_async_copy(xbuf.at[0], ys_hbm.at[pl.ds(m * tile + sb * sub, sub)], zsem)

    @pl.when(f == 0)
    def _():
        @pl.when(n_live > 0)
        def _():
            gather_start(0, 0)

        for sb, rows in subs:
            slot = sb % 2
            if sb + 1 < len(subs):
                @pl.when(sb + 1 < n_live)
                def _():
                    gather_start(sb + 1, 1 - slot)

            @pl.when(sb < n_live)
            def _():
                pltpu.make_async_copy(x_hbm.at[pl.ds(0, sub)], xbuf.at[slot], gsem.at[slot]).wait()
                h_scr[rows, :] = _rms(xbuf[slot], g_ref[...]).astype(BF16)

        @pl.when(m > 0)
        def _():
            for sb, rows in subs:
                @pl.when(sb < ns_ref[m - 1])
                def _():
                    result_copy(sb, rows, m - 1).wait()

        for sb, rows in subs:
            @pl.when(sb < n_live)
            def _():
                acc[rows, :] = jnp.zeros((sub, acc.shape[1]), F32)

    @pl.when(f == 1)
    def _():
        xbuf[0] = jnp.zeros(xbuf.shape[1:], F32)
        for sb, rows in subs:
            @pl.when(sb >= n_live)
            def _():
                zero_copy(sb).start()

    for sb, rows in subs:
        @pl.when(sb < n_live)
        def _():
            acc[rows, :] += _swiglu_partial(h_scr[rows, :], wg_ref, wu_ref, wd_ref)

    @pl.when(f == n_f - 1)
    def _():
        for sb, rows in subs:
            @pl.when(sb < n_live)
            def _():
                result_copy(sb, rows, m).start()

            @pl.when(sb >= n_live)
            def _():
                zero_copy(sb).wait()

        @pl.when(m == n_m - 1)
        def _():
            for sb, rows in subs:
                @pl.when(sb < n_live)
                def _():
                    result_copy(sb, rows, m).wait()


def _experts(x, g, wg_all, wu_all, wd_all, j, tile_expert, n_sub, sub_base, src, n_tiles, *, tf):
    t, d = x.shape
    tile = MOE_ROW_TILE
    ff = wg_all.shape[-1]
    n_f = ff // tf

    def f_eff(m, f, ns):
        return jnp.where(ns[m] > 0, f, n_f - 1)

    w_in_spec = pl.BlockSpec((None, None, d, tf), lambda m, f, te, ns, sbase, sr: (j, te[m], 0, f_eff(m, f, ns)))
    w_dn_spec = pl.BlockSpec((None, None, tf, d), lambda m, f, te, ns, sbase, sr: (j, te[m], f_eff(m, f, ns), 0))
    return pl.pallas_call(
        _expert_kernel,
        grid_spec=pltpu.PrefetchScalarGridSpec(
            num_scalar_prefetch=4,
            grid=(n_tiles, n_f),
            in_specs=[
                pl.BlockSpec(memory_space=pl.ANY),
                pl.BlockSpec((1, d), lambda m, f, te, ns, sbase, sr: (0, 0)),
                w_in_spec,
                w_in_spec,
                w_dn_spec,
            ],
            out_specs=pl.BlockSpec(memory_space=pl.ANY),
            scratch_shapes=[
                pltpu.VMEM((2, MOE_SUB_TILE, d), F32),
                pltpu.VMEM((tile, d), BF16),
                pltpu.VMEM((tile, d), F32),
                pltpu.SemaphoreType.DMA((2,)),
                pltpu.SemaphoreType.DMA(()),
                pltpu.SemaphoreType.DMA(()),
            ],
        ),
        out_shape=jax.ShapeDtypeStruct((n_tiles * tile, d), F32),
        compiler_params=_cparams(("arbitrary", "arbitrary")),
        name="moe_experts",
    )(tile_expert, n_sub, sub_base, src, x, g, wg_all, wu_all, wd_all)


def _combine_kernel(dest_ref, x_ref, meta_ref, ys_hbm, o_ref, buf, sem, *, tb, t):
    base = pl.program_id(0) * tb

    def start(r, carry):
        for k in range(2):
            pltpu.make_async_copy(
                ys_hbm.at[pl.ds(dest_ref[k * t + base + r], 1)], buf.at[k, pl.ds(r, 1)], sem).start()
        return carry

    lax.fori_loop(0, tb, start, 0, unroll=8)
    for k in range(2):
        pltpu.make_async_copy(ys_hbm.at[pl.ds(0, tb)], buf.at[k], sem).wait()
    meta = meta_ref[...]
    o_ref[...] = x_ref[...] + _lane_col(meta, META_P1) * buf[0] + _lane_col(meta, META_P2) * buf[1]


def _combine(x, meta, ys, dest, *, tb):
    t, d = x.shape
    return pl.pallas_call(
        functools.partial(_combine_kernel, tb=tb, t=t),
        grid_spec=pltpu.PrefetchScalarGridSpec(
            num_scalar_prefetch=1,
            grid=(t // tb,),
            in_specs=[
                pl.BlockSpec((tb, d), lambda i, dst: (i, 0)),
                pl.BlockSpec((tb, LANES), lambda i, dst: (i, 0)),
                pl.BlockSpec(memory_space=pl.ANY),
            ],
            out_specs=pl.BlockSpec((tb, d), lambda i, dst: (i, 0)),
            scratch_shapes=[pltpu.VMEM((2, tb, d), F32), pltpu.SemaphoreType.DMA(())],
        ),
        out_shape=jax.ShapeDtypeStruct((t, d), F32),
        compiler_params=_cparams(("arbitrary",)),
        name="moe_combine",
    )(dest, x, meta, ys)


def _moe(x, g, router_w, router_b, wg_all, wu_all, wd_all, j, *, tb_route, tf):
    t, d = x.shape
    w_pad = jnp.pad(router_w, ((0, 0), (0, LANES - N_EXPERTS)))
    b_pad = jnp.pad(router_b, (0, LANES - N_EXPERTS))[None, :]
    meta, cnt = _router(x, g, w_pad, b_pad, tm=tb_route)
    dest, src, tile_expert, n_sub, sub_base, n_tiles = _moe_plan(meta, cnt, t)
    ys = _experts(x, g, wg_all, wu_all, wd_all, j, tile_expert, n_sub, sub_base, src, n_tiles, tf=tf)
    return _combine(x, meta, ys, dest, tb=min(256, t))


def _ple_kernel(x_ref, g_ref, wg_ref, p_ref, wp_ref, gn_ref, o_ref, hn_ref, wgb_scr, wpb_scr):
    @pl.when(pl.program_id(0) == 0)
    def _():
        wgb_scr[...] = wg_ref[...].astype(BF16)
        wpb_scr[...] = wp_ref[...].astype(BF16)

    x = x_ref[...]
    gate = jax.nn.sigmoid(_dot(_rms(x, g_ref[...]).astype(BF16), wgb_scr[...]))
    x_new = x + gate * _dot(p_ref[...].astype(BF16), wpb_scr[...])
    o_ref[...] = x_new
    hn_ref[...] = _rms(x_new, gn_ref[...]).astype(BF16)


def _ple(x, g, wg_all, p_all, wp_all, layer, g_next, *, tm):
    t, d = x.shape
    pd = p_all.shape[-1]
    row_spec = pl.BlockSpec((tm, d), lambda m: (m, 0))
    gain_spec = pl.BlockSpec((1, d), lambda m: (0, 0))
    return pl.pallas_call(
        _ple_kernel,
        grid=(t // tm,),
        in_specs=[
            row_spec,
            gain_spec,
            _resident_weight_spec((None, d, d), (layer, 0, 0)),
            pl.BlockSpec((None, tm, pd), lambda m: (layer, m, 0)),
            _resident_weight_spec((None, pd, d), (layer, 0, 0)),
            gain_spec,
        ],
        out_specs=[row_spec, row_spec],
        out_shape=[jax.ShapeDtypeStruct((t, d), F32), jax.ShapeDtypeStruct((t, d), BF16)],
        scratch_shapes=[pltpu.VMEM((d, d), BF16), pltpu.VMEM((pd, d), BF16)],
        compiler_params=_cparams(("arbitrary",)),
        name="ple_gate",
    )(x, g, wg_all, p_all, wp_all, g_next)


def _rope_tables(seq):
    rows = seq // GRID_W
    row = jnp.broadcast_to(jnp.arange(rows, dtype=F32)[:, None], (rows, GRID_W)).reshape(seq)
    col = jnp.broadcast_to(jnp.arange(GRID_W, dtype=F32)[None, :], (rows, GRID_W)).reshape(seq)
    axis_dim = ATTN_HEAD_DIM // 2
    inv_freq = ROPE_THETA ** (-jnp.arange(0, axis_dim, 2, dtype=F32) / axis_dim)
    ar, ac = row[:, None] * inv_freq, col[:, None] * inv_freq
    cos_t = jnp.concatenate([jnp.cos(ar), jnp.cos(ar), jnp.cos(ac), jnp.cos(ac)], axis=-1)
    sin_t = jnp.concatenate([-jnp.sin(ar), jnp.sin(ar), -jnp.sin(ac), jnp.sin(ac)], axis=-1)
    return cos_t, sin_t


def kernel(x, p, norm_mix_g, w_in, conv_w, conv_b, b_igate, b_fgate, mlstm_norm_g, q_norm_g, k_norm_g, w_mlstm_up, w_attn_up, w_out, norm_ffn_g, ffn_w_gate, ffn_w_up, ffn_w_down, moe_router, moe_router_b, moe_w_gate, moe_w_up, moe_w_down, norm_ple_g, w_ple_gate, w_ple_proj):
    b, s, d = x.shape
    depth = w_in.shape[0]
    t = b * s
    col = _col_layout(d)
    cos_t, sin_t = _rope_tables(s)
    xt = x.reshape(t, d)
    p_all = p.reshape(depth, t, p.shape[-1])
    w_in_t = jnp.swapaxes(w_in, 1, 2)

    tm_big, tm_half = min(1024, t), min(512, t)
    tn = min(512, d)
    tf = 256

    h_mix = _rms_cast(xt, norm_mix_g[0][None, :], tm=tm_half)
    for i in range(depth):
        z = _in_proj(h_mix, w_in_t, i, tm=min(2048, t), tn=tn)
        z3 = z.reshape(b, s, col["total"])
        qk3 = _conv_silu(z3, conv_w[i], conv_b[i][None, :], col0=col["qk"], tc=256)
        gate_bias = jnp.concatenate(
            [b_igate[i].reshape(-1), b_fgate[i].reshape(-1), jnp.zeros((LANES - N_GATE_COLS,), F32)])[None, :]
        hf, hb = _mlstm_scan(qk3, z3, gate_bias, v_col0=col["mv"], gate_col0=col["gates"])
        q3, k3, vt3 = _qk_prep(z3, cos_t, sin_t, q_norm_g[i][None, :], k_norm_g[i][None, :],
                               aq_col0=col["aq"], ak_col0=col["ak"], av_col0=col["av"], ts=min(512, s))
        ha = _flash_gqa(q3, k3, vt3, _score_bound(q_norm_g[i], k_norm_g[i]), tq=min(256, s), tk=min(512, s))
        hm = _mlstm_out(hf.reshape(t, MLSTM_WIDTH), hb.reshape(t, MLSTM_WIDTH), z,
                        mlstm_norm_g[i].reshape(1, MLSTM_WIDTH), mo_col0=col["mo"], tm=tm_half)
        y = _merge(hm, ha.reshape(t, ATTN_WIDTH), z, w_mlstm_up, w_attn_up, i,
                   gm_col0=col["gm"], ga_col0=col["ga"], tm=tm_big, tn=tn)
        xt = _out_proj(y, w_out, i, xt, tm=tm_half)

        j = i // 2
        g_ffn = norm_ffn_g[i][None, :]
        if i % 2 == 0:
            xt = _ffn(xt, g_ffn, ffn_w_gate, ffn_w_up, ffn_w_down, j, tm=tm_big, tf=tf)
        else:
            xt = _moe(xt, g_ffn, moe_router[j], moe_router_b[j], moe_w_gate, moe_w_up, moe_w_down, j,
                      tb_route=tm_half, tf=tf)

        g_next = norm_mix_g[min(i + 1, depth - 1)][None, :]
        xt, h_mix = _ple(xt, norm_ple_g[i][None, :], w_ple_gate, p_all, w_ple_proj, i, g_next, tm=tm_half)

    return xt.reshape(b, s, d)
```

```python
import functools
import itertools

import jax
import jax.numpy as jnp
from jax import lax
from jax.experimental import pallas as pl
from jax.experimental.pallas import tpu as pltpu

F32 = jnp.float32
BF16 = jnp.bfloat16

GRID_W = 64
N_MLSTM_HEADS = 4
MLSTM_QK_DIM = 128
MLSTM_V_DIM = 256
MLSTM_QK_WIDTH = N_MLSTM_HEADS * MLSTM_QK_DIM
MLSTM_WIDTH = N_MLSTM_HEADS * MLSTM_V_DIM
MLSTM_CHUNK = 256
N_DIRS = 2
N_ATTN_HEADS = 8
N_KV_HEADS = 2
ATTN_HEAD_DIM = 128
ATTN_WIDTH = N_ATTN_HEADS * ATTN_HEAD_DIM
KV_WIDTH = N_KV_HEADS * ATTN_HEAD_DIM
ROPE_THETA = 10000.0
N_EXPERTS = 8
EPS = 1e-6

LANES = 128
VMEM_LIMIT_BYTES = 56 * 2**20

N_GATE_COLS = 2 * N_DIRS * N_MLSTM_HEADS


def _col_layout(d_model):
    names = ("qk", "mv", "mo", "aq", "ak", "av", "gm", "ga", "gates")
    widths = (2 * MLSTM_QK_WIDTH, MLSTM_WIDTH, MLSTM_WIDTH, ATTN_WIDTH, KV_WIDTH, KV_WIDTH, d_model, d_model, LANES)
    off, out = 0, {}
    for n, w in zip(names, widths):
        out[n] = off
        off += w
    out["total"] = off
    return out


N_HEAD_COLS = 2 * MLSTM_QK_WIDTH + MLSTM_WIDTH


def _cparams(semantics):
    return pltpu.CompilerParams(dimension_semantics=semantics, vmem_limit_bytes=VMEM_LIMIT_BYTES)


def _rms(x, g):
    ms = jnp.mean(x * x, axis=-1, keepdims=True)
    return x * lax.rsqrt(ms + EPS) * g


def _dot(a, b):
    return jnp.dot(a, b, preferred_element_type=F32)


def _dot_nt(a, b):
    return lax.dot_general(a, b, (((1,), (1,)), ((), ())), preferred_element_type=F32)


def _dot_tn(a, b):
    return lax.dot_general(a, b, (((0,), (0,)), ((), ())), preferred_element_type=F32)


def _rms_cast_kernel(x_ref, g_ref, h_ref):
    h_ref[...] = _rms(x_ref[...], g_ref[...]).astype(BF16)


def _rms_cast(x, g, *, tm):
    t, d = x.shape
    row_spec = pl.BlockSpec((tm, d), lambda m: (m, 0))
    return pl.pallas_call(
        _rms_cast_kernel,
        grid=(t // tm,),
        in_specs=[row_spec, pl.BlockSpec((1, d), lambda m: (0, 0))],
        out_specs=row_spec,
        out_shape=jax.ShapeDtypeStruct((t, d), BF16),
        compiler_params=_cparams(("parallel",)),
        name="input_rms_norm",
    )(x, g)


def _in_proj_kernel(h_ref, wm_ref, wn_ref, wg_ref, o_ref, *, head_blocks, tail_blocks):
    j = pl.program_id(1)
    ng = N_GATE_COLS

    @pl.when(j < head_blocks)
    def _():
        o_ref[...] = _dot_nt(h_ref[...], wm_ref[...].astype(BF16))

    @pl.when((j >= head_blocks) & (j < head_blocks + tail_blocks))
    def _():
        w = jnp.concatenate([wm_ref[ng:, :], wn_ref[...]], axis=0).astype(BF16)
        o_ref[...] = _dot_nt(h_ref[...], w)

    @pl.when(j == head_blocks + tail_blocks)
    def _():
        pad = jnp.zeros((wm_ref.shape[0] - ng, wm_ref.shape[1]), F32)
        w = jnp.concatenate([wg_ref[...], pad], axis=0).astype(BF16)
        o_ref[...] = _dot_nt(h_ref[...], w)


def _in_proj(h, wt_all, layer, *, tm, tn):
    t, d = h.shape
    n_cols = wt_all.shape[1]
    ng = N_GATE_COLS
    tail = n_cols - N_HEAD_COLS - ng
    head_blocks, tail_blocks = N_HEAD_COLS // tn, tail // tn
    assert tail % tn == 0 and N_HEAD_COLS % tn == 0 and n_cols % ng == 0
    last_main = head_blocks + tail_blocks - 1
    return pl.pallas_call(
        functools.partial(_in_proj_kernel, head_blocks=head_blocks, tail_blocks=tail_blocks),
        grid=(t // tm, head_blocks + tail_blocks + 1),
        in_specs=[
            pl.BlockSpec((tm, d), lambda m, j: (m, 0)),
            pl.BlockSpec((None, tn, d), lambda m, j: (layer, jnp.minimum(j, last_main), 0)),
            pl.BlockSpec((None, ng, d), lambda m, j: (layer, jnp.minimum((j + 1) * (tn // ng), n_cols // ng - 1), 0)),
            pl.BlockSpec((None, ng, d), lambda m, j: (layer, N_HEAD_COLS // ng, 0)),
        ],
        out_specs=pl.BlockSpec((tm, tn), lambda m, j: (m, j)),
        out_shape=jax.ShapeDtypeStruct((t, N_HEAD_COLS + tail + LANES), F32),
        compiler_params=_cparams(("parallel", "parallel")),
        name="in_proj",
    )(h, wt_all, wt_all, wt_all)


def _conv_silu_kernel(z_ref, w_ref, b_ref, o_ref, *, q_blocks, q_scale):
    x = z_ref[0]
    s = x.shape[0]
    row = lax.broadcasted_iota(jnp.int32, x.shape, 0)
    prev = jnp.where(row == 0, 0.0, pltpu.roll(x, 1, 0))
    nxt = jnp.where(row == s - 1, 0.0, pltpu.roll(x, s - 1, 0))
    y = prev * w_ref[0:1, :] + x * w_ref[1:2, :] + nxt * w_ref[2:3, :] + b_ref[...]
    y = y * jax.nn.sigmoid(y)
    scale = jnp.where(pl.program_id(1) < q_blocks, q_scale, 1.0).astype(F32)
    o_ref[0] = (y * scale).astype(o_ref.dtype)


def _conv_silu(z3, conv_w, conv_b, *, col0, tc):
    b, s, _ = z3.shape
    width = conv_w.shape[1]
    kern = functools.partial(_conv_silu_kernel, q_blocks=MLSTM_QK_WIDTH // tc, q_scale=MLSTM_QK_DIM ** -0.5)
    return pl.pallas_call(
        kern,
        grid=(b, width // tc),
        in_specs=[
            pl.BlockSpec((1, s, tc), lambda i, j: (i, 0, col0 // tc + j)),
            pl.BlockSpec((3, tc), lambda i, j: (0, j)),
            pl.BlockSpec((1, tc), lambda i, j: (0, j)),
        ],
        out_specs=pl.BlockSpec((1, s, tc), lambda i, j: (i, 0, j)),
        out_shape=jax.ShapeDtypeStruct((b, s, width), BF16),
        compiler_params=_cparams(("parallel", "parallel")),
        name="mlstm_conv_silu",
    )(z3, conv_w, conv_b)


def _log_sigmoid(x):
    return jnp.minimum(x, 0.0) - jnp.log1p(jnp.exp(-jnp.abs(x)))


def _lane_col(x, idx):
    lane = lax.broadcasted_iota(jnp.int32, x.shape, 1)
    return jnp.sum(jnp.where(lane == idx, x, 0.0), axis=-1, keepdims=True)


def _mlstm_kernel(qkf_ref, qkb_ref, vf_ref, vb_ref, gf_ref, gb_ref, bias_ref, hf_ref, hb_ref, c_scr, m_scr):
    L = MLSTM_CHUNK
    dk, dv, nh = MLSTM_QK_DIM, MLSTM_V_DIM, N_MLSTM_HEADS

    @pl.when(pl.program_id(0) == 0)
    def _():
        c_scr[...] = jnp.zeros_like(c_scr)
        m_scr[...] = jnp.zeros_like(m_scr)

    r_i = lax.broadcasted_iota(jnp.int32, (L, L), 0)
    c_i = lax.broadcasted_iota(jnp.int32, (L, L), 1)
    lane = lax.broadcasted_iota(jnp.int32, (L, LANES), 1)
    ones_blk = jnp.where(lane == 0, 1.0, 0.0).astype(F32)

    dirs = ((qkf_ref, vf_ref, gf_ref, hf_ref), (qkb_ref, vb_ref, gb_ref, hb_ref))
    for bi, (d, (qk_ref, v_ref, g_ref, h_ref)) in itertools.product(range(qkf_ref.shape[0]), enumerate(dirs)):
        mask = (c_i <= r_i) if d == 0 else (c_i >= r_i)
        g = g_ref[bi] + bias_ref[...]
        gp = jnp.where(lane < N_GATE_COLS // 2, g, _log_sigmoid(g))
        gp_t = gp.T
        bc = jnp.dot(mask.astype(F32), gp, precision=lax.Precision.HIGHEST, preferred_element_type=F32)
        bc_t = bc.T
        end_row = L - 1 if d == 0 else 0
        for h in range(nh):
            ci = (bi * N_DIRS + d) * nh + h
            ch_i, ch_f = d * nh + h, N_GATE_COLS // 2 + d * nh + h
            i_row = gp_t[ch_i:ch_i + 1, :]
            bc_row = bc_t[ch_f:ch_f + 1, :]
            i_col = _lane_col(gp, ch_i)
            bc_col = _lane_col(bc, ch_f)
            m_prev = m_scr[ci, 0:1, 0:1]
            c_prev = c_scr[ci]

            a_col = bc_col + m_prev
            dm = jnp.where(mask, bc_col - bc_row + i_row, -jnp.inf)
            m_t = jnp.maximum(a_col, jnp.max(dm, axis=-1, keepdims=True))
            w_inter = jnp.exp(a_col - m_t)
            w_intra = jnp.exp(dm - m_t)

            q = qk_ref[bi, :, h * dk:(h + 1) * dk]
            k = qk_ref[bi, :, MLSTM_QK_WIDTH + h * dk:MLSTM_QK_WIDTH + (h + 1) * dk]
            v_ext = jnp.concatenate([v_ref[bi, :, h * dv:(h + 1) * dv], ones_blk], axis=1)
            p = (_dot_nt(q, k) * w_intra).astype(BF16)
            num_ext = w_inter * _dot(q, c_prev.astype(BF16)) + _dot(p, v_ext.astype(BF16))
            den = num_ext[:, dv:dv + 1]
            h_ref[bi, :, h * dv:(h + 1) * dv] = num_ext[:, :dv] / jnp.maximum(jnp.abs(den), jnp.exp(-m_t))

            b_end = bc_col[end_row:end_row + 1, :]
            g_col = b_end - bc_col + i_col
            m_new = jnp.maximum(b_end + m_prev, jnp.max(g_col, axis=0, keepdims=True))
            decay = jnp.exp(b_end + m_prev - m_new)
            ws = jnp.exp(g_col - m_new)
            c_scr[ci] = decay * c_prev + _dot_tn(k, (ws * v_ext).astype(BF16))
            m_scr[ci] = jnp.broadcast_to(m_new, m_scr.shape[1:])


def _mlstm_scan(qk3, z3, gate_bias, *, v_col0, gate_col0):
    b, s, _ = qk3.shape
    L = MLSTM_CHUNK
    nc = s // L
    n_chain = b * N_DIRS * N_MLSTM_HEADS
    vb, gb = v_col0 // MLSTM_WIDTH, gate_col0 // LANES
    qk_w = 2 * MLSTM_QK_WIDTH
    return pl.pallas_call(
        _mlstm_kernel,
        grid=(nc,),
        in_specs=[
            pl.BlockSpec((b, L, qk_w), lambda c: (0, c, 0)),
            pl.BlockSpec((b, L, qk_w), lambda c: (0, nc - 1 - c, 0)),
            pl.BlockSpec((b, L, MLSTM_WIDTH), lambda c: (0, c, vb)),
            pl.BlockSpec((b, L, MLSTM_WIDTH), lambda c: (0, nc - 1 - c, vb)),
            pl.BlockSpec((b, L, LANES), lambda c: (0, c, gb)),
            pl.BlockSpec((b, L, LANES), lambda c: (0, nc - 1 - c, gb)),
            pl.BlockSpec((1, LANES), lambda c: (0, 0)),
        ],
        out_specs=[
            pl.BlockSpec((b, L, MLSTM_WIDTH), lambda c: (0, c, 0)),
            pl.BlockSpec((b, L, MLSTM_WIDTH), lambda c: (0, nc - 1 - c, 0)),
        ],
        out_shape=[jax.ShapeDtypeStruct((b, s, MLSTM_WIDTH), F32)] * 2,
        scratch_shapes=[
            pltpu.VMEM((n_chain, MLSTM_QK_DIM, MLSTM_V_DIM + LANES), F32),
            pltpu.VMEM((n_chain, 8, LANES), F32),
        ],
        compiler_params=_cparams(("arbitrary",)),
        name="mlstm_scan",
    )(qk3, qk3, z3, z3, z3, z3, gate_bias)


LOG2_E = 1.4426950408889634


def _qk_prep_kernel(aq_ref, ak_ref, av_ref, cos_ref, sin_ref, gq_ref, gk_ref, q_ref, k_ref, vt_ref):
    cos, sin = cos_ref[...], sin_ref[...]
    hd = ATTN_HEAD_DIM
    lane = lax.broadcasted_iota(jnp.int32, cos.shape, 1)
    first_half = (lane % (hd // 2)) < (hd // 4)

    def prep(x, g, scale):
        n = _rms(x, g)
        partner = jnp.where(first_half, pltpu.roll(n, hd - hd // 4, 1), pltpu.roll(n, hd // 4, 1))
        return ((n * cos + partner * sin) * scale).astype(BF16)

    for h in range(N_ATTN_HEADS):
        q_ref[0, :, h * hd:(h + 1) * hd] = prep(aq_ref[0, :, h * hd:(h + 1) * hd], gq_ref[...], hd ** -0.5 * LOG2_E)
    for h in range(N_KV_HEADS):
        k_ref[0, :, h * hd:(h + 1) * hd] = prep(ak_ref[0, :, h * hd:(h + 1) * hd], gk_ref[...], 1.0)
    vt_ref[0] = av_ref[0].T.astype(BF16)


def _qk_prep(z3, cos_t, sin_t, gq, gk, *, aq_col0, ak_col0, av_col0, ts):
    b, s, _ = z3.shape
    return pl.pallas_call(
        _qk_prep_kernel,
        grid=(b, s // ts),
        in_specs=[
            pl.BlockSpec((1, ts, ATTN_WIDTH), lambda i, j: (i, j, aq_col0 // ATTN_WIDTH)),
            pl.BlockSpec((1, ts, KV_WIDTH), lambda i, j: (i, j, ak_col0 // KV_WIDTH)),
            pl.BlockSpec((1, ts, KV_WIDTH), lambda i, j: (i, j, av_col0 // KV_WIDTH)),
            pl.BlockSpec((ts, ATTN_HEAD_DIM), lambda i, j: (j, 0)),
            pl.BlockSpec((ts, ATTN_HEAD_DIM), lambda i, j: (j, 0)),
            pl.BlockSpec((1, ATTN_HEAD_DIM), lambda i, j: (0, 0)),
            pl.BlockSpec((1, ATTN_HEAD_DIM), lambda i, j: (0, 0)),
        ],
        out_specs=[
            pl.BlockSpec((1, ts, ATTN_WIDTH), lambda i, j: (i, j, 0)),
            pl.BlockSpec((1, ts, KV_WIDTH), lambda i, j: (i, j, 0)),
            pl.BlockSpec((1, KV_WIDTH, ts), lambda i, j: (i, 0, j)),
        ],
        out_shape=[jax.ShapeDtypeStruct((b, s, ATTN_WIDTH), BF16), jax.ShapeDtypeStruct((b, s, KV_WIDTH), BF16),
                   jax.ShapeDtypeStruct((b, KV_WIDTH, s), BF16)],
        compiler_params=_cparams(("parallel", "parallel")),
        name="attn_qk_norm_rope",
    )(z3, z3, z3, cos_t, sin_t, gq, gk)


SOFTMAX_SHIFT_LIMIT = 60.0


def _flash_kernel(bound_ref, q_ref, k_ref, vt_ref, o_ref, *, grp, tk):
    hd = ATTN_HEAD_DIM
    tq = q_ref.shape[1]
    s_len = k_ref.shape[1]
    rows = grp * tq
    bound = bound_ref[0, 0]
    q = jnp.concatenate([q_ref[0, :, g * hd:(g + 1) * hd] for g in range(grp)], axis=0)
    chunks = [slice(c * tk, (c + 1) * tk) for c in range(s_len // tk)]

    def store(acc, l):
        o = (acc / l).T
        for g in range(grp):
            o_ref[0, :, g * hd:(g + 1) * hd] = o[g * tq:(g + 1) * tq, :].astype(o_ref.dtype)

    @pl.when(bound <= SOFTMAX_SHIFT_LIMIT)
    def _():
        l = jnp.zeros((1, rows), F32)
        acc = jnp.zeros((hd, rows), F32)
        for ck in chunks:
            p = jnp.exp2(_dot_nt(k_ref[0, ck, :], q) - bound)
            l = l + jnp.sum(p, axis=0, keepdims=True)
            acc = acc + _dot(vt_ref[0, :, ck], p.astype(BF16))
        store(acc, l)

    @pl.when(bound > SOFTMAX_SHIFT_LIMIT)
    def _():
        m = jnp.full((1, rows), -jnp.inf, F32)
        l = jnp.zeros((1, rows), F32)
        acc = jnp.zeros((hd, rows), F32)
        for ck in chunks:
            st = _dot_nt(k_ref[0, ck, :], q)
            m_new = jnp.maximum(m, jnp.max(st, axis=0, keepdims=True))
            alpha = jnp.exp2(m - m_new)
            p = jnp.exp2(st - m_new)
            l = alpha * l + jnp.sum(p, axis=0, keepdims=True)
            acc = alpha * acc + _dot(vt_ref[0, :, ck], p.astype(BF16))
            m = m_new
        store(acc, l)


def _score_bound(gq, gk):
    scale = ATTN_HEAD_DIM ** -0.5 * LOG2_E
    return (1.02 * ATTN_HEAD_DIM * scale * jnp.max(jnp.abs(gq)) * jnp.max(jnp.abs(gk))).reshape(1, 1).astype(F32)


def _flash_gqa(q3, k3, vt3, bound, *, tq, tk):
    b, s, _ = q3.shape
    hd = ATTN_HEAD_DIM
    grp = N_ATTN_HEADS // N_KV_HEADS
    return pl.pallas_call(
        functools.partial(_flash_kernel, grp=grp, tk=tk),
        grid=(b, N_KV_HEADS, s // tq),
        in_specs=[
            pl.BlockSpec(memory_space=pltpu.SMEM),
            pl.BlockSpec((1, tq, grp * hd), lambda i, h, qi: (i, qi, h)),
            pl.BlockSpec((1, s, hd), lambda i, h, qi: (i, 0, h)),
            pl.BlockSpec((1, hd, s), lambda i, h, qi: (i, h, 0)),
        ],
        out_specs=pl.BlockSpec((1, tq, grp * hd), lambda i, h, qi: (i, qi, h)),
        out_shape=jax.ShapeDtypeStruct((b, s, ATTN_WIDTH), BF16),
        compiler_params=_cparams(("parallel", "parallel", "parallel")),
        name="gqa_flash",
    )(bound, q3, k3, vt3)


def _mlstm_out_kernel(hf_ref, hb_ref, mo_ref, ng_ref, hm_ref):
    dv = MLSTM_V_DIM
    for h in range(N_MLSTM_HEADS):
        sl = slice(h * dv, (h + 1) * dv)
        hm = _rms(hf_ref[:, sl] + hb_ref[:, sl], ng_ref[:, sl])
        hm_ref[:, sl] = (hm * jax.nn.sigmoid(mo_ref[:, sl])).astype(BF16)


def _mlstm_out(hf, hb, z, ng, *, mo_col0, tm):
    t, w = hf.shape
    row_spec = pl.BlockSpec((tm, w), lambda m: (m, 0))
    return pl.pallas_call(
        _mlstm_out_kernel,
        grid=(t // tm,),
        in_specs=[row_spec, row_spec, pl.BlockSpec((tm, w), lambda m: (m, mo_col0 // w)),
                  pl.BlockSpec((1, w), lambda m: (0, 0))],
        out_specs=row_spec,
        out_shape=jax.ShapeDtypeStruct((t, w), BF16),
        compiler_params=_cparams(("parallel",)),
        name="mlstm_out_norm",
    )(hf, hb, z, ng)


def _merge_kernel(hm_ref, ha_ref, gm_ref, ga_ref, wm_ref, wa_ref, y_ref, wmb_scr, wab_scr):
    @pl.when(pl.program_id(1) == 0)
    def _():
        wmb_scr[...] = wm_ref[...].astype(BF16)
        wab_scr[...] = wa_ref[...].astype(BF16)

    ym = _dot(hm_ref[...], wmb_scr[...])
    ya = _dot(ha_ref[...], wab_scr[...])
    y_ref[...] = (jax.nn.sigmoid(gm_ref[...]) * ym + jax.nn.sigmoid(ga_ref[...]) * ya).astype(y_ref.dtype)


def _merge(hm, ha, z, wm_all, wa_all, layer, *, gm_col0, ga_col0, tm, tn):
    t, w = hm.shape
    d = wm_all.shape[2]
    act_spec = pl.BlockSpec((tm, w), lambda j, m: (m, 0))
    w_spec = pl.BlockSpec((None, w, tn), lambda j, m: (layer, 0, j))
    return pl.pallas_call(
        _merge_kernel,
        grid=(d // tn, t // tm),
        in_specs=[
            act_spec,
            act_spec,
            pl.BlockSpec((tm, tn), lambda j, m: (m, gm_col0 // tn + j)),
            pl.BlockSpec((tm, tn), lambda j, m: (m, ga_col0 // tn + j)),
            w_spec,
            w_spec,
        ],
        out_specs=pl.BlockSpec((tm, tn), lambda j, m: (m, j)),
        out_shape=jax.ShapeDtypeStruct((t, d), BF16),
        scratch_shapes=[pltpu.VMEM((w, tn), BF16), pltpu.VMEM((w, tn), BF16)],
        compiler_params=_cparams(("parallel", "arbitrary")),
        name="branch_merge",
    )(hm, ha, z, z, wm_all, wa_all)


def _resident_weight_spec(shape, index):
    return pl.BlockSpec(shape, lambda *_: index, pipeline_mode=pl.Buffered(1))


def _out_proj_kernel(y_ref, w_ref, x_ref, o_ref, wb_scr):
    @pl.when(pl.program_id(0) == 0)
    def _():
        wb_scr[...] = w_ref[...].astype(BF16)

    o_ref[...] = x_ref[...] + _dot(y_ref[...], wb_scr[...])


def _out_proj(y, w_all, layer, x, *, tm):
    t, d = x.shape
    k = y.shape[1]
    return pl.pallas_call(
        _out_proj_kernel,
        grid=(t // tm,),
        in_specs=[
            pl.BlockSpec((tm, k), lambda m: (m, 0)),
            _resident_weight_spec((None, k, d), (layer, 0, 0)),
            pl.BlockSpec((tm, d), lambda m: (m, 0)),
        ],
        out_specs=pl.BlockSpec((tm, d), lambda m: (m, 0)),
        out_shape=jax.ShapeDtypeStruct((t, d), F32),
        scratch_shapes=[pltpu.VMEM((k, d), BF16)],
        compiler_params=_cparams(("arbitrary",)),
        name="out_proj_residual",
    )(y, w_all, x)


def _swiglu_partial(h, wg_ref, wu_ref, wd_ref):
    a = _dot(h, wg_ref[...].astype(BF16))
    u = _dot(h, wu_ref[...].astype(BF16))
    t = a * jax.nn.sigmoid(a) * u
    return _dot(t.astype(BF16), wd_ref[...].astype(BF16))


def _ffn_kernel(x_ref, g_ref, wg_ref, wu_ref, wd_ref, o_ref, h_scr):
    @pl.when(pl.program_id(1) == 0)
    def _():
        x = x_ref[...]
        h_scr[...] = _rms(x, g_ref[...]).astype(BF16)
        o_ref[...] = x

    o_ref[...] += _swiglu_partial(h_scr[...], wg_ref, wu_ref, wd_ref)


def _ffn(x, g, wg_all, wu_all, wd_all, j, *, tm, tf):
    t, d = x.shape
    ff = wg_all.shape[-1]
    w_in_spec = pl.BlockSpec((None, d, tf), lambda m, f: (j, 0, f))
    return pl.pallas_call(
        _ffn_kernel,
        grid=(t // tm, ff // tf),
        in_specs=[
            pl.BlockSpec((tm, d), lambda m, f: (m, 0)),
            pl.BlockSpec((1, d), lambda m, f: (0, 0)),
            w_in_spec,
            w_in_spec,
            pl.BlockSpec((None, tf, d), lambda m, f: (j, f, 0)),
        ],
        out_specs=pl.BlockSpec((tm, d), lambda m, f: (m, 0)),
        out_shape=jax.ShapeDtypeStruct((t, d), F32),
        scratch_shapes=[pltpu.VMEM((tm, d), BF16)],
        compiler_params=_cparams(("parallel", "arbitrary")),
        name="dense_swiglu",
    )(x, g, wg_all, wu_all, wd_all)


MOE_ROW_TILE = 2560
MOE_SUB_TILE = 512
META_E1, META_E2, META_P1, META_P2, META_R1, META_R2 = range(6)


def _router_kernel(x_ref, g_ref, w_ref, b_ref, meta_ref, cnt_ref, carry_scr):
    @pl.when(pl.program_id(0) == 0)
    def _():
        carry_scr[...] = jnp.zeros_like(carry_scr)

    h = _rms(x_ref[...], g_ref[...])
    logits = jnp.dot(h, w_ref[...], precision=lax.Precision.HIGHEST, preferred_element_type=F32) + b_ref[...]
    tb = logits.shape[0]
    lane = lax.broadcasted_iota(jnp.int32, logits.shape, 1)
    logits = jnp.where(lane < N_EXPERTS, logits, -jnp.inf)
    v1 = jnp.max(logits, axis=-1, keepdims=True)
    i1 = jnp.min(jnp.where(logits == v1, lane, LANES), axis=-1, keepdims=True)
    rest = jnp.where(lane == i1, -jnp.inf, logits)
    v2 = jnp.max(rest, axis=-1, keepdims=True)
    i2 = jnp.min(jnp.where(rest == v2, lane, LANES), axis=-1, keepdims=True)
    e2 = jnp.exp(v2 - v1)
    p1 = 1.0 / (1.0 + e2)
    p2 = e2 / (1.0 + e2)
    sel = ((lane == i1) | (lane == i2)).astype(F32)
    r_i = lax.broadcasted_iota(jnp.int32, (tb, tb), 0)
    c_i = lax.broadcasted_iota(jnp.int32, (tb, tb), 1)
    earlier = (c_i < r_i).astype(BF16)
    rank = carry_scr[0:1, :] + _dot(earlier, sel.astype(BF16))
    r1 = jnp.sum(jnp.where(lane == i1, rank, 0.0), axis=-1, keepdims=True)
    r2 = jnp.sum(jnp.where(lane == i2, rank, 0.0), axis=-1, keepdims=True)
    meta = jnp.zeros_like(logits)
    for idx, val in ((META_E1, i1.astype(F32)), (META_E2, i2.astype(F32)), (META_P1, p1), (META_P2, p2),
                     (META_R1, r1), (META_R2, r2)):
        meta = jnp.where(lane == idx, val, meta)
    meta_ref[...] = meta
    total = carry_scr[0:1, :] + jnp.sum(sel, axis=0, keepdims=True)
    carry_scr[...] = jnp.broadcast_to(total, carry_scr.shape)
    cnt_ref[...] = jnp.broadcast_to(total, cnt_ref.shape)


def _router(x, g, w_pad, b_pad, *, tm):
    t, d = x.shape
    return pl.pallas_call(
        _router_kernel,
        grid=(t // tm,),
        in_specs=[
            pl.BlockSpec((tm, d), lambda m: (m, 0)),
            pl.BlockSpec((1, d), lambda m: (0, 0)),
            pl.BlockSpec((d, LANES), lambda m: (0, 0)),
            pl.BlockSpec((1, LANES), lambda m: (0, 0)),
        ],
        out_specs=[pl.BlockSpec((tm, LANES), lambda m: (m, 0)), pl.BlockSpec((8, LANES), lambda m: (0, 0))],
        out_shape=[jax.ShapeDtypeStruct((t, LANES), F32), jax.ShapeDtypeStruct((8, LANES), F32)],
        scratch_shapes=[pltpu.VMEM((8, LANES), F32)],
        compiler_params=_cparams(("arbitrary",)),
        name="moe_router",
    )(x, g, w_pad, b_pad)


def _moe_plan(meta, cnt, t):
    tile, sub = MOE_ROW_TILE, MOE_SUB_TILE
    n_tiles = 2 * t // tile + N_EXPERTS
    e1 = meta[:, META_E1].astype(jnp.int32)
    e2 = meta[:, META_E2].astype(jnp.int32)
    r1 = meta[:, META_R1].astype(jnp.int32)
    r2 = meta[:, META_R2].astype(jnp.int32)
    counts = cnt[0, :N_EXPERTS].astype(jnp.int32)
    tiles_e = (counts + tile - 1) // tile
    tile_end = jnp.cumsum(tiles_e)
    tile_start = tile_end - tiles_e
    row_start = tile_start * tile
    dest = jnp.concatenate([row_start[e1] + r1, row_start[e2] + r2]).astype(jnp.int32)
    m = jnp.arange(n_tiles, dtype=jnp.int32)
    live = m < tile_end[-1]
    m_eff = jnp.where(live, m, jnp.maximum(tile_end[-1] - 1, 0))
    tile_expert = jnp.minimum(jnp.searchsorted(tile_end, m_eff, side="right"), N_EXPERTS - 1).astype(jnp.int32)
    rows_left = counts[tile_expert] - (m_eff - tile_start[tile_expert]) * tile
    n_sub = jnp.where(live, jnp.clip((rows_left + sub - 1) // sub, 0, tile // sub), 0).astype(jnp.int32)
    sub_base = (jnp.cumsum(n_sub) - n_sub).astype(jnp.int32)
    tok = jnp.arange(t, dtype=jnp.int32)
    src_pos = sub_base[dest // tile] * sub + dest % tile
    n_src = (2 * t // sub + N_EXPERTS) * sub
    src = jnp.zeros((n_src,), jnp.int32).at[src_pos].set(jnp.concatenate([tok, tok]))
    return dest, src, tile_expert, n_sub, sub_base, n_tiles


def _expert_kernel(te_ref, ns_ref, sbase_ref, src_ref, x_hbm, g_ref, wg_ref, wu_ref, wd_ref, ys_hbm,
                   xbuf, h_scr, acc, gsem, zsem, osem):
    del te_ref
    m, f = pl.program_id(0), pl.program_id(1)
    n_m, n_f = pl.num_programs(0), pl.num_programs(1)
    tile, sub = MOE_ROW_TILE, MOE_SUB_TILE
    n_live = ns_ref[m]
    subs = [(sb, slice(sb * sub, (sb + 1) * sub)) for sb in range(tile // sub)]

    def gather_start(sb, slot):
        def start(r, carry):
            tok = src_ref[(sbase_ref[m] + sb) * sub + r]
            pltpu.make_async_copy(x_hbm.at[pl.ds(tok, 1)], xbuf.at[slot, pl.ds(r, 1)], gsem.at[slot]).start()
            return carry

        lax.fori_loop(0, sub, start, 0, unroll=8)

    def result_copy(sb, rows, tile_idx):
        return pltpu.make_async_copy(acc.at[rows], ys_hbm.at[pl.ds(tile_idx * tile + sb * sub, sub)], osem)

    def zero_copy(sb):
        return pltpu.make_async_copy(xbuf.at[0], ys_hbm.at[pl.ds(m * tile + sb * sub, sub)], zsem)

    @pl.when(f == 0)
    def _():
        @pl.when(n_live > 0)
        def _():
            gather_start(0, 0)

        for sb, rows in subs:
            slot = sb % 2
            if sb + 1 < len(subs):
                @pl.when(sb + 1 < n_live)
                def _():
                    gather_start(sb + 1, 1 - slot)

            @pl.when(sb < n_live)
            def _():
                pltpu.make_async_copy(x_hbm.at[pl.ds(0, sub)], xbuf.at[slot], gsem.at[slot]).wait()
                h_scr[rows, :] = _rms(xbuf[slot], g_ref[...]).astype(BF16)

        @pl.when(m > 0)
        def _():
            for sb, rows in subs:
                @pl.when(sb < ns_ref[m - 1])
                def _():
                    result_copy(sb, rows, m - 1).wait()

        for sb, rows in subs:
            @pl.when(sb < n_live)
            def _():
                acc[rows, :] = jnp.zeros((sub, acc.shape[1]), F32)

    @pl.when(f == 1)
    def _():
        xbuf[0] = jnp.zeros(xbuf.shape[1:], F32)
        for sb, rows in subs:
            @pl.when(sb >= n_live)
            def _():
                zero_copy(sb).start()

    for sb, rows in subs:
        @pl.when(sb < n_live)
        def _():
            acc[rows, :] += _swiglu_partial(h_scr[rows, :], wg_ref, wu_ref, wd_ref)

    @pl.when(f == n_f - 1)
    def _():
        for sb, rows in subs:
            @pl.when(sb < n_live)
            def _():
                result_copy(sb, rows, m).start()

            @pl.when(sb >= n_live)
            def _():
                zero_copy(sb).wait()

        @pl.when(m == n_m - 1)
        def _():
            for sb, rows in subs:
                @pl.when(sb < n_live)
                def _():
                    result_copy(sb, rows, m).wait()


def _experts(x, g, wg_all, wu_all, wd_all, j, tile_expert, n_sub, sub_base, src, n_tiles, *, tf):
    t, d = x.shape
    tile = MOE_ROW_TILE
    ff = wg_all.shape[-1]
    n_f = ff // tf

    def f_eff(m, f, ns):
        return jnp.where(ns[m] > 0, f, n_f - 1)

    w_in_spec = pl.BlockSpec((None, None, d, tf), lambda m, f, te, ns, sbase, sr: (j, te[m], 0, f_eff(m, f, ns)))
    w_dn_spec = pl.BlockSpec((None, None, tf, d), lambda m, f, te, ns, sbase, sr: (j, te[m], f_eff(m, f, ns), 0))
    return pl.pallas_call(
        _expert_kernel,
        grid_spec=pltpu.PrefetchScalarGridSpec(
            num_scalar_prefetch=4,
            grid=(n_tiles, n_f),
            in_specs=[
                pl.BlockSpec(memory_space=pl.ANY),
                pl.BlockSpec((1, d), lambda m, f, te, ns, sbase, sr: (0, 0)),
                w_in_spec,
                w_in_spec,
                w_dn_spec,
            ],
            out_specs=pl.BlockSpec(memory_space=pl.ANY),
            scratch_shapes=[
                pltpu.VMEM((2, MOE_SUB_TILE, d), F32),
                pltpu.VMEM((tile, d), BF16),
                pltpu.VMEM((tile, d), F32),
                pltpu.SemaphoreType.DMA((2,)),
                pltpu.SemaphoreType.DMA(()),
                pltpu.SemaphoreType.DMA(()),
            ],
        ),
        out_shape=jax.ShapeDtypeStruct((n_tiles * tile, d), F32),
        compiler_params=_cparams(("arbitrary", "arbitrary")),
        name="moe_experts",
    )(tile_expert, n_sub, sub_base, src, x, g, wg_all, wu_all, wd_all)


def _combine_kernel(dest_ref, x_ref, meta_ref, ys_hbm, o_ref, buf, sem, *, tb, t):
    i, n = pl.program_id(0), pl.num_programs(0)

    def gather_start(tile_idx, slot):
        def start(r, carry):
            for k in range(2):
                row = dest_ref[k * t + tile_idx * tb + r]
                pltpu.make_async_copy(ys_hbm.at[pl.ds(row, 1)], buf.at[slot, k, pl.ds(r, 1)], sem.at[slot]).start()
            return carry

        lax.fori_loop(0, tb, start, 0, unroll=8)

    def combine(slot):
        for k in range(2):
            pltpu.make_async_copy(ys_hbm.at[pl.ds(0, tb)], buf.at[slot, k], sem.at[slot]).wait()
        meta = meta_ref[...]
        o_ref[...] = (x_ref[...] + _lane_col(meta, META_P1) * buf[slot, 0]
                      + _lane_col(meta, META_P2) * buf[slot, 1])

    @pl.when(i == 0)
    def _():
        gather_start(0, 0)

    for slot in range(2):
        @pl.when(i % 2 == slot)
        def _():
            @pl.when(i + 1 < n)
            def _():
                gather_start(i + 1, 1 - slot)

            combine(slot)


def _combine(x, meta, ys, dest, *, tb):
    t, d = x.shape
    return pl.pallas_call(
        functools.partial(_combine_kernel, tb=tb, t=t),
        grid_spec=pltpu.PrefetchScalarGridSpec(
            num_scalar_prefetch=1,
            grid=(t // tb,),
            in_specs=[
                pl.BlockSpec((tb, d), lambda i, dst: (i, 0)),
                pl.BlockSpec((tb, LANES), lambda i, dst: (i, 0)),
                pl.BlockSpec(memory_space=pl.ANY),
            ],
            out_specs=pl.BlockSpec((tb, d), lambda i, dst: (i, 0)),
            scratch_shapes=[pltpu.VMEM((2, 2, tb, d), F32), pltpu.SemaphoreType.DMA((2,))],
        ),
        out_shape=jax.ShapeDtypeStruct((t, d), F32),
        compiler_params=_cparams(("arbitrary",)),
        name="moe_combine",
    )(dest, x, meta, ys)


def _moe(x, g, router_w, router_b, wg_all, wu_all, wd_all, j, *, tb_route, tf):
    t, d = x.shape
    w_pad = jnp.pad(router_w, ((0, 0), (0, LANES - N_EXPERTS)))
    b_pad = jnp.pad(router_b, (0, LANES - N_EXPERTS))[None, :]
    meta, cnt = _router(x, g, w_pad, b_pad, tm=tb_route)
    dest, src, tile_expert, n_sub, sub_base, n_tiles = _moe_plan(meta, cnt, t)
    ys = _experts(x, g, wg_all, wu_all, wd_all, j, tile_expert, n_sub, sub_base, src, n_tiles, tf=tf)
    return _combine(x, meta, ys, dest, tb=min(256, t))


def _ple_kernel(x_ref, g_ref, wg_ref, p_ref, wp_ref, gn_ref, o_ref, hn_ref, wgb_scr, wpb_scr):
    @pl.when(pl.program_id(0) == 0)
    def _():
        wgb_scr[...] = wg_ref[...].astype(BF16)
        wpb_scr[...] = wp_ref[...].astype(BF16)

    x = x_ref[...]
    gate = jax.nn.sigmoid(_dot(_rms(x, g_ref[...]).astype(BF16), wgb_scr[...]))
    x_new = x + gate * _dot(p_ref[...].astype(BF16), wpb_scr[...])
    o_ref[...] = x_new
    hn_ref[...] = _rms(x_new, gn_ref[...]).astype(BF16)


def _ple(x, g, wg_all, p_all, wp_all, layer, g_next, *, tm):
    t, d = x.shape
    pd = p_all.shape[-1]
    row_spec = pl.BlockSpec((tm, d), lambda m: (m, 0))
    gain_spec = pl.BlockSpec((1, d), lambda m: (0, 0))
    return pl.pallas_call(
        _ple_kernel,
        grid=(t // tm,),
        in_specs=[
            row_spec,
            gain_spec,
            _resident_weight_spec((None, d, d), (layer, 0, 0)),
            pl.BlockSpec((None, tm, pd), lambda m: (layer, m, 0)),
            _resident_weight_spec((None, pd, d), (layer, 0, 0)),
            gain_spec,
        ],
        out_specs=[row_spec, row_spec],
        out_shape=[jax.ShapeDtypeStruct((t, d), F32), jax.ShapeDtypeStruct((t, d), BF16)],
        scratch_shapes=[pltpu.VMEM((d, d), BF16), pltpu.VMEM((pd, d), BF16)],
        compiler_params=_cparams(("arbitrary",)),
        name="ple_gate",
    )(x, g, wg_all, p_all, wp_all, g_next)


def _rope_tables(seq):
    rows = seq // GRID_W
    row = jnp.broadcast_to(jnp.arange(rows, dtype=F32)[:, None], (rows, GRID_W)).reshape(seq)
    col = jnp.broadcast_to(jnp.arange(GRID_W, dtype=F32)[None, :], (rows, GRID_W)).reshape(seq)
    axis_dim = ATTN_HEAD_DIM // 2
    inv_freq = ROPE_THETA ** (-jnp.arange(0, axis_dim, 2, dtype=F32) / axis_dim)
    ar, ac = row[:, None] * inv_freq, col[:, None] * inv_freq
    cos_t = jnp.concatenate([jnp.cos(ar), jnp.cos(ar), jnp.cos(ac), jnp.cos(ac)], axis=-1)
    sin_t = jnp.concatenate([-jnp.sin(ar), jnp.sin(ar), -jnp.sin(ac), jnp.sin(ac)], axis=-1)
    return cos_t, sin_t


def kernel(x, p, norm_mix_g, w_in, conv_w, conv_b, b_igate, b_fgate, mlstm_norm_g, q_norm_g, k_norm_g, w_mlstm_up, w_attn_up, w_out, norm_ffn_g, ffn_w_gate, ffn_w_up, ffn_w_down, moe_router, moe_router_b, moe_w_gate, moe_w_up, moe_w_down, norm_ple_g, w_ple_gate, w_ple_proj):
    b, s, d = x.shape
    depth = w_in.shape[0]
    t = b * s
    col = _col_layout(d)
    cos_t, sin_t = _rope_tables(s)
    xt = x.reshape(t, d)
    p_all = p.reshape(depth, t, p.shape[-1])
    w_in_t = jnp.swapaxes(w_in, 1, 2)

    tm_big, tm_half = min(1024, t), min(512, t)
    tn = min(512, d)
    tf = 256

    h_mix = _rms_cast(xt, norm_mix_g[0][None, :], tm=tm_half)
    for i in range(depth):
        z = _in_proj(h_mix, w_in_t, i, tm=min(2048, t), tn=tn)
        z3 = z.reshape(b, s, col["total"])
        qk3 = _conv_silu(z3, conv_w[i], conv_b[i][None, :], col0=col["qk"], tc=256)
        gate_bias = jnp.concatenate(
            [b_igate[i].reshape(-1), b_fgate[i].reshape(-1), jnp.zeros((LANES - N_GATE_COLS,), F32)])[None, :]
        hf, hb = _mlstm_scan(qk3, z3, gate_bias, v_col0=col["mv"], gate_col0=col["gates"])
        q3, k3, vt3 = _qk_prep(z3, cos_t, sin_t, q_norm_g[i][None, :], k_norm_g[i][None, :],
                               aq_col0=col["aq"], ak_col0=col["ak"], av_col0=col["av"], ts=min(512, s))
        ha = _flash_gqa(q3, k3, vt3, _score_bound(q_norm_g[i], k_norm_g[i]), tq=min(256, s), tk=min(2048, s))
        hm = _mlstm_out(hf.reshape(t, MLSTM_WIDTH), hb.reshape(t, MLSTM_WIDTH), z,
                        mlstm_norm_g[i].reshape(1, MLSTM_WIDTH), mo_col0=col["mo"], tm=tm_half)
        y = _merge(hm, ha.reshape(t, ATTN_WIDTH), z, w_mlstm_up, w_attn_up, i,
                   gm_col0=col["gm"], ga_col0=col["ga"], tm=tm_big, tn=tn)
        xt = _out_proj(y, w_out, i, xt, tm=tm_half)

        j = i // 2
        g_ffn = norm_ffn_g[i][None, :]
        if i % 2 == 0:
            xt = _ffn(xt, g_ffn, ffn_w_gate, ffn_w_up, ffn_w_down, j, tm=tm_big, tf=tf)
        else:
            xt = _moe(xt, g_ffn, moe_router[j], moe_router_b[j], moe_w_gate, moe_w_up, moe_w_down, j,
                      tb_route=tm_half, tf=tf)

        g_next = norm_mix_g[min(i + 1, depth - 1)][None, :]
        xt, h_mix = _ple(xt, norm_ple_g[i][None, :], w_ple_gate, p_all, w_ple_proj, i, g_next, tm=tm_half)

    return xt.reshape(b, s, d)
```

```python
import functools
import itertools

import jax
import jax.numpy as jnp
from jax import lax
from jax.experimental import pallas as pl
from jax.experimental.pallas import tpu as pltpu

F32 = jnp.float32
BF16 = jnp.bfloat16

GRID_W = 64
N_MLSTM_HEADS = 4
MLSTM_QK_DIM = 128
MLSTM_V_DIM = 256
MLSTM_QK_WIDTH = N_MLSTM_HEADS * MLSTM_QK_DIM
MLSTM_WIDTH = N_MLSTM_HEADS * MLSTM_V_DIM
MLSTM_CHUNK = 256
N_DIRS = 2
N_ATTN_HEADS = 8
N_KV_HEADS = 2
ATTN_HEAD_DIM = 128
ATTN_WIDTH = N_ATTN_HEADS * ATTN_HEAD_DIM
KV_WIDTH = N_KV_HEADS * ATTN_HEAD_DIM
ROPE_THETA = 10000.0
N_EXPERTS = 8
EPS = 1e-6

LANES = 128
VMEM_LIMIT_BYTES = 56 * 2**20

N_GATE_COLS = 2 * N_DIRS * N_MLSTM_HEADS


def _col_layout(d_model):
    names = ("qk", "mv", "mo", "aq", "gm", "ga", "ak", "av", "gates")
    widths = (2 * MLSTM_QK_WIDTH, MLSTM_WIDTH, MLSTM_WIDTH, ATTN_WIDTH, d_model, d_model, KV_WIDTH, KV_WIDTH, LANES)
    off, out = 0, {}
    for n, w in zip(names, widths):
        out[n] = off
        off += w
    out["total"] = off
    return out


N_HEAD_COLS = 2 * MLSTM_QK_WIDTH + MLSTM_WIDTH


def _cparams(semantics):
    return pltpu.CompilerParams(dimension_semantics=semantics, vmem_limit_bytes=VMEM_LIMIT_BYTES)


def _rms(x, g):
    ms = jnp.mean(x * x, axis=-1, keepdims=True)
    return x * lax.rsqrt(ms + EPS) * g


def _dot(a, b):
    return jnp.dot(a, b, preferred_element_type=F32)


def _dot_nt(a, b):
    return lax.dot_general(a, b, (((1,), (1,)), ((), ())), preferred_element_type=F32)


def _dot_tn(a, b):
    return lax.dot_general(a, b, (((0,), (0,)), ((), ())), preferred_element_type=F32)


def _rms_cast_kernel(x_ref, g_ref, h_ref):
    h_ref[...] = _rms(x_ref[...], g_ref[...]).astype(BF16)


def _rms_cast(x, g, *, tm):
    t, d = x.shape
    row_spec = pl.BlockSpec((tm, d), lambda m: (m, 0))
    return pl.pallas_call(
        _rms_cast_kernel,
        grid=(t // tm,),
        in_specs=[row_spec, pl.BlockSpec((1, d), lambda m: (0, 0))],
        out_specs=row_spec,
        out_shape=jax.ShapeDtypeStruct((t, d), BF16),
        compiler_params=_cparams(("parallel",)),
        name="input_rms_norm",
    )(x, g)


def _in_proj_kernel(h_ref, wm_ref, wn_ref, wg_ref, o_ref, *, head_blocks, tail_blocks):
    j = pl.program_id(1)
    ng = N_GATE_COLS

    @pl.when(j < head_blocks)
    def _():
        o_ref[...] = _dot_nt(h_ref[...], wm_ref[...].astype(BF16))

    @pl.when((j >= head_blocks) & (j < head_blocks + tail_blocks))
    def _():
        w = jnp.concatenate([wm_ref[ng:, :], wn_ref[...]], axis=0).astype(BF16)
        o_ref[...] = _dot_nt(h_ref[...], w)

    @pl.when(j == head_blocks + tail_blocks)
    def _():
        pad = jnp.zeros((wm_ref.shape[0] - ng, wm_ref.shape[1]), F32)
        w = jnp.concatenate([wg_ref[...], pad], axis=0).astype(BF16)
        o_ref[...] = _dot_nt(h_ref[...], w)


def _in_proj(h, wt_all, layer, *, tm, tn):
    t, d = h.shape
    n_cols = wt_all.shape[1]
    ng = N_GATE_COLS
    tail = n_cols - N_HEAD_COLS - ng
    head_blocks, tail_blocks = N_HEAD_COLS // tn, tail // tn
    assert tail % tn == 0 and N_HEAD_COLS % tn == 0 and n_cols % ng == 0
    last_main = head_blocks + tail_blocks - 1
    pre, kv, gates2 = (MLSTM_WIDTH + ATTN_WIDTH) // tn, 2 * KV_WIDTH // tn, 2 * d // tn
    assert (2 * KV_WIDTH) % tn == 0 and pre + kv + gates2 == tail_blocks

    def out_block(j):
        tb = j - head_blocks
        moved = jnp.where(tb < pre, tb, jnp.where(tb < pre + kv, tb + gates2, tb - kv))
        return jnp.where((tb >= 0) & (tb < tail_blocks), head_blocks + moved, j)

    return pl.pallas_call(
        functools.partial(_in_proj_kernel, head_blocks=head_blocks, tail_blocks=tail_blocks),
        grid=(t // tm, head_blocks + tail_blocks + 1),
        in_specs=[
            pl.BlockSpec((tm, d), lambda m, j: (m, 0)),
            pl.BlockSpec((None, tn, d), lambda m, j: (layer, jnp.minimum(j, last_main), 0)),
            pl.BlockSpec((None, ng, d), lambda m, j: (layer, jnp.minimum((j + 1) * (tn // ng), n_cols // ng - 1), 0)),
            pl.BlockSpec((None, ng, d), lambda m, j: (layer, N_HEAD_COLS // ng, 0)),
        ],
        out_specs=pl.BlockSpec((tm, tn), lambda m, j: (m, out_block(j))),
        out_shape=jax.ShapeDtypeStruct((t, N_HEAD_COLS + tail + LANES), F32),
        compiler_params=_cparams(("parallel", "parallel")),
        name="in_proj",
    )(h, wt_all, wt_all, wt_all)


def _conv_silu_kernel(z_ref, w_ref, b_ref, o_ref, *, q_blocks, q_scale):
    x = z_ref[0]
    s = x.shape[0]
    row = lax.broadcasted_iota(jnp.int32, x.shape, 0)
    prev = jnp.where(row == 0, 0.0, pltpu.roll(x, 1, 0))
    nxt = jnp.where(row == s - 1, 0.0, pltpu.roll(x, s - 1, 0))
    y = prev * w_ref[0:1, :] + x * w_ref[1:2, :] + nxt * w_ref[2:3, :] + b_ref[...]
    y = y * jax.nn.sigmoid(y)
    scale = jnp.where(pl.program_id(1) < q_blocks, q_scale, 1.0).astype(F32)
    o_ref[0] = (y * scale).astype(o_ref.dtype)


def _conv_silu(z3, conv_w, conv_b, *, col0, tc):
    b, s, _ = z3.shape
    width = conv_w.shape[1]
    kern = functools.partial(_conv_silu_kernel, q_blocks=MLSTM_QK_WIDTH // tc, q_scale=MLSTM_QK_DIM ** -0.5)
    return pl.pallas_call(
        kern,
        grid=(b, width // tc),
        in_specs=[
            pl.BlockSpec((1, s, tc), lambda i, j: (i, 0, col0 // tc + j)),
            pl.BlockSpec((3, tc), lambda i, j: (0, j)),
            pl.BlockSpec((1, tc), lambda i, j: (0, j)),
        ],
        out_specs=pl.BlockSpec((1, s, tc), lambda i, j: (i, 0, j)),
        out_shape=jax.ShapeDtypeStruct((b, s, width), BF16),
        compiler_params=_cparams(("parallel", "parallel")),
        name="mlstm_conv_silu",
    )(z3, conv_w, conv_b)


def _log_sigmoid(x):
    return jnp.minimum(x, 0.0) - jnp.log1p(jnp.exp(-jnp.abs(x)))


def _lane_col(x, idx):
    lane = lax.broadcasted_iota(jnp.int32, x.shape, 1)
    return jnp.sum(jnp.where(lane == idx, x, 0.0), axis=-1, keepdims=True)


def _mlstm_kernel(qkf_ref, qkb_ref, vf_ref, vb_ref, gf_ref, gb_ref, bias_ref, hf_ref, hb_ref, c_scr, m_scr):
    L = MLSTM_CHUNK
    dk, dv, nh = MLSTM_QK_DIM, MLSTM_V_DIM, N_MLSTM_HEADS

    @pl.when(pl.program_id(0) == 0)
    def _():
        c_scr[...] = jnp.zeros_like(c_scr)
        m_scr[...] = jnp.zeros_like(m_scr)

    r_i = lax.broadcasted_iota(jnp.int32, (L, L), 0)
    c_i = lax.broadcasted_iota(jnp.int32, (L, L), 1)
    lane = lax.broadcasted_iota(jnp.int32, (L, LANES), 1)
    ones_blk = jnp.where(lane == 0, 1.0, 0.0).astype(F32)

    dirs = ((qkf_ref, vf_ref, gf_ref, hf_ref), (qkb_ref, vb_ref, gb_ref, hb_ref))
    for bi, (d, (qk_ref, v_ref, g_ref, h_ref)) in itertools.product(range(qkf_ref.shape[0]), enumerate(dirs)):
        mask = (c_i <= r_i) if d == 0 else (c_i >= r_i)
        g = g_ref[bi] + bias_ref[...]
        gp = jnp.where(lane < N_GATE_COLS // 2, g, _log_sigmoid(g))
        gp_t = gp.T
        bc = jnp.dot(mask.astype(F32), gp, precision=lax.Precision.HIGHEST, preferred_element_type=F32)
        bc_t = bc.T
        end_row = L - 1 if d == 0 else 0
        for h in range(nh):
            ci = (bi * N_DIRS + d) * nh + h
            ch_i, ch_f = d * nh + h, N_GATE_COLS // 2 + d * nh + h
            i_row = gp_t[ch_i:ch_i + 1, :]
            bc_row = bc_t[ch_f:ch_f + 1, :]
            i_col = _lane_col(gp, ch_i)
            bc_col = _lane_col(bc, ch_f)
            m_prev = m_scr[ci, 0:1, 0:1]
            c_prev = c_scr[ci]

            a_col = bc_col + m_prev
            dm = jnp.where(mask, bc_col - bc_row + i_row, -jnp.inf)
            m_t = jnp.maximum(a_col, jnp.max(dm, axis=-1, keepdims=True))
            w_inter = jnp.exp(a_col - m_t)
            w_intra = jnp.exp(dm - m_t)

            q = qk_ref[bi, :, h * dk:(h + 1) * dk]
            k = qk_ref[bi, :, MLSTM_QK_WIDTH + h * dk:MLSTM_QK_WIDTH + (h + 1) * dk]
            v_ext = jnp.concatenate([v_ref[bi, :, h * dv:(h + 1) * dv], ones_blk], axis=1)
            p = (_dot_nt(q, k) * w_intra).astype(BF16)
            num_ext = w_inter * _dot(q, c_prev.astype(BF16)) + _dot(p, v_ext.astype(BF16))
            den = num_ext[:, dv:dv + 1]
            h_ref[bi, :, h * dv:(h + 1) * dv] = num_ext[:, :dv] / jnp.maximum(jnp.abs(den), jnp.exp(-m_t))

            b_end = bc_col[end_row:end_row + 1, :]
            g_col = b_end - bc_col + i_col
            m_new = jnp.maximum(b_end + m_prev, jnp.max(g_col, axis=0, keepdims=True))
            decay = jnp.exp(b_end + m_prev - m_new)
            ws = jnp.exp(g_col - m_new)
            c_scr[ci] = decay * c_prev + _dot_tn(k, (ws * v_ext).astype(BF16))
            m_scr[ci] = jnp.broadcast_to(m_new, m_scr.shape[1:])


def _mlstm_scan(qk3, z3, gate_bias, *, v_col0, gate_col0):
    b, s, _ = qk3.shape
    L = MLSTM_CHUNK
    nc = s // L
    n_chain = b * N_DIRS * N_MLSTM_HEADS
    vb, gb = v_col0 // MLSTM_WIDTH, gate_col0 // LANES
    qk_w = 2 * MLSTM_QK_WIDTH
    return pl.pallas_call(
        _mlstm_kernel,
        grid=(nc,),
        in_specs=[
            pl.BlockSpec((b, L, qk_w), lambda c: (0, c, 0)),
            pl.BlockSpec((b, L, qk_w), lambda c: (0, nc - 1 - c, 0)),
            pl.BlockSpec((b, L, MLSTM_WIDTH), lambda c: (0, c, vb)),
            pl.BlockSpec((b, L, MLSTM_WIDTH), lambda c: (0, nc - 1 - c, vb)),
            pl.BlockSpec((b, L, LANES), lambda c: (0, c, gb)),
            pl.BlockSpec((b, L, LANES), lambda c: (0, nc - 1 - c, gb)),
            pl.BlockSpec((1, LANES), lambda c: (0, 0)),
        ],
        out_specs=[
            pl.BlockSpec((b, L, MLSTM_WIDTH), lambda c: (0, c, 0)),
            pl.BlockSpec((b, L, MLSTM_WIDTH), lambda c: (0, nc - 1 - c, 0)),
        ],
        out_shape=[jax.ShapeDtypeStruct((b, s, MLSTM_WIDTH), F32)] * 2,
        scratch_shapes=[
            pltpu.VMEM((n_chain, MLSTM_QK_DIM, MLSTM_V_DIM + LANES), F32),
            pltpu.VMEM((n_chain, 8, LANES), F32),
        ],
        compiler_params=_cparams(("arbitrary",)),
        name="mlstm_scan",
    )(qk3, qk3, z3, z3, z3, z3, gate_bias)


LOG2_E = 1.4426950408889634


def _qk_prep_kernel(aq_ref, ak_ref, av_ref, cos_ref, sin_ref, gq_ref, gk_ref, q_ref, k_ref, vt_ref):
    cos, sin = cos_ref[...], sin_ref[...]
    hd = ATTN_HEAD_DIM
    lane = lax.broadcasted_iota(jnp.int32, cos.shape, 1)
    first_half = (lane % (hd // 2)) < (hd // 4)

    def prep(x, g, scale):
        n = _rms(x, g)
        partner = jnp.where(first_half, pltpu.roll(n, hd - hd // 4, 1), pltpu.roll(n, hd // 4, 1))
        return ((n * cos + partner * sin) * scale).astype(BF16)

    for h in range(N_ATTN_HEADS):
        q_ref[0, :, h * hd:(h + 1) * hd] = prep(aq_ref[0, :, h * hd:(h + 1) * hd], gq_ref[...], hd ** -0.5 * LOG2_E)
    for h in range(N_KV_HEADS):
        k_ref[0, :, h * hd:(h + 1) * hd] = prep(ak_ref[0, :, h * hd:(h + 1) * hd], gk_ref[...], 1.0)
    vt_ref[0] = av_ref[0].T.astype(BF16)


def _qk_prep(z3, cos_t, sin_t, gq, gk, *, aq_col0, ak_col0, av_col0, ts):
    b, s, _ = z3.shape
    return pl.pallas_call(
        _qk_prep_kernel,
        grid=(b, s // ts),
        in_specs=[
            pl.BlockSpec((1, ts, ATTN_WIDTH), lambda i, j: (i, j, aq_col0 // ATTN_WIDTH)),
            pl.BlockSpec((1, ts, KV_WIDTH), lambda i, j: (i, j, ak_col0 // KV_WIDTH)),
            pl.BlockSpec((1, ts, KV_WIDTH), lambda i, j: (i, j, av_col0 // KV_WIDTH)),
            pl.BlockSpec((ts, ATTN_HEAD_DIM), lambda i, j: (j, 0)),
            pl.BlockSpec((ts, ATTN_HEAD_DIM), lambda i, j: (j, 0)),
            pl.BlockSpec((1, ATTN_HEAD_DIM), lambda i, j: (0, 0)),
            pl.BlockSpec((1, ATTN_HEAD_DIM), lambda i, j: (0, 0)),
        ],
        out_specs=[
            pl.BlockSpec((1, ts, ATTN_WIDTH), lambda i, j: (i, j, 0)),
            pl.BlockSpec((1, ts, KV_WIDTH), lambda i, j: (i, j, 0)),
            pl.BlockSpec((1, KV_WIDTH, ts), lambda i, j: (i, 0, j)),
        ],
        out_shape=[jax.ShapeDtypeStruct((b, s, ATTN_WIDTH), BF16), jax.ShapeDtypeStruct((b, s, KV_WIDTH), BF16),
                   jax.ShapeDtypeStruct((b, KV_WIDTH, s), BF16)],
        compiler_params=_cparams(("parallel", "parallel")),
        name="attn_qk_norm_rope",
    )(z3, z3, z3, cos_t, sin_t, gq, gk)


SOFTMAX_SHIFT_LIMIT = 60.0


def _flash_kernel(bound_ref, q_ref, k_ref, vt_ref, o_ref, *, grp, tk):
    hd = ATTN_HEAD_DIM
    tq = q_ref.shape[1]
    s_len = k_ref.shape[1]
    rows = grp * tq
    bound = bound_ref[0, 0]
    q = jnp.concatenate([q_ref[0, :, g * hd:(g + 1) * hd] for g in range(grp)], axis=0)
    chunks = [slice(c * tk, (c + 1) * tk) for c in range(s_len // tk)]

    def store(acc, l):
        o = (acc / l).T
        for g in range(grp):
            o_ref[0, :, g * hd:(g + 1) * hd] = o[g * tq:(g + 1) * tq, :].astype(o_ref.dtype)

    @pl.when(bound <= SOFTMAX_SHIFT_LIMIT)
    def _():
        l = jnp.zeros((1, rows), F32)
        acc = jnp.zeros((hd, rows), F32)
        for ck in chunks:
            p = jnp.exp2(_dot_nt(k_ref[0, ck, :], q) - bound)
            l = l + jnp.sum(p, axis=0, keepdims=True)
            acc = acc + _dot(vt_ref[0, :, ck], p.astype(BF16))
        store(acc, l)

    @pl.when(bound > SOFTMAX_SHIFT_LIMIT)
    def _():
        m = jnp.full((1, rows), -jnp.inf, F32)
        l = jnp.zeros((1, rows), F32)
        acc = jnp.zeros((hd, rows), F32)
        for ck in chunks:
            st = _dot_nt(k_ref[0, ck, :], q)
            m_new = jnp.maximum(m, jnp.max(st, axis=0, keepdims=True))
            alpha = jnp.exp2(m - m_new)
            p = jnp.exp2(st - m_new)
            l = alpha * l + jnp.sum(p, axis=0, keepdims=True)
            acc = alpha * acc + _dot(vt_ref[0, :, ck], p.astype(BF16))
            m = m_new
        store(acc, l)


def _score_bound(gq, gk):
    scale = ATTN_HEAD_DIM ** -0.5 * LOG2_E
    return (1.02 * ATTN_HEAD_DIM * scale * jnp.max(jnp.abs(gq)) * jnp.max(jnp.abs(gk))).reshape(1, 1).astype(F32)


def _flash_gqa(q3, k3, vt3, bound, *, tq, tk):
    b, s, _ = q3.shape
    hd = ATTN_HEAD_DIM
    grp = N_ATTN_HEADS // N_KV_HEADS
    return pl.pallas_call(
        functools.partial(_flash_kernel, grp=grp, tk=tk),
        grid=(b, N_KV_HEADS, s // tq),
        in_specs=[
            pl.BlockSpec(memory_space=pltpu.SMEM),
            pl.BlockSpec((1, tq, grp * hd), lambda i, h, qi: (i, qi, h)),
            pl.BlockSpec((1, s, hd), lambda i, h, qi: (i, 0, h)),
            pl.BlockSpec((1, hd, s), lambda i, h, qi: (i, h, 0)),
        ],
        out_specs=pl.BlockSpec((1, tq, grp * hd), lambda i, h, qi: (i, qi, h)),
        out_shape=jax.ShapeDtypeStruct((b, s, ATTN_WIDTH), BF16),
        compiler_params=_cparams(("parallel", "parallel", "parallel")),
        name="gqa_flash",
    )(bound, q3, k3, vt3)


def _mlstm_out_kernel(hf_ref, hb_ref, mo_ref, ng_ref, hm_ref):
    dv = MLSTM_V_DIM
    for h in range(N_MLSTM_HEADS):
        sl = slice(h * dv, (h + 1) * dv)
        hm = _rms(hf_ref[:, sl] + hb_ref[:, sl], ng_ref[:, sl])
        hm_ref[:, sl] = (hm * jax.nn.sigmoid(mo_ref[:, sl])).astype(BF16)


def _mlstm_out(hf, hb, z, ng, *, mo_col0, tm):
    t, w = hf.shape
    row_spec = pl.BlockSpec((tm, w), lambda m: (m, 0))
    return pl.pallas_call(
        _mlstm_out_kernel,
        grid=(t // tm,),
        in_specs=[row_spec, row_spec, pl.BlockSpec((tm, w), lambda m: (m, mo_col0 // w)),
                  pl.BlockSpec((1, w), lambda m: (0, 0))],
        out_specs=row_spec,
        out_shape=jax.ShapeDtypeStruct((t, w), BF16),
        compiler_params=_cparams(("parallel",)),
        name="mlstm_out_norm",
    )(hf, hb, z, ng)


def _merge_kernel(hm_ref, ha_ref, gm_ref, ga_ref, wm_ref, wa_ref, y_ref, wmb_scr, wab_scr):
    @pl.when(pl.program_id(0) == 0)
    def _():
        wmb_scr[...] = wm_ref[...].astype(BF16)
        wab_scr[...] = wa_ref[...].astype(BF16)

    ym = _dot(hm_ref[...], wmb_scr[...])
    ya = _dot(ha_ref[...], wab_scr[...])
    y_ref[...] = (jax.nn.sigmoid(gm_ref[...]) * ym + jax.nn.sigmoid(ga_ref[...]) * ya).astype(y_ref.dtype)


def _merge(hm, ha, z, wm_all, wa_all, layer, *, gm_col0, ga_col0, tm):
    t, w = hm.shape
    d = wm_all.shape[2]
    act_spec = pl.BlockSpec((tm, w), lambda m: (m, 0))
    w_spec = _resident_weight_spec((None, w, d), (layer, 0, 0))
    return pl.pallas_call(
        _merge_kernel,
        grid=(t // tm,),
        in_specs=[
            act_spec,
            act_spec,
            pl.BlockSpec((tm, d), lambda m: (m, gm_col0 // d)),
            pl.BlockSpec((tm, d), lambda m: (m, ga_col0 // d)),
            w_spec,
            w_spec,
        ],
        out_specs=pl.BlockSpec((tm, d), lambda m: (m, 0)),
        out_shape=jax.ShapeDtypeStruct((t, d), BF16),
        scratch_shapes=[pltpu.VMEM((w, d), BF16), pltpu.VMEM((w, d), BF16)],
        compiler_params=_cparams(("arbitrary",)),
        name="branch_merge",
    )(hm, ha, z, z, wm_all, wa_all)


def _resident_weight_spec(shape, index):
    return pl.BlockSpec(shape, lambda *_: index, pipeline_mode=pl.Buffered(1))


def _out_proj_kernel(y_ref, w_ref, x_ref, o_ref, wb_scr):
    @pl.when(pl.program_id(0) == 0)
    def _():
        wb_scr[...] = w_ref[...].astype(BF16)

    o_ref[...] = x_ref[...] + _dot(y_ref[...], wb_scr[...])


def _out_proj(y, w_all, layer, x, *, tm):
    t, d = x.shape
    k = y.shape[1]
    return pl.pallas_call(
        _out_proj_kernel,
        grid=(t // tm,),
        in_specs=[
            pl.BlockSpec((tm, k), lambda m: (m, 0)),
            _resident_weight_spec((None, k, d), (layer, 0, 0)),
            pl.BlockSpec((tm, d), lambda m: (m, 0)),
        ],
        out_specs=pl.BlockSpec((tm, d), lambda m: (m, 0)),
        out_shape=jax.ShapeDtypeStruct((t, d), F32),
        scratch_shapes=[pltpu.VMEM((k, d), BF16)],
        compiler_params=_cparams(("arbitrary",)),
        name="out_proj_residual",
    )(y, w_all, x)


def _swiglu_partial(h, wg_ref, wu_ref, wd_ref):
    a = _dot(h, wg_ref[...].astype(BF16))
    u = _dot(h, wu_ref[...].astype(BF16))
    t = a * jax.nn.sigmoid(a) * u
    return _dot(t.astype(BF16), wd_ref[...].astype(BF16))


def _ffn_kernel(x_ref, g_ref, wg_ref, wu_ref, wd_ref, o_ref, h_scr):
    @pl.when(pl.program_id(1) == 0)
    def _():
        x = x_ref[...]
        h_scr[...] = _rms(x, g_ref[...]).astype(BF16)
        o_ref[...] = x

    o_ref[...] += _swiglu_partial(h_scr[...], wg_ref, wu_ref, wd_ref)


def _ffn(x, g, wg_all, wu_all, wd_all, j, *, tm, tf):
    t, d = x.shape
    ff = wg_all.shape[-1]
    w_in_spec = pl.BlockSpec((None, d, tf), lambda m, f: (j, 0, f))
    return pl.pallas_call(
        _ffn_kernel,
        grid=(t // tm, ff // tf),
        in_specs=[
            pl.BlockSpec((tm, d), lambda m, f: (m, 0)),
            pl.BlockSpec((1, d), lambda m, f: (0, 0)),
            w_in_spec,
            w_in_spec,
            pl.BlockSpec((None, tf, d), lambda m, f: (j, f, 0)),
        ],
        out_specs=pl.BlockSpec((tm, d), lambda m, f: (m, 0)),
        out_shape=jax.ShapeDtypeStruct((t, d), F32),
        scratch_shapes=[pltpu.VMEM((tm, d), BF16)],
        compiler_params=_cparams(("parallel", "arbitrary")),
        name="dense_swiglu",
    )(x, g, wg_all, wu_all, wd_all)


MOE_ROW_TILE = 2560
MOE_SUB_TILE = 256
META_E1, META_E2, META_P1, META_P2, META_R1, META_R2 = range(6)


def _router_kernel(x_ref, g_ref, w_ref, b_ref, meta_ref, cnt_ref, carry_scr):
    @pl.when(pl.program_id(0) == 0)
    def _():
        carry_scr[...] = jnp.zeros_like(carry_scr)

    h = _rms(x_ref[...], g_ref[...])
    logits = jnp.dot(h, w_ref[...], precision=lax.Precision.HIGHEST, preferred_element_type=F32) + b_ref[...]
    tb = logits.shape[0]
    lane = lax.broadcasted_iota(jnp.int32, logits.shape, 1)
    logits = jnp.where(lane < N_EXPERTS, logits, -jnp.inf)
    v1 = jnp.max(logits, axis=-1, keepdims=True)
    i1 = jnp.min(jnp.where(logits == v1, lane, LANES), axis=-1, keepdims=True)
    rest = jnp.where(lane == i1, -jnp.inf, logits)
    v2 = jnp.max(rest, axis=-1, keepdims=True)
    i2 = jnp.min(jnp.where(rest == v2, lane, LANES), axis=-1, keepdims=True)
    e2 = jnp.exp(v2 - v1)
    p1 = 1.0 / (1.0 + e2)
    p2 = e2 / (1.0 + e2)
    sel = ((lane == i1) | (lane == i2)).astype(F32)
    r_i = lax.broadcasted_iota(jnp.int32, (tb, tb), 0)
    c_i = lax.broadcasted_iota(jnp.int32, (tb, tb), 1)
    earlier = (c_i < r_i).astype(BF16)
    rank = carry_scr[0:1, :] + _dot(earlier, sel.astype(BF16))
    r1 = jnp.sum(jnp.where(lane == i1, rank, 0.0), axis=-1, keepdims=True)
    r2 = jnp.sum(jnp.where(lane == i2, rank, 0.0), axis=-1, keepdims=True)
    meta = jnp.zeros_like(logits)
    for idx, val in ((META_E1, i1.astype(F32)), (META_E2, i2.astype(F32)), (META_P1, p1), (META_P2, p2),
                     (META_R1, r1), (META_R2, r2)):
        meta = jnp.where(lane == idx, val, meta)
    meta_ref[...] = meta
    total = carry_scr[0:1, :] + jnp.sum(sel, axis=0, keepdims=True)
    carry_scr[...] = jnp.broadcast_to(total, carry_scr.shape)
    cnt_ref[...] = jnp.broadcast_to(total, cnt_ref.shape)


def _router(x, g, w_pad, b_pad, *, tm):
    t, d = x.shape
    return pl.pallas_call(
        _router_kernel,
        grid=(t // tm,),
        in_specs=[
            pl.BlockSpec((tm, d), lambda m: (m, 0)),
            pl.BlockSpec((1, d), lambda m: (0, 0)),
            pl.BlockSpec((d, LANES), lambda m: (0, 0)),
            pl.BlockSpec((1, LANES), lambda m: (0, 0)),
        ],
        out_specs=[pl.BlockSpec((tm, LANES), lambda m: (m, 0)), pl.BlockSpec((8, LANES), lambda m: (0, 0))],
        out_shape=[jax.ShapeDtypeStruct((t, LANES), F32), jax.ShapeDtypeStruct((8, LANES), F32)],
        scratch_shapes=[pltpu.VMEM((8, LANES), F32)],
        compiler_params=_cparams(("arbitrary",)),
        name="moe_router",
    )(x, g, w_pad, b_pad)


def _moe_plan(meta, cnt, t):
    tile, sub = MOE_ROW_TILE, MOE_SUB_TILE
    n_tiles = 2 * t // tile + N_EXPERTS
    e1 = meta[:, META_E1].astype(jnp.int32)
    e2 = meta[:, META_E2].astype(jnp.int32)
    r1 = meta[:, META_R1].astype(jnp.int32)
    r2 = meta[:, META_R2].astype(jnp.int32)
    counts = cnt[0, :N_EXPERTS].astype(jnp.int32)
    tiles_e = (counts + tile - 1) // tile
    tile_end = jnp.cumsum(tiles_e)
    tile_start = tile_end - tiles_e
    row_start = tile_start * tile
    dest = jnp.concatenate([row_start[e1] + r1, row_start[e2] + r2]).astype(jnp.int32)
    m = jnp.arange(n_tiles, dtype=jnp.int32)
    live = m < tile_end[-1]
    m_eff = jnp.where(live, m, jnp.maximum(tile_end[-1] - 1, 0))
    tile_expert = jnp.minimum(jnp.searchsorted(tile_end, m_eff, side="right"), N_EXPERTS - 1).astype(jnp.int32)
    rows_left = counts[tile_expert] - (m_eff - tile_start[tile_expert]) * tile
    n_sub = jnp.where(live, jnp.clip((rows_left + sub - 1) // sub, 0, tile // sub), 0).astype(jnp.int32)
    sub_base = (jnp.cumsum(n_sub) - n_sub).astype(jnp.int32)
    tok = jnp.arange(t, dtype=jnp.int32)
    src_pos = sub_base[dest // tile] * sub + dest % tile
    n_src = (2 * t // sub + N_EXPERTS) * sub
    src = jnp.zeros((n_src,), jnp.int32).at[src_pos].set(jnp.concatenate([tok, tok]))
    return dest, src, tile_expert, n_sub, sub_base, n_tiles


def _expert_kernel(te_ref, ns_ref, sbase_ref, src_ref, x_hbm, g_ref, wg_ref, wu_ref, wd_ref, ys_hbm,
                   xbuf, h_scr, acc, gsem, zsem, osem):
    del te_ref
    m, f = pl.program_id(0), pl.program_id(1)
    n_m, n_f = pl.num_programs(0), pl.num_programs(1)
    tile, sub = MOE_ROW_TILE, MOE_SUB_TILE
    n_live = ns_ref[m]
    subs = [(sb, slice(sb * sub, (sb + 1) * sub)) for sb in range(tile // sub)]

    def gather_start(sb, slot):
        def start(r, carry):
            tok = src_ref[(sbase_ref[m] + sb) * sub + r]
            pltpu.make_async_copy(x_hbm.at[pl.ds(tok, 1)], xbuf.at[slot, pl.ds(r, 1)], gsem.at[slot]).start()
            return carry

        lax.fori_loop(0, sub, start, 0, unroll=8)

    def result_copy(sb, rows, tile_idx):
        return pltpu.make_async_copy(acc.at[rows], ys_hbm.at[pl.ds(tile_idx * tile + sb * sub, sub)], osem)

    def zero_copy(sb):
        return pltpu.make_async_copy(xbuf.at[0], ys_hbm.at[pl.ds(m * tile + sb * sub, sub)], zsem)

    @pl.when(f == 0)
    def _():
        @pl.when(n_live > 0)
        def _():
            gather_start(0, 0)

        for sb, rows in subs:
            slot = sb % 2
            if sb + 1 < len(subs):
                @pl.when(sb + 1 < n_live)
                def _():
                    gather_start(sb + 1, 1 - slot)

            @pl.when(sb < n_live)
            def _():
                pltpu.make_async_copy(x_hbm.at[pl.ds(0, sub)], xbuf.at[slot], gsem.at[slot]).wait()
                h_scr[rows, :] = _rms(xbuf[slot], g_ref[...]).astype(BF16)

        @pl.when(m > 0)
        def _():
            for sb, rows in subs:
                @pl.when(sb < ns_ref[m - 1])
                def _():
                    result_copy(sb, rows, m - 1).wait()

        for sb, rows in subs:
            @pl.when(sb < n_live)
            def _():
                acc[rows, :] = jnp.zeros((sub, acc.shape[1]), F32)

    @pl.when(f == 1)
    def _():
        xbuf[0] = jnp.zeros(xbuf.shape[1:], F32)
        for sb, rows in subs:
            @pl.when(sb >= n_live)
            def _():
                zero_copy(sb).start()

    for pb in range(len(subs) // 2):
        both = slice(2 * pb * sub, (2 * pb + 2) * sub)
        first = slice(2 * pb * sub, (2 * pb + 1) * sub)

        @pl.when(2 * pb + 2 <= n_live)
        def _():
            acc[both, :] += _swiglu_partial(h_scr[both, :], wg_ref, wu_ref, wd_ref)

        @pl.when(2 * pb + 1 == n_live)
        def _():
            acc[first, :] += _swiglu_partial(h_scr[first, :], wg_ref, wu_ref, wd_ref)

    @pl.when(f == n_f - 1)
    def _():
        for sb, rows in subs:
            @pl.when(sb < n_live)
            def _():
                result_copy(sb, rows, m).start()

            @pl.when(sb >= n_live)
            def _():
                zero_copy(sb).wait()

        @pl.when(m == n_m - 1)
        def _():
            for sb, rows in subs:
                @pl.when(sb < n_live)
                def _():
                    result_copy(sb, rows, m).wait()


def _experts(x, g, wg_all, wu_all, wd_all, j, tile_expert, n_sub, sub_base, src, n_tiles, *, tf):
    t, d = x.shape
    tile = MOE_ROW_TILE
    ff = wg_all.shape[-1]
    n_f = ff // tf

    def f_eff(m, f, ns):
        return jnp.where(ns[m] > 0, f, n_f - 1)

    w_in_spec = pl.BlockSpec((None, None, d, tf), lambda m, f, te, ns, sbase, sr: (j, te[m], 0, f_eff(m, f, ns)))
    w_dn_spec = pl.BlockSpec((None, None, tf, d), lambda m, f, te, ns, sbase, sr: (j, te[m], f_eff(m, f, ns), 0))
    return pl.pallas_call(
        _expert_kernel,
        grid_spec=pltpu.PrefetchScalarGridSpec(
            num_scalar_prefetch=4,
            grid=(n_tiles, n_f),
            in_specs=[
                pl.BlockSpec(memory_space=pl.ANY),
                pl.BlockSpec((1, d), lambda m, f, te, ns, sbase, sr: (0, 0)),
                w_in_spec,
                w_in_spec,
                w_dn_spec,
            ],
            out_specs=pl.BlockSpec(memory_space=pl.ANY),
            scratch_shapes=[
                pltpu.VMEM((2, MOE_SUB_TILE, d), F32),
                pltpu.VMEM((tile, d), BF16),
                pltpu.VMEM((tile, d), F32),
                pltpu.SemaphoreType.DMA((2,)),
                pltpu.SemaphoreType.DMA(()),
                pltpu.SemaphoreType.DMA(()),
            ],
        ),
        out_shape=jax.ShapeDtypeStruct((n_tiles * tile, d), F32),
        compiler_params=_cparams(("arbitrary", "arbitrary")),
        name="moe_experts",
    )(tile_expert, n_sub, sub_base, src, x, g, wg_all, wu_all, wd_all)


def _combine_kernel(dest_ref, x_ref, meta_ref, ys_hbm, o_ref, buf, sem, *, tb, t):
    i, n = pl.program_id(0), pl.num_programs(0)

    def gather_start(tile_idx, slot):
        def start(r, carry):
            for k in range(2):
                row = dest_ref[k * t + tile_idx * tb + r]
                pltpu.make_async_copy(ys_hbm.at[pl.ds(row, 1)], buf.at[slot, k, pl.ds(r, 1)], sem.at[slot]).start()
            return carry

        lax.fori_loop(0, tb, start, 0, unroll=8)

    def combine(slot):
        for k in range(2):
            pltpu.make_async_copy(ys_hbm.at[pl.ds(0, tb)], buf.at[slot, k], sem.at[slot]).wait()
        meta = meta_ref[...]
        o_ref[...] = (x_ref[...] + _lane_col(meta, META_P1) * buf[slot, 0]
                      + _lane_col(meta, META_P2) * buf[slot, 1])

    @pl.when(i == 0)
    def _():
        gather_start(0, 0)

    for slot in range(2):
        @pl.when(i % 2 == slot)
        def _():
            @pl.when(i + 1 < n)
            def _():
                gather_start(i + 1, 1 - slot)

            combine(slot)


def _combine(x, meta, ys, dest, *, tb):
    t, d = x.shape
    return pl.pallas_call(
        functools.partial(_combine_kernel, tb=tb, t=t),
        grid_spec=pltpu.PrefetchScalarGridSpec(
            num_scalar_prefetch=1,
            grid=(t // tb,),
            in_specs=[
                pl.BlockSpec((tb, d), lambda i, dst: (i, 0)),
                pl.BlockSpec((tb, LANES), lambda i, dst: (i, 0)),
                pl.BlockSpec(memory_space=pl.ANY),
            ],
            out_specs=pl.BlockSpec((tb, d), lambda i, dst: (i, 0)),
            scratch_shapes=[pltpu.VMEM((2, 2, tb, d), F32), pltpu.SemaphoreType.DMA((2,))],
        ),
        out_shape=jax.ShapeDtypeStruct((t, d), F32),
        compiler_params=_cparams(("arbitrary",)),
        name="moe_combine",
    )(dest, x, meta, ys)


def _moe(x, g, router_w, router_b, wg_all, wu_all, wd_all, j, *, tb_route, tf):
    t, d = x.shape
    w_pad = jnp.pad(router_w, ((0, 0), (0, LANES - N_EXPERTS)))
    b_pad = jnp.pad(router_b, (0, LANES - N_EXPERTS))[None, :]
    meta, cnt = _router(x, g, w_pad, b_pad, tm=tb_route)
    dest, src, tile_expert, n_sub, sub_base, n_tiles = _moe_plan(meta, cnt, t)
    ys = _experts(x, g, wg_all, wu_all, wd_all, j, tile_expert, n_sub, sub_base, src, n_tiles, tf=tf)
    return _combine(x, meta, ys, dest, tb=min(256, t))


def _ple_kernel(x_ref, g_ref, wg_ref, p_ref, wp_ref, gn_ref, o_ref, hn_ref, wgb_scr, wpb_scr):
    @pl.when(pl.program_id(0) == 0)
    def _():
        wgb_scr[...] = wg_ref[...].astype(BF16)
        wpb_scr[...] = wp_ref[...].astype(BF16)

    x = x_ref[...]
    gate = jax.nn.sigmoid(_dot(_rms(x, g_ref[...]).astype(BF16), wgb_scr[...]))
    x_new = x + gate * _dot(p_ref[...].astype(BF16), wpb_scr[...])
    o_ref[...] = x_new
    hn_ref[...] = _rms(x_new, gn_ref[...]).astype(BF16)


def _ple(x, g, wg_all, p_all, wp_all, layer, g_next, *, tm):
    t, d = x.shape
    pd = p_all.shape[-1]
    row_spec = pl.BlockSpec((tm, d), lambda m: (m, 0))
    gain_spec = pl.BlockSpec((1, d), lambda m: (0, 0))
    return pl.pallas_call(
        _ple_kernel,
        grid=(t // tm,),
        in_specs=[
            row_spec,
            gain_spec,
            _resident_weight_spec((None, d, d), (layer, 0, 0)),
            pl.BlockSpec((None, tm, pd), lambda m: (layer, m, 0)),
            _resident_weight_spec((None, pd, d), (layer, 0, 0)),
            gain_spec,
        ],
        out_specs=[row_spec, row_spec],
        out_shape=[jax.ShapeDtypeStruct((t, d), F32), jax.ShapeDtypeStruct((t, d), BF16)],
        scratch_shapes=[pltpu.VMEM((d, d), BF16), pltpu.VMEM((pd, d), BF16)],
        compiler_params=_cparams(("arbitrary",)),
        name="ple_gate",
    )(x, g, wg_all, p_all, wp_all, g_next)


def _rope_tables(seq):
    rows = seq // GRID_W
    row = jnp.broadcast_to(jnp.arange(rows, dtype=F32)[:, None], (rows, GRID_W)).reshape(seq)
    col = jnp.broadcast_to(jnp.arange(GRID_W, dtype=F32)[None, :], (rows, GRID_W)).reshape(seq)
    axis_dim = ATTN_HEAD_DIM // 2
    inv_freq = ROPE_THETA ** (-jnp.arange(0, axis_dim, 2, dtype=F32) / axis_dim)
    ar, ac = row[:, None] * inv_freq, col[:, None] * inv_freq
    cos_t = jnp.concatenate([jnp.cos(ar), jnp.cos(ar), jnp.cos(ac), jnp.cos(ac)], axis=-1)
    sin_t = jnp.concatenate([-jnp.sin(ar), jnp.sin(ar), -jnp.sin(ac), jnp.sin(ac)], axis=-1)
    return cos_t, sin_t


def kernel(x, p, norm_mix_g, w_in, conv_w, conv_b, b_igate, b_fgate, mlstm_norm_g, q_norm_g, k_norm_g, w_mlstm_up, w_attn_up, w_out, norm_ffn_g, ffn_w_gate, ffn_w_up, ffn_w_down, moe_router, moe_router_b, moe_w_gate, moe_w_up, moe_w_down, norm_ple_g, w_ple_gate, w_ple_proj):
    b, s, d = x.shape
    depth = w_in.shape[0]
    t = b * s
    col = _col_layout(d)
    cos_t, sin_t = _rope_tables(s)
    xt = x.reshape(t, d)
    p_all = p.reshape(depth, t, p.shape[-1])
    w_in_t = jnp.swapaxes(w_in, 1, 2)

    tm_big, tm_half = min(1024, t), min(512, t)
    tn = min(512, d)
    tf = 256

    h_mix = _rms_cast(xt, norm_mix_g[0][None, :], tm=tm_half)
    for i in range(depth):
        z = _in_proj(h_mix, w_in_t, i, tm=min(2048, t), tn=tn)
        z3 = z.reshape(b, s, col["total"])
        qk3 = _conv_silu(z3, conv_w[i], conv_b[i][None, :], col0=col["qk"], tc=256)
        gate_bias = jnp.concatenate(
            [b_igate[i].reshape(-1), b_fgate[i].reshape(-1), jnp.zeros((LANES - N_GATE_COLS,), F32)])[None, :]
        hf, hb = _mlstm_scan(qk3, z3, gate_bias, v_col0=col["mv"], gate_col0=col["gates"])
        q3, k3, vt3 = _qk_prep(z3, cos_t, sin_t, q_norm_g[i][None, :], k_norm_g[i][None, :],
                               aq_col0=col["aq"], ak_col0=col["ak"], av_col0=col["av"], ts=min(512, s))
        ha = _flash_gqa(q3, k3, vt3, _score_bound(q_norm_g[i], k_norm_g[i]), tq=min(256, s), tk=min(2048, s))
        hm = _mlstm_out(hf.reshape(t, MLSTM_WIDTH), hb.reshape(t, MLSTM_WIDTH), z,
                        mlstm_norm_g[i].reshape(1, MLSTM_WIDTH), mo_col0=col["mo"], tm=tm_half)
        y = _merge(hm, ha.reshape(t, ATTN_WIDTH), z, w_mlstm_up, w_attn_up, i,
                   gm_col0=col["gm"], ga_col0=col["ga"], tm=tm_half)
        xt = _out_proj(y, w_out, i, xt, tm=tm_half)

        j = i // 2
        g_ffn = norm_ffn_g[i][None, :]
        if i % 2 == 0:
            xt = _ffn(xt, g_ffn, ffn_w_gate, ffn_w_up, ffn_w_down, j, tm=tm_big, tf=tf)
        else:
            xt = _moe(xt, g_ffn, moe_router[j], moe_router_b[j], moe_w_gate, moe_w_up, moe_w_down, j,
                      tb_route=tm_half, tf=tf)

        g_next = norm_mix_g[min(i + 1, depth - 1)][None, :]
        xt, h_mix = _ple(xt, norm_ple_g[i][None, :], w_ple_gate, p_all, w_ple_proj, i, g_next, tm=tm_half)

    return xt.reshape(b, s, d)
```

```python
import functools
import itertools

import jax
import jax.numpy as jnp
from jax import lax
from jax.experimental import pallas as pl
from jax.experimental.pallas import tpu as pltpu

F32 = jnp.float32
BF16 = jnp.bfloat16

GRID_W = 64
N_MLSTM_HEADS = 4
MLSTM_QK_DIM = 128
MLSTM_V_DIM = 256
MLSTM_QK_WIDTH = N_MLSTM_HEADS * MLSTM_QK_DIM
MLSTM_WIDTH = N_MLSTM_HEADS * MLSTM_V_DIM
MLSTM_CHUNK = 256
N_DIRS = 2
N_ATTN_HEADS = 8
N_KV_HEADS = 2
ATTN_HEAD_DIM = 128
ATTN_WIDTH = N_ATTN_HEADS * ATTN_HEAD_DIM
KV_WIDTH = N_KV_HEADS * ATTN_HEAD_DIM
ROPE_THETA = 10000.0
N_EXPERTS = 8
EPS = 1e-6

LANES = 128
SUBLANES = 8
VMEM_LIMIT_BYTES = 56 * 2**20

N_GATE_COLS = 2 * N_DIRS * N_MLSTM_HEADS


def _col_layout(d_model):
    names = ("qk", "mv", "mo", "aq", "gm", "ga", "ak", "av")
    widths = (2 * MLSTM_QK_WIDTH, MLSTM_WIDTH, MLSTM_WIDTH, ATTN_WIDTH, d_model, d_model, KV_WIDTH, KV_WIDTH)
    off, out = 0, {}
    for n, w in zip(names, widths):
        out[n] = off
        off += w
    out["total"] = off
    return out


N_HEAD_COLS = 2 * MLSTM_QK_WIDTH + MLSTM_WIDTH


def _cparams(semantics):
    return pltpu.CompilerParams(dimension_semantics=semantics, vmem_limit_bytes=VMEM_LIMIT_BYTES)


def _rms(x, g):
    ms = jnp.mean(x * x, axis=-1, keepdims=True)
    return x * lax.rsqrt(ms + EPS) * g


def _dot(a, b):
    return jnp.dot(a, b, preferred_element_type=F32)


def _dot_nt(a, b):
    return lax.dot_general(a, b, (((1,), (1,)), ((), ())), preferred_element_type=F32)


def _dot_tn(a, b):
    return lax.dot_general(a, b, (((0,), (0,)), ((), ())), preferred_element_type=F32)


def _rms_cast_kernel(x_ref, g_ref, h_ref):
    h_ref[...] = _rms(x_ref[...], g_ref[...]).astype(BF16)


def _rms_cast(x, g, *, tm):
    t, d = x.shape
    row_spec = pl.BlockSpec((tm, d), lambda m: (m, 0))
    return pl.pallas_call(
        _rms_cast_kernel,
        grid=(t // tm,),
        in_specs=[row_spec, pl.BlockSpec((1, d), lambda m: (0, 0))],
        out_specs=row_spec,
        out_shape=jax.ShapeDtypeStruct((t, d), BF16),
        compiler_params=_cparams(("parallel",)),
        name="input_rms_norm",
    )(x, g)


def _in_proj_kernel(h_ref, wm_ref, wn_ref, wg_ref, o_ref, gates_ref, *, head_blocks, tail_blocks):
    j = pl.program_id(1)
    ng = N_GATE_COLS

    @pl.when(j < head_blocks)
    def _():
        o_ref[...] = _dot_nt(h_ref[...], wm_ref[...].astype(BF16)).astype(o_ref.dtype)

    @pl.when((j >= head_blocks) & (j < head_blocks + tail_blocks))
    def _():
        w = jnp.concatenate([wm_ref[ng:, :], wn_ref[...]], axis=0).astype(BF16)
        o_ref[...] = _dot_nt(h_ref[...], w).astype(o_ref.dtype)

    @pl.when(j == head_blocks + tail_blocks)
    def _():
        pad = jnp.zeros((gates_ref.shape[1] - ng, wg_ref.shape[1]), F32)
        w = jnp.concatenate([wg_ref[...], pad], axis=0).astype(BF16)
        gates_ref[...] = _dot_nt(h_ref[...], w)


def _in_proj(h, wt_all, layer, *, tm, tn):
    t, d = h.shape
    n_cols = wt_all.shape[1]
    ng = N_GATE_COLS
    tail = n_cols - N_HEAD_COLS - ng
    head_blocks, tail_blocks = N_HEAD_COLS // tn, tail // tn
    assert tail % tn == 0 and N_HEAD_COLS % tn == 0 and n_cols % ng == 0
    last_main = head_blocks + tail_blocks - 1
    pre, kv, gates2 = (MLSTM_WIDTH + ATTN_WIDTH) // tn, 2 * KV_WIDTH // tn, 2 * d // tn
    assert (2 * KV_WIDTH) % tn == 0 and pre + kv + gates2 == tail_blocks

    def out_block(j):
        jz = jnp.minimum(j, last_main)
        tb = jz - head_blocks
        moved = jnp.where(tb < pre, tb, jnp.where(tb < pre + kv, tb + gates2, tb - kv))
        return jnp.where(tb >= 0, head_blocks + moved, jz)

    return pl.pallas_call(
        functools.partial(_in_proj_kernel, head_blocks=head_blocks, tail_blocks=tail_blocks),
        grid=(t // tm, head_blocks + tail_blocks + 1),
        in_specs=[
            pl.BlockSpec((tm, d), lambda m, j: (m, 0)),
            pl.BlockSpec((None, tn, d), lambda m, j: (layer, jnp.minimum(j, last_main), 0)),
            pl.BlockSpec((None, ng, d), lambda m, j: (layer, jnp.minimum((j + 1) * (tn // ng), n_cols // ng - 1), 0)),
            pl.BlockSpec((None, ng, d), lambda m, j: (layer, N_HEAD_COLS // ng, 0)),
        ],
        out_specs=[pl.BlockSpec((tm, tn), lambda m, j: (m, out_block(j))),
                   pl.BlockSpec((tm, LANES), lambda m, j: (m, 0))],
        out_shape=[jax.ShapeDtypeStruct((t, N_HEAD_COLS + tail), BF16), jax.ShapeDtypeStruct((t, LANES), F32)],
        compiler_params=_cparams(("parallel", "arbitrary")),
        name="in_proj",
    )(h, wt_all, wt_all, wt_all)


def _conv_silu_kernel(z_ref, w_ref, b_ref, o_ref, *, q_blocks, q_scale):
    x = z_ref[0].astype(F32)
    s = x.shape[0]
    row = lax.broadcasted_iota(jnp.int32, x.shape, 0)
    prev = jnp.where(row == 0, 0.0, pltpu.roll(x, 1, 0))
    nxt = jnp.where(row == s - 1, 0.0, pltpu.roll(x, s - 1, 0))
    y = prev * w_ref[0:1, :] + x * w_ref[1:2, :] + nxt * w_ref[2:3, :] + b_ref[...]
    y = y * jax.nn.sigmoid(y)
    scale = jnp.where(pl.program_id(1) < q_blocks, q_scale, 1.0).astype(F32)
    o_ref[0] = (y * scale).astype(o_ref.dtype)


def _conv_silu(z3, conv_w, conv_b, *, col0, tc):
    b, s, _ = z3.shape
    width = conv_w.shape[1]
    kern = functools.partial(_conv_silu_kernel, q_blocks=MLSTM_QK_WIDTH // tc, q_scale=MLSTM_QK_DIM ** -0.5)
    return pl.pallas_call(
        kern,
        grid=(b, width // tc),
        in_specs=[
            pl.BlockSpec((1, s, tc), lambda i, j: (i, 0, col0 // tc + j)),
            pl.BlockSpec((3, tc), lambda i, j: (0, j)),
            pl.BlockSpec((1, tc), lambda i, j: (0, j)),
        ],
        out_specs=pl.BlockSpec((1, s, tc), lambda i, j: (i, 0, j)),
        out_shape=jax.ShapeDtypeStruct((b, s, width), BF16),
        compiler_params=_cparams(("parallel", "parallel")),
        name="mlstm_conv_silu",
    )(z3, conv_w, conv_b)


def _log_sigmoid(x):
    return jnp.minimum(x, 0.0) - jnp.log1p(jnp.exp(-jnp.abs(x)))


def _lane_col(x, idx):
    lane = lax.broadcasted_iota(jnp.int32, x.shape, 1)
    return jnp.sum(jnp.where(lane == idx, x, 0.0), axis=-1, keepdims=True)


def _mlstm_kernel(qkf_ref, qkb_ref, vf_ref, vb_ref, gf_ref, gb_ref, bias_ref, hf_ref, hb_ref, c_scr, m_scr):
    L = MLSTM_CHUNK
    dk, dv, nh = MLSTM_QK_DIM, MLSTM_V_DIM, N_MLSTM_HEADS

    @pl.when(pl.program_id(0) == 0)
    def _():
        c_scr[...] = jnp.zeros_like(c_scr)
        m_scr[...] = jnp.zeros_like(m_scr)

    r_i = lax.broadcasted_iota(jnp.int32, (L, L), 0)
    c_i = lax.broadcasted_iota(jnp.int32, (L, L), 1)
    lane = lax.broadcasted_iota(jnp.int32, (L, LANES), 1)
    ones_blk = jnp.where(lane == 0, 1.0, 0.0).astype(F32)

    dirs = ((qkf_ref, vf_ref, gf_ref, hf_ref), (qkb_ref, vb_ref, gb_ref, hb_ref))
    for bi, (d, (qk_ref, v_ref, g_ref, h_ref)) in itertools.product(range(qkf_ref.shape[0]), enumerate(dirs)):
        mask = (c_i <= r_i) if d == 0 else (c_i >= r_i)
        g = g_ref[bi] + bias_ref[...]
        gp = jnp.where(lane < N_GATE_COLS // 2, g, _log_sigmoid(g))
        gp_t = gp.T
        bc = jnp.dot(mask.astype(F32), gp, precision=lax.Precision.HIGHEST, preferred_element_type=F32)
        bc_t = bc.T
        end_row = L - 1 if d == 0 else 0
        for h in range(nh):
            ci = (bi * N_DIRS + d) * nh + h
            ch_i, ch_f = d * nh + h, N_GATE_COLS // 2 + d * nh + h
            i_row = gp_t[ch_i:ch_i + 1, :]
            bc_row = bc_t[ch_f:ch_f + 1, :]
            i_col = _lane_col(gp, ch_i)
            bc_col = _lane_col(bc, ch_f)
            m_prev = m_scr[ci, 0:1, 0:1]
            c_prev = c_scr[ci]

            a_col = bc_col + m_prev
            dm = jnp.where(mask, bc_col - bc_row + i_row, -jnp.inf)
            m_t = jnp.maximum(a_col, jnp.max(dm, axis=-1, keepdims=True))
            w_inter = jnp.exp(a_col - m_t)
            w_intra = jnp.exp(dm - m_t)

            q = qk_ref[bi, :, h * dk:(h + 1) * dk]
            k = qk_ref[bi, :, MLSTM_QK_WIDTH + h * dk:MLSTM_QK_WIDTH + (h + 1) * dk]
            v = v_ref[bi, :, h * dv:(h + 1) * dv].astype(F32)
            v_ext = jnp.concatenate([v, ones_blk], axis=1)
            p = (_dot_nt(q, k) * w_intra).astype(BF16)
            num_ext = w_inter * _dot(q, c_prev.astype(BF16)) + _dot(p, v_ext.astype(BF16))
            den = num_ext[:, dv:dv + 1]
            h_out = num_ext[:, :dv] / jnp.maximum(jnp.abs(den), jnp.exp(-m_t))
            h_ref[bi, :, h * dv:(h + 1) * dv] = h_out.astype(h_ref.dtype)

            b_end = bc_col[end_row:end_row + 1, :]
            g_col = b_end - bc_col + i_col
            m_new = jnp.maximum(b_end + m_prev, jnp.max(g_col, axis=0, keepdims=True))
            decay = jnp.exp(b_end + m_prev - m_new)
            ws = jnp.exp(g_col - m_new)
            c_scr[ci] = decay * c_prev + _dot_tn(k, (ws * v_ext).astype(BF16))
            m_scr[ci] = jnp.broadcast_to(m_new, m_scr.shape[1:])


def _mlstm_scan(qk3, z3, gates3, gate_bias, *, v_col0):
    b, s, _ = qk3.shape
    L = MLSTM_CHUNK
    assert s % L == 0
    nc = s // L
    n_chain = b * N_DIRS * N_MLSTM_HEADS
    vb = v_col0 // MLSTM_WIDTH
    qk_w = 2 * MLSTM_QK_WIDTH
    return pl.pallas_call(
        _mlstm_kernel,
        grid=(nc,),
        in_specs=[
            pl.BlockSpec((b, L, qk_w), lambda c: (0, c, 0)),
            pl.BlockSpec((b, L, qk_w), lambda c: (0, nc - 1 - c, 0)),
            pl.BlockSpec((b, L, MLSTM_WIDTH), lambda c: (0, c, vb)),
            pl.BlockSpec((b, L, MLSTM_WIDTH), lambda c: (0, nc - 1 - c, vb)),
            pl.BlockSpec((b, L, LANES), lambda c: (0, c, 0)),
            pl.BlockSpec((b, L, LANES), lambda c: (0, nc - 1 - c, 0)),
            pl.BlockSpec((1, LANES), lambda c: (0, 0)),
        ],
        out_specs=[
            pl.BlockSpec((b, L, MLSTM_WIDTH), lambda c: (0, c, 0)),
            pl.BlockSpec((b, L, MLSTM_WIDTH), lambda c: (0, nc - 1 - c, 0)),
        ],
        out_shape=[jax.ShapeDtypeStruct((b, s, MLSTM_WIDTH), BF16)] * 2,
        scratch_shapes=[
            pltpu.VMEM((n_chain, MLSTM_QK_DIM, MLSTM_V_DIM + LANES), F32),
            pltpu.VMEM((n_chain, SUBLANES, LANES), F32),
        ],
        compiler_params=_cparams(("arbitrary",)),
        name="mlstm_scan",
    )(qk3, qk3, z3, z3, gates3, gates3, gate_bias)


LOG2_E = 1.4426950408889634


def _qk_prep_kernel(aq_ref, ak_ref, av_ref, cos_ref, sin_ref, gq_ref, gk_ref, q_ref, k_ref, vt_ref):
    cos, sin = cos_ref[...], sin_ref[...]
    hd = ATTN_HEAD_DIM
    lane = lax.broadcasted_iota(jnp.int32, cos.shape, 1)
    first_half = (lane % (hd // 2)) < (hd // 4)

    def prep(x, g, scale):
        n = _rms(x.astype(F32), g)
        partner = jnp.where(first_half, pltpu.roll(n, hd - hd // 4, 1), pltpu.roll(n, hd // 4, 1))
        return ((n * cos + partner * sin) * scale).astype(BF16)

    for h in range(N_ATTN_HEADS):
        q_ref[0, :, h * hd:(h + 1) * hd] = prep(aq_ref[0, :, h * hd:(h + 1) * hd], gq_ref[...], hd ** -0.5 * LOG2_E)
    for h in range(N_KV_HEADS):
        k_ref[0, :, h * hd:(h + 1) * hd] = prep(ak_ref[0, :, h * hd:(h + 1) * hd], gk_ref[...], 1.0)
    vt_ref[0] = av_ref[0].astype(F32).T.astype(BF16)


def _qk_prep(z3, cos_t, sin_t, gq, gk, *, aq_col0, ak_col0, av_col0, ts):
    b, s, _ = z3.shape
    return pl.pallas_call(
        _qk_prep_kernel,
        grid=(b, s // ts),
        in_specs=[
            pl.BlockSpec((1, ts, ATTN_WIDTH), lambda i, j: (i, j, aq_col0 // ATTN_WIDTH)),
            pl.BlockSpec((1, ts, KV_WIDTH), lambda i, j: (i, j, ak_col0 // KV_WIDTH)),
            pl.BlockSpec((1, ts, KV_WIDTH), lambda i, j: (i, j, av_col0 // KV_WIDTH)),
            pl.BlockSpec((ts, ATTN_HEAD_DIM), lambda i, j: (j, 0)),
            pl.BlockSpec((ts, ATTN_HEAD_DIM), lambda i, j: (j, 0)),
            pl.BlockSpec((1, ATTN_HEAD_DIM), lambda i, j: (0, 0)),
            pl.BlockSpec((1, ATTN_HEAD_DIM), lambda i, j: (0, 0)),
        ],
        out_specs=[
            pl.BlockSpec((1, ts, ATTN_WIDTH), lambda i, j: (i, j, 0)),
            pl.BlockSpec((1, ts, KV_WIDTH), lambda i, j: (i, j, 0)),
            pl.BlockSpec((1, KV_WIDTH, ts), lambda i, j: (i, 0, j)),
        ],
        out_shape=[jax.ShapeDtypeStruct((b, s, ATTN_WIDTH), BF16), jax.ShapeDtypeStruct((b, s, KV_WIDTH), BF16),
                   jax.ShapeDtypeStruct((b, KV_WIDTH, s), BF16)],
        compiler_params=_cparams(("parallel", "parallel")),
        name="attn_qk_norm_rope",
    )(z3, z3, z3, cos_t, sin_t, gq, gk)


SOFTMAX_SHIFT_LIMIT = 60.0


def _flash_kernel(bound_ref, q_ref, k_ref, vt_ref, o_ref, *, grp, tk):
    hd = ATTN_HEAD_DIM
    tq = q_ref.shape[1]
    s_len = k_ref.shape[1]
    rows = grp * tq
    bound = bound_ref[0, 0]
    q = jnp.concatenate([q_ref[0, :, g * hd:(g + 1) * hd] for g in range(grp)], axis=0)
    chunks = [slice(c * tk, (c + 1) * tk) for c in range(s_len // tk)]

    def store(acc, l):
        o = (acc / l).T
        for g in range(grp):
            o_ref[0, :, g * hd:(g + 1) * hd] = o[g * tq:(g + 1) * tq, :].astype(o_ref.dtype)

    @pl.when(bound <= SOFTMAX_SHIFT_LIMIT)
    def _():
        l = jnp.zeros((1, rows), F32)
        acc = jnp.zeros((hd, rows), F32)
        for ck in chunks:
            p = jnp.exp2(_dot_nt(k_ref[0, ck, :], q) - bound)
            l = l + jnp.sum(p, axis=0, keepdims=True)
            acc = acc + _dot(vt_ref[0, :, ck], p.astype(BF16))
        store(acc, l)

    @pl.when(bound > SOFTMAX_SHIFT_LIMIT)
    def _():
        m = jnp.full((1, rows), -jnp.inf, F32)
        l = jnp.zeros((1, rows), F32)
        acc = jnp.zeros((hd, rows), F32)
        for ck in chunks:
            st = _dot_nt(k_ref[0, ck, :], q)
            m_new = jnp.maximum(m, jnp.max(st, axis=0, keepdims=True))
            alpha = jnp.exp2(m - m_new)
            p = jnp.exp2(st - m_new)
            l = alpha * l + jnp.sum(p, axis=0, keepdims=True)
            acc = alpha * acc + _dot(vt_ref[0, :, ck], p.astype(BF16))
            m = m_new
        store(acc, l)


def _score_bound(gq, gk):
    scale = ATTN_HEAD_DIM ** -0.5 * LOG2_E
    return (1.02 * ATTN_HEAD_DIM * scale * jnp.max(jnp.abs(gq)) * jnp.max(jnp.abs(gk))).reshape(1, 1).astype(F32)


def _flash_gqa(q3, k3, vt3, bound, *, tq, tk):
    b, s, _ = q3.shape
    hd = ATTN_HEAD_DIM
    grp = N_ATTN_HEADS // N_KV_HEADS
    return pl.pallas_call(
        functools.partial(_flash_kernel, grp=grp, tk=tk),
        grid=(b, N_KV_HEADS, s // tq),
        in_specs=[
            pl.BlockSpec(memory_space=pltpu.SMEM),
            pl.BlockSpec((1, tq, grp * hd), lambda i, h, qi: (i, qi, h)),
            pl.BlockSpec((1, s, hd), lambda i, h, qi: (i, 0, h)),
            pl.BlockSpec((1, hd, s), lambda i, h, qi: (i, h, 0)),
        ],
        out_specs=pl.BlockSpec((1, tq, grp * hd), lambda i, h, qi: (i, qi, h)),
        out_shape=jax.ShapeDtypeStruct((b, s, ATTN_WIDTH), BF16),
        compiler_params=_cparams(("parallel", "parallel", "parallel")),
        name="gqa_flash",
    )(bound, q3, k3, vt3)


def _mlstm_out_kernel(hf_ref, hb_ref, mo_ref, ng_ref, hm_ref):
    dv = MLSTM_V_DIM
    for h in range(N_MLSTM_HEADS):
        sl = slice(h * dv, (h + 1) * dv)
        hm = _rms(hf_ref[:, sl].astype(F32) + hb_ref[:, sl].astype(F32), ng_ref[:, sl])
        hm_ref[:, sl] = (hm * jax.nn.sigmoid(mo_ref[:, sl].astype(F32))).astype(BF16)


def _mlstm_out(hf, hb, z, ng, *, mo_col0, tm):
    t, w = hf.shape
    row_spec = pl.BlockSpec((tm, w), lambda m: (m, 0))
    return pl.pallas_call(
        _mlstm_out_kernel,
        grid=(t // tm,),
        in_specs=[row_spec, row_spec, pl.BlockSpec((tm, w), lambda m: (m, mo_col0 // w)),
                  pl.BlockSpec((1, w), lambda m: (0, 0))],
        out_specs=row_spec,
        out_shape=jax.ShapeDtypeStruct((t, w), BF16),
        compiler_params=_cparams(("parallel",)),
        name="mlstm_out_norm",
    )(hf, hb, z, ng)


def _merge_kernel(hm_ref, ha_ref, gm_ref, ga_ref, wm_ref, wa_ref, y_ref, wmb_scr, wab_scr):
    @pl.when(pl.program_id(0) == 0)
    def _():
        wmb_scr[...] = wm_ref[...].astype(BF16)
        wab_scr[...] = wa_ref[...].astype(BF16)

    ym = _dot(hm_ref[...], wmb_scr[...])
    ya = _dot(ha_ref[...], wab_scr[...])
    gate_m = jax.nn.sigmoid(gm_ref[...].astype(F32))
    gate_a = jax.nn.sigmoid(ga_ref[...].astype(F32))
    y_ref[...] = (gate_m * ym + gate_a * ya).astype(y_ref.dtype)


def _merge(hm, ha, z, wm_all, wa_all, layer, *, gm_col0, ga_col0, tm):
    t, w = hm.shape
    d = wm_all.shape[2]
    act_spec = pl.BlockSpec((tm, w), lambda m: (m, 0))
    w_spec = _resident_weight_spec((None, w, d), (layer, 0, 0))
    return pl.pallas_call(
        _merge_kernel,
        grid=(t // tm,),
        in_specs=[
            act_spec,
            act_spec,
            pl.BlockSpec((tm, d), lambda m: (m, gm_col0 // d)),
            pl.BlockSpec((tm, d), lambda m: (m, ga_col0 // d)),
            w_spec,
            w_spec,
        ],
        out_specs=pl.BlockSpec((tm, d), lambda m: (m, 0)),
        out_shape=jax.ShapeDtypeStruct((t, d), BF16),
        scratch_shapes=[pltpu.VMEM((w, d), BF16), pltpu.VMEM((w, d), BF16)],
        compiler_params=_cparams(("arbitrary",)),
        name="branch_merge",
    )(hm, ha, z, z, wm_all, wa_all)


def _resident_weight_spec(shape, index):
    return pl.BlockSpec(shape, lambda *_: index, pipeline_mode=pl.Buffered(1))


def _out_proj_kernel(y_ref, w_ref, x_ref, o_ref, wb_scr):
    @pl.when(pl.program_id(0) == 0)
    def _():
        wb_scr[...] = w_ref[...].astype(BF16)

    o_ref[...] = x_ref[...] + _dot(y_ref[...], wb_scr[...])


def _out_proj(y, w_all, layer, x, *, tm):
    t, d = x.shape
    k = y.shape[1]
    return pl.pallas_call(
        _out_proj_kernel,
        grid=(t // tm,),
        in_specs=[
            pl.BlockSpec((tm, k), lambda m: (m, 0)),
            _resident_weight_spec((None, k, d), (layer, 0, 0)),
            pl.BlockSpec((tm, d), lambda m: (m, 0)),
        ],
        out_specs=pl.BlockSpec((tm, d), lambda m: (m, 0)),
        out_shape=jax.ShapeDtypeStruct((t, d), F32),
        scratch_shapes=[pltpu.VMEM((k, d), BF16)],
        compiler_params=_cparams(("arbitrary",)),
        name="out_proj_residual",
    )(y, w_all, x)


def _swiglu_partial(h, wg_ref, wu_ref, wd_ref):
    a = _dot(h, wg_ref[...].astype(BF16))
    u = _dot(h, wu_ref[...].astype(BF16))
    t = a * jax.nn.sigmoid(a) * u
    return _dot(t.astype(BF16), wd_ref[...].astype(BF16))


def _ffn_kernel(x_ref, g_ref, wg_ref, wu_ref, wd_ref, o_ref, h_scr):
    @pl.when(pl.program_id(1) == 0)
    def _():
        x = x_ref[...]
        h_scr[...] = _rms(x, g_ref[...]).astype(BF16)
        o_ref[...] = x

    o_ref[...] += _swiglu_partial(h_scr[...], wg_ref, wu_ref, wd_ref)


def _ffn(x, g, wg_all, wu_all, wd_all, j, *, tm, tf):
    t, d = x.shape
    ff = wg_all.shape[-1]
    w_in_spec = pl.BlockSpec((None, d, tf), lambda m, f: (j, 0, f))
    return pl.pallas_call(
        _ffn_kernel,
        grid=(t // tm, ff // tf),
        in_specs=[
            pl.BlockSpec((tm, d), lambda m, f: (m, 0)),
            pl.BlockSpec((1, d), lambda m, f: (0, 0)),
            w_in_spec,
            w_in_spec,
            pl.BlockSpec((None, tf, d), lambda m, f: (j, f, 0)),
        ],
        out_specs=pl.BlockSpec((tm, d), lambda m, f: (m, 0)),
        out_shape=jax.ShapeDtypeStruct((t, d), F32),
        scratch_shapes=[pltpu.VMEM((tm, d), BF16)],
        compiler_params=_cparams(("parallel", "arbitrary")),
        name="dense_swiglu",
    )(x, g, wg_all, wu_all, wd_all)


MOE_ROW_TILE = 2560
MOE_SUB_TILE = 256
META_E1, META_E2, META_P1, META_P2, META_R1, META_R2 = range(6)


def _router_kernel(x_ref, g_ref, w_ref, b_ref, meta_ref, cnt_ref, carry_scr):
    @pl.when(pl.program_id(0) == 0)
    def _():
        carry_scr[...] = jnp.zeros_like(carry_scr)

    h = _rms(x_ref[...], g_ref[...])
    logits = jnp.dot(h, w_ref[...], precision=lax.Precision.HIGHEST, preferred_element_type=F32) + b_ref[...]
    tb = logits.shape[0]
    lane = lax.broadcasted_iota(jnp.int32, logits.shape, 1)
    logits = jnp.where(lane < N_EXPERTS, logits, -jnp.inf)
    v1 = jnp.max(logits, axis=-1, keepdims=True)
    i1 = jnp.min(jnp.where(logits == v1, lane, LANES), axis=-1, keepdims=True)
    rest = jnp.where(lane == i1, -jnp.inf, logits)
    v2 = jnp.max(rest, axis=-1, keepdims=True)
    i2 = jnp.min(jnp.where(rest == v2, lane, LANES), axis=-1, keepdims=True)
    e2 = jnp.exp(v2 - v1)
    p1 = 1.0 / (1.0 + e2)
    p2 = e2 / (1.0 + e2)
    sel = ((lane == i1) | (lane == i2)).astype(F32)
    r_i = lax.broadcasted_iota(jnp.int32, (tb, tb), 0)
    c_i = lax.broadcasted_iota(jnp.int32, (tb, tb), 1)
    earlier = (c_i < r_i).astype(BF16)
    rank = carry_scr[0:1, :] + _dot(earlier, sel.astype(BF16))
    r1 = jnp.sum(jnp.where(lane == i1, rank, 0.0), axis=-1, keepdims=True)
    r2 = jnp.sum(jnp.where(lane == i2, rank, 0.0), axis=-1, keepdims=True)
    meta = jnp.zeros_like(logits)
    for idx, val in ((META_E1, i1.astype(F32)), (META_E2, i2.astype(F32)), (META_P1, p1), (META_P2, p2),
                     (META_R1, r1), (META_R2, r2)):
        meta = jnp.where(lane == idx, val, meta)
    meta_ref[...] = meta
    total = carry_scr[0:1, :] + jnp.sum(sel, axis=0, keepdims=True)
    carry_scr[...] = jnp.broadcast_to(total, carry_scr.shape)
    cnt_ref[...] = jnp.broadcast_to(total, cnt_ref.shape)


def _router(x, g, w_pad, b_pad, *, tm):
    t, d = x.shape
    return pl.pallas_call(
        _router_kernel,
        grid=(t // tm,),
        in_specs=[
            pl.BlockSpec((tm, d), lambda m: (m, 0)),
            pl.BlockSpec((1, d), lambda m: (0, 0)),
            pl.BlockSpec((d, LANES), lambda m: (0, 0)),
            pl.BlockSpec((1, LANES), lambda m: (0, 0)),
        ],
        out_specs=[pl.BlockSpec((tm, LANES), lambda m: (m, 0)), pl.BlockSpec((SUBLANES, LANES), lambda m: (0, 0))],
        out_shape=[jax.ShapeDtypeStruct((t, LANES), F32), jax.ShapeDtypeStruct((SUBLANES, LANES), F32)],
        scratch_shapes=[pltpu.VMEM((SUBLANES, LANES), F32)],
        compiler_params=_cparams(("arbitrary",)),
        name="moe_router",
    )(x, g, w_pad, b_pad)


def _moe_plan(meta, cnt, t):
    tile, sub = MOE_ROW_TILE, MOE_SUB_TILE
    n_tiles = 2 * t // tile + N_EXPERTS
    e1 = meta[:, META_E1].astype(jnp.int32)
    e2 = meta[:, META_E2].astype(jnp.int32)
    r1 = meta[:, META_R1].astype(jnp.int32)
    r2 = meta[:, META_R2].astype(jnp.int32)
    counts = cnt[0, :N_EXPERTS].astype(jnp.int32)
    tiles_e = (counts + tile - 1) // tile
    tile_end = jnp.cumsum(tiles_e)
    tile_start = tile_end - tiles_e
    row_start = tile_start * tile
    dest = jnp.concatenate([row_start[e1] + r1, row_start[e2] + r2]).astype(jnp.int32)
    m = jnp.arange(n_tiles, dtype=jnp.int32)
    live = m < tile_end[-1]
    m_eff = jnp.where(live, m, jnp.maximum(tile_end[-1] - 1, 0))
    tile_expert = jnp.minimum(jnp.searchsorted(tile_end, m_eff, side="right"), N_EXPERTS - 1).astype(jnp.int32)
    rows_left = counts[tile_expert] - (m_eff - tile_start[tile_expert]) * tile
    n_sub = jnp.where(live, jnp.clip((rows_left + sub - 1) // sub, 0, tile // sub), 0).astype(jnp.int32)
    sub_base = (jnp.cumsum(n_sub) - n_sub).astype(jnp.int32)
    tok = jnp.arange(t, dtype=jnp.int32)
    src_pos = sub_base[dest // tile] * sub + dest % tile
    n_src = (2 * t // sub + N_EXPERTS) * sub
    src = jnp.zeros((n_src,), jnp.int32).at[src_pos].set(jnp.concatenate([tok, tok]))
    return dest, src, tile_expert, n_sub, sub_base, n_tiles


def _expert_kernel(te_ref, ns_ref, sbase_ref, src_ref, x_hbm, g_ref, wg_ref, wu_ref, wd_ref, ys_hbm,
                   xbuf, h_scr, acc, gsem, zsem, osem):
    del te_ref
    m, f = pl.program_id(0), pl.program_id(1)
    n_m, n_f = pl.num_programs(0), pl.num_programs(1)
    tile, sub = MOE_ROW_TILE, MOE_SUB_TILE
    n_live = ns_ref[m]
    subs = [(sb, slice(sb * sub, (sb + 1) * sub)) for sb in range(tile // sub)]

    def gather_start(sb, slot):
        def start(r, carry):
            tok = src_ref[(sbase_ref[m] + sb) * sub + r]
            pltpu.make_async_copy(x_hbm.at[pl.ds(tok, 1)], xbuf.at[slot, pl.ds(r, 1)], gsem.at[slot]).start()
            return carry

        lax.fori_loop(0, sub, start, 0, unroll=8)

    def result_copy(sb, rows, tile_idx):
        return pltpu.make_async_copy(acc.at[rows], ys_hbm.at[pl.ds(tile_idx * tile + sb * sub, sub)], osem)

    def zero_copy(sb):
        return pltpu.make_async_copy(xbuf.at[0], ys_hbm.at[pl.ds(m * tile + sb * sub, sub)], zsem)

    @pl.when(f == 0)
    def _():
        @pl.when(n_live > 0)
        def _():
            gather_start(0, 0)

        for sb, rows in subs:
            slot = sb % 2
            if sb + 1 < len(subs):
                @pl.when(sb + 1 < n_live)
                def _():
                    gather_start(sb + 1, 1 - slot)

            @pl.when(sb < n_live)
            def _():
                pltpu.make_async_copy(x_hbm.at[pl.ds(0, sub)], xbuf.at[slot], gsem.at[slot]).wait()
                h_scr[rows, :] = _rms(xbuf[slot], g_ref[...]).astype(BF16)

        @pl.when(m > 0)
        def _():
            for sb, rows in subs:
                @pl.when(sb < ns_ref[m - 1])
                def _():
                    result_copy(sb, rows, m - 1).wait()

        for sb, rows in subs:
            @pl.when(sb < n_live)
            def _():
                acc[rows, :] = jnp.zeros((sub, acc.shape[1]), F32)

    @pl.when(f == 1)
    def _():
        xbuf[0] = jnp.zeros(xbuf.shape[1:], F32)
        for sb, rows in subs:
            @pl.when(sb >= n_live)
            def _():
                zero_copy(sb).start()

    for pb in range(len(subs) // 2):
        both = slice(2 * pb * sub, (2 * pb + 2) * sub)
        first = slice(2 * pb * sub, (2 * pb + 1) * sub)

        @pl.when(2 * pb + 2 <= n_live)
        def _():
            acc[both, :] += _swiglu_partial(h_scr[both, :], wg_ref, wu_ref, wd_ref)

        @pl.when(2 * pb + 1 == n_live)
        def _():
            acc[first, :] += _swiglu_partial(h_scr[first, :], wg_ref, wu_ref, wd_ref)

    @pl.when(f == n_f - 1)
    def _():
        for sb, rows in subs:
            @pl.when(sb < n_live)
            def _():
                result_copy(sb, rows, m).start()

            @pl.when(sb >= n_live)
            def _():
                zero_copy(sb).wait()

        @pl.when(m == n_m - 1)
        def _():
            for sb, rows in subs:
                @pl.when(sb < n_live)
                def _():
                    result_copy(sb, rows, m).wait()


def _experts(x, g, wg_all, wu_all, wd_all, j, tile_expert, n_sub, sub_base, src, n_tiles, *, tf):
    t, d = x.shape
    tile = MOE_ROW_TILE
    ff = wg_all.shape[-1]
    n_f = ff // tf

    def f_eff(m, f, ns):
        return jnp.where(ns[m] > 0, f, n_f - 1)

    w_in_spec = pl.BlockSpec((None, None, d, tf), lambda m, f, te, ns, sbase, sr: (j, te[m], 0, f_eff(m, f, ns)))
    w_dn_spec = pl.BlockSpec((None, None, tf, d), lambda m, f, te, ns, sbase, sr: (j, te[m], f_eff(m, f, ns), 0))
    return pl.pallas_call(
        _expert_kernel,
        grid_spec=pltpu.PrefetchScalarGridSpec(
            num_scalar_prefetch=4,
            grid=(n_tiles, n_f),
            in_specs=[
                pl.BlockSpec(memory_space=pl.ANY),
                pl.BlockSpec((1, d), lambda m, f, te, ns, sbase, sr: (0, 0)),
                w_in_spec,
                w_in_spec,
                w_dn_spec,
            ],
            out_specs=pl.BlockSpec(memory_space=pl.ANY),
            scratch_shapes=[
                pltpu.VMEM((2, MOE_SUB_TILE, d), F32),
                pltpu.VMEM((tile, d), BF16),
                pltpu.VMEM((tile, d), F32),
                pltpu.SemaphoreType.DMA((2,)),
                pltpu.SemaphoreType.DMA(()),
                pltpu.SemaphoreType.DMA(()),
            ],
        ),
        out_shape=jax.ShapeDtypeStruct((n_tiles * tile, d), F32),
        compiler_params=_cparams(("arbitrary", "arbitrary")),
        name="moe_experts",
    )(tile_expert, n_sub, sub_base, src, x, g, wg_all, wu_all, wd_all)


def _combine_kernel(dest_ref, x_ref, meta_ref, ys_hbm, o_ref, buf, sem, *, tb, t):
    i, n = pl.program_id(0), pl.num_programs(0)

    def gather_start(tile_idx, slot):
        def start(r, carry):
            for k in range(2):
                row = dest_ref[k * t + tile_idx * tb + r]
                pltpu.make_async_copy(ys_hbm.at[pl.ds(row, 1)], buf.at[slot, k, pl.ds(r, 1)], sem.at[slot]).start()
            return carry

        lax.fori_loop(0, tb, start, 0, unroll=8)

    def combine(slot):
        for k in range(2):
            pltpu.make_async_copy(ys_hbm.at[pl.ds(0, tb)], buf.at[slot, k], sem.at[slot]).wait()
        meta = meta_ref[...]
        o_ref[...] = (x_ref[...] + _lane_col(meta, META_P1) * buf[slot, 0]
                      + _lane_col(meta, META_P2) * buf[slot, 1])

    @pl.when(i == 0)
    def _():
        gather_start(0, 0)

    for slot in range(2):
        @pl.when(i % 2 == slot)
        def _():
            @pl.when(i + 1 < n)
            def _():
                gather_start(i + 1, 1 - slot)

            combine(slot)


def _combine(x, meta, ys, dest, *, tb):
    t, d = x.shape
    return pl.pallas_call(
        functools.partial(_combine_kernel, tb=tb, t=t),
        grid_spec=pltpu.PrefetchScalarGridSpec(
            num_scalar_prefetch=1,
            grid=(t // tb,),
            in_specs=[
                pl.BlockSpec((tb, d), lambda i, dst: (i, 0)),
                pl.BlockSpec((tb, LANES), lambda i, dst: (i, 0)),
                pl.BlockSpec(memory_space=pl.ANY),
            ],
            out_specs=pl.BlockSpec((tb, d), lambda i, dst: (i, 0)),
            scratch_shapes=[pltpu.VMEM((2, 2, tb, d), F32), pltpu.SemaphoreType.DMA((2,))],
        ),
        out_shape=jax.ShapeDtypeStruct((t, d), F32),
        compiler_params=_cparams(("arbitrary",)),
        name="moe_combine",
    )(dest, x, meta, ys)


def _moe(x, g, router_w, router_b, wg_all, wu_all, wd_all, j, *, tb_route, tf):
    t, d = x.shape
    w_pad = jnp.pad(router_w, ((0, 0), (0, LANES - N_EXPERTS)))
    b_pad = jnp.pad(router_b, (0, LANES - N_EXPERTS))[None, :]
    meta, cnt = _router(x, g, w_pad, b_pad, tm=tb_route)
    dest, src, tile_expert, n_sub, sub_base, n_tiles = _moe_plan(meta, cnt, t)
    ys = _experts(x, g, wg_all, wu_all, wd_all, j, tile_expert, n_sub, sub_base, src, n_tiles, tf=tf)
    return _combine(x, meta, ys, dest, tb=min(256, t))


def _ple_kernel(x_ref, g_ref, wg_ref, p_ref, wp_ref, gn_ref, o_ref, hn_ref, wgb_scr, wpb_scr):
    @pl.when(pl.program_id(0) == 0)
    def _():
        wgb_scr[...] = wg_ref[...].astype(BF16)
        wpb_scr[...] = wp_ref[...].astype(BF16)

    x = x_ref[...]
    gate = jax.nn.sigmoid(_dot(_rms(x, g_ref[...]).astype(BF16), wgb_scr[...]))
    x_new = x + gate * _dot(p_ref[...].astype(BF16), wpb_scr[...])
    o_ref[...] = x_new
    hn_ref[...] = _rms(x_new, gn_ref[...]).astype(BF16)


def _ple(x, g, wg_all, p_all, wp_all, layer, g_next, *, tm):
    t, d = x.shape
    pd = p_all.shape[-1]
    row_spec = pl.BlockSpec((tm, d), lambda m: (m, 0))
    gain_spec = pl.BlockSpec((1, d), lambda m: (0, 0))
    return pl.pallas_call(
        _ple_kernel,
        grid=(t // tm,),
        in_specs=[
            row_spec,
            gain_spec,
            _resident_weight_spec((None, d, d), (layer, 0, 0)),
            pl.BlockSpec((None, tm, pd), lambda m: (layer, m, 0)),
            _resident_weight_spec((None, pd, d), (layer, 0, 0)),
            gain_spec,
        ],
        out_specs=[row_spec, row_spec],
        out_shape=[jax.ShapeDtypeStruct((t, d), F32), jax.ShapeDtypeStruct((t, d), BF16)],
        scratch_shapes=[pltpu.VMEM((d, d), BF16), pltpu.VMEM((pd, d), BF16)],
        compiler_params=_cparams(("arbitrary",)),
        name="ple_gate",
    )(x, g, wg_all, p_all, wp_all, g_next)


def _rope_tables(seq):
    rows = seq // GRID_W
    row = jnp.broadcast_to(jnp.arange(rows, dtype=F32)[:, None], (rows, GRID_W)).reshape(seq)
    col = jnp.broadcast_to(jnp.arange(GRID_W, dtype=F32)[None, :], (rows, GRID_W)).reshape(seq)
    axis_dim = ATTN_HEAD_DIM // 2
    inv_freq = ROPE_THETA ** (-jnp.arange(0, axis_dim, 2, dtype=F32) / axis_dim)
    ar, ac = row[:, None] * inv_freq, col[:, None] * inv_freq
    cos_t = jnp.concatenate([jnp.cos(ar), jnp.cos(ar), jnp.cos(ac), jnp.cos(ac)], axis=-1)
    sin_t = jnp.concatenate([-jnp.sin(ar), jnp.sin(ar), -jnp.sin(ac), jnp.sin(ac)], axis=-1)
    return cos_t, sin_t


def kernel(x, p, norm_mix_g, w_in, conv_w, conv_b, b_igate, b_fgate, mlstm_norm_g, q_norm_g, k_norm_g, w_mlstm_up, w_attn_up, w_out, norm_ffn_g, ffn_w_gate, ffn_w_up, ffn_w_down, moe_router, moe_router_b, moe_w_gate, moe_w_up, moe_w_down, norm_ple_g, w_ple_gate, w_ple_proj):
    b, s, d = x.shape
    depth = w_in.shape[0]
    t = b * s
    col = _col_layout(d)
    cos_t, sin_t = _rope_tables(s)
    xt = x.reshape(t, d)
    p_all = p.reshape(depth, t, p.shape[-1])
    w_in_t = jnp.swapaxes(w_in, 1, 2)

    tm_big, tm_half = min(1024, t), min(512, t)
    tn = min(512, d)
    tf = 256

    h_mix = _rms_cast(xt, norm_mix_g[0][None, :], tm=tm_half)
    for i in range(depth):
        z, gates = _in_proj(h_mix, w_in_t, i, tm=min(2048, t), tn=tn)
        z3 = z.reshape(b, s, col["total"])
        qk3 = _conv_silu(z3, conv_w[i], conv_b[i][None, :], col0=col["qk"], tc=256)
        gate_bias = jnp.concatenate(
            [b_igate[i].reshape(-1), b_fgate[i].reshape(-1), jnp.zeros((LANES - N_GATE_COLS,), F32)])[None, :]
        hf, hb = _mlstm_scan(qk3, z3, gates.reshape(b, s, LANES), gate_bias, v_col0=col["mv"])
        q3, k3, vt3 = _qk_prep(z3, cos_t, sin_t, q_norm_g[i][None, :], k_norm_g[i][None, :],
                               aq_col0=col["aq"], ak_col0=col["ak"], av_col0=col["av"], ts=min(512, s))
        ha = _flash_gqa(q3, k3, vt3, _score_bound(q_norm_g[i], k_norm_g[i]), tq=min(256, s), tk=min(2048, s))
        hm = _mlstm_out(hf.reshape(t, MLSTM_WIDTH), hb.reshape(t, MLSTM_WIDTH), z,
                        mlstm_norm_g[i].reshape(1, MLSTM_WIDTH), mo_col0=col["mo"], tm=tm_half)
        y = _merge(hm, ha.reshape(t, ATTN_WIDTH), z, w_mlstm_up, w_attn_up, i,
                   gm_col0=col["gm"], ga_col0=col["ga"], tm=tm_half)
        xt = _out_proj(y, w_out, i, xt, tm=tm_half)

        j = i // 2
        g_ffn = norm_ffn_g[i][None, :]
        if i % 2 == 0:
            xt = _ffn(xt, g_ffn, ffn_w_gate, ffn_w_up, ffn_w_down, j, tm=tm_big, tf=tf)
        else:
            xt = _moe(xt, g_ffn, moe_router[j], moe_router_b[j], moe_w_gate, moe_w_up, moe_w_down, j,
                      tb_route=tm_half, tf=tf)

        g_next = norm_mix_g[min(i + 1, depth - 1)][None, :]
        xt, h_mix = _ple(xt, norm_ple_g[i][None, :], w_ple_gate, p_all, w_ple_proj, i, g_next, tm=tm_half)

    return xt.reshape(b, s, d)
```

```python
import functools
import itertools

import jax
import jax.numpy as jnp
from jax import lax
from jax.experimental import pallas as pl
from jax.experimental.pallas import tpu as pltpu

F32 = jnp.float32
BF16 = jnp.bfloat16

GRID_W = 64
N_MLSTM_HEADS = 4
MLSTM_QK_DIM = 128
MLSTM_V_DIM = 256
MLSTM_QK_WIDTH = N_MLSTM_HEADS * MLSTM_QK_DIM
MLSTM_WIDTH = N_MLSTM_HEADS * MLSTM_V_DIM
MLSTM_CHUNK = 256
N_DIRS = 2
N_ATTN_HEADS = 8
N_KV_HEADS = 2
ATTN_HEAD_DIM = 128
ATTN_WIDTH = N_ATTN_HEADS * ATTN_HEAD_DIM
KV_WIDTH = N_KV_HEADS * ATTN_HEAD_DIM
ROPE_THETA = 10000.0
N_EXPERTS = 8
EPS = 1e-6

LANES = 128
SUBLANES = 8
VMEM_LIMIT_BYTES = 56 * 2**20

N_GATE_COLS = 2 * N_DIRS * N_MLSTM_HEADS


def _col_layout(d_model):
    names = ("qk", "mv", "mo", "aq", "gm", "ga", "ak", "av")
    widths = (2 * MLSTM_QK_WIDTH, MLSTM_WIDTH, MLSTM_WIDTH, ATTN_WIDTH, d_model, d_model, KV_WIDTH, KV_WIDTH)
    off, out = 0, {}
    for n, w in zip(names, widths):
        out[n] = off
        off += w
    out["total"] = off
    return out


N_HEAD_COLS = 2 * MLSTM_QK_WIDTH + MLSTM_WIDTH


def _cparams(semantics):
    return pltpu.CompilerParams(dimension_semantics=semantics, vmem_limit_bytes=VMEM_LIMIT_BYTES)


def _rms(x, g):
    ms = jnp.mean(x * x, axis=-1, keepdims=True)
    return x * lax.rsqrt(ms + EPS) * g


def _dot(a, b):
    return jnp.dot(a, b, preferred_element_type=F32)


def _dot_nt(a, b):
    return lax.dot_general(a, b, (((1,), (1,)), ((), ())), preferred_element_type=F32)


def _dot_tn(a, b):
    return lax.dot_general(a, b, (((0,), (0,)), ((), ())), preferred_element_type=F32)


def _rms_cast_kernel(x_ref, g_ref, h_ref):
    h_ref[...] = _rms(x_ref[...], g_ref[...]).astype(BF16)


def _rms_cast(x, g, *, tm):
    t, d = x.shape
    row_spec = pl.BlockSpec((tm, d), lambda m: (m, 0))
    return pl.pallas_call(
        _rms_cast_kernel,
        grid=(t // tm,),
        in_specs=[row_spec, pl.BlockSpec((1, d), lambda m: (0, 0))],
        out_specs=row_spec,
        out_shape=jax.ShapeDtypeStruct((t, d), BF16),
        compiler_params=_cparams(("parallel",)),
        name="input_rms_norm",
    )(x, g)


def _in_proj_kernel(h_ref, wm_ref, wn_ref, wg_ref, o_ref, gates_ref, *, head_blocks, tail_blocks):
    j = pl.program_id(1)
    ng = N_GATE_COLS

    @pl.when(j < head_blocks)
    def _():
        o_ref[...] = _dot_nt(h_ref[...], wm_ref[...].astype(BF16)).astype(o_ref.dtype)

    @pl.when((j >= head_blocks) & (j < head_blocks + tail_blocks))
    def _():
        w = jnp.concatenate([wm_ref[ng:, :], wn_ref[...]], axis=0).astype(BF16)
        o_ref[...] = _dot_nt(h_ref[...], w).astype(o_ref.dtype)

    @pl.when(j == head_blocks + tail_blocks)
    def _():
        pad = jnp.zeros((gates_ref.shape[1] - ng, wg_ref.shape[1]), F32)
        w = jnp.concatenate([wg_ref[...], pad], axis=0).astype(BF16)
        gates_ref[...] = _dot_nt(h_ref[...], w)


def _in_proj(h, wt_all, layer, *, tm, tn):
    t, d = h.shape
    n_cols = wt_all.shape[1]
    ng = N_GATE_COLS
    tail = n_cols - N_HEAD_COLS - ng
    head_blocks, tail_blocks = N_HEAD_COLS // tn, tail // tn
    assert tail % tn == 0 and N_HEAD_COLS % tn == 0 and n_cols % ng == 0
    last_main = head_blocks + tail_blocks - 1
    pre, kv, gates2 = (MLSTM_WIDTH + ATTN_WIDTH) // tn, 2 * KV_WIDTH // tn, 2 * d // tn
    assert (2 * KV_WIDTH) % tn == 0 and pre + kv + gates2 == tail_blocks

    def out_block(j):
        jz = jnp.minimum(j, last_main)
        tb = jz - head_blocks
        moved = jnp.where(tb < pre, tb, jnp.where(tb < pre + kv, tb + gates2, tb - kv))
        return jnp.where(tb >= 0, head_blocks + moved, jz)

    return pl.pallas_call(
        functools.partial(_in_proj_kernel, head_blocks=head_blocks, tail_blocks=tail_blocks),
        grid=(t // tm, head_blocks + tail_blocks + 1),
        in_specs=[
            pl.BlockSpec((tm, d), lambda m, j: (m, 0)),
            pl.BlockSpec((None, tn, d), lambda m, j: (layer, jnp.minimum(j, last_main), 0)),
            pl.BlockSpec((None, ng, d), lambda m, j: (layer, jnp.minimum((j + 1) * (tn // ng), n_cols // ng - 1), 0)),
            pl.BlockSpec((None, ng, d), lambda m, j: (layer, N_HEAD_COLS // ng, 0)),
        ],
        out_specs=[pl.BlockSpec((tm, tn), lambda m, j: (m, out_block(j))),
                   pl.BlockSpec((tm, LANES), lambda m, j: (m, 0))],
        out_shape=[jax.ShapeDtypeStruct((t, N_HEAD_COLS + tail), BF16), jax.ShapeDtypeStruct((t, LANES), F32)],
        compiler_params=_cparams(("parallel", "arbitrary")),
        name="in_proj",
    )(h, wt_all, wt_all, wt_all)


def _conv_silu_kernel(z_ref, w_ref, b_ref, o_ref, *, q_blocks, q_scale):
    x = z_ref[0].astype(F32)
    s = x.shape[0]
    row = lax.broadcasted_iota(jnp.int32, x.shape, 0)
    prev = jnp.where(row == 0, 0.0, pltpu.roll(x, 1, 0))
    nxt = jnp.where(row == s - 1, 0.0, pltpu.roll(x, s - 1, 0))
    y = prev * w_ref[0:1, :] + x * w_ref[1:2, :] + nxt * w_ref[2:3, :] + b_ref[...]
    y = y * jax.nn.sigmoid(y)
    scale = jnp.where(pl.program_id(1) < q_blocks, q_scale, 1.0).astype(F32)
    o_ref[0] = (y * scale).astype(o_ref.dtype)


def _conv_silu(z3, conv_w, conv_b, *, col0, tc):
    b, s, _ = z3.shape
    width = conv_w.shape[1]
    kern = functools.partial(_conv_silu_kernel, q_blocks=MLSTM_QK_WIDTH // tc, q_scale=MLSTM_QK_DIM ** -0.5)
    return pl.pallas_call(
        kern,
        grid=(b, width // tc),
        in_specs=[
            pl.BlockSpec((1, s, tc), lambda i, j: (i, 0, col0 // tc + j)),
            pl.BlockSpec((3, tc), lambda i, j: (0, j)),
            pl.BlockSpec((1, tc), lambda i, j: (0, j)),
        ],
        out_specs=pl.BlockSpec((1, s, tc), lambda i, j: (i, 0, j)),
        out_shape=jax.ShapeDtypeStruct((b, s, width), BF16),
        compiler_params=_cparams(("parallel", "parallel")),
        name="mlstm_conv_silu",
    )(z3, conv_w, conv_b)


def _log_sigmoid(x):
    return jnp.minimum(x, 0.0) - jnp.log1p(jnp.exp(-jnp.abs(x)))


def _lane_col(x, idx):
    lane = lax.broadcasted_iota(jnp.int32, x.shape, 1)
    return jnp.sum(jnp.where(lane == idx, x, 0.0), axis=-1, keepdims=True)


def _mlstm_kernel(qkf_ref, qkb_ref, vf_ref, vb_ref, gf_ref, gb_ref, bias_ref, hf_ref, hb_ref, c_scr, m_scr):
    L = MLSTM_CHUNK
    dk, dv, nh = MLSTM_QK_DIM, MLSTM_V_DIM, N_MLSTM_HEADS

    @pl.when(pl.program_id(0) == 0)
    def _():
        c_scr[...] = jnp.zeros_like(c_scr)
        m_scr[...] = jnp.zeros_like(m_scr)

    r_i = lax.broadcasted_iota(jnp.int32, (L, L), 0)
    c_i = lax.broadcasted_iota(jnp.int32, (L, L), 1)
    lane = lax.broadcasted_iota(jnp.int32, (L, LANES), 1)
    ones_blk = jnp.where(lane == 0, 1.0, 0.0).astype(F32)

    dirs = ((qkf_ref, vf_ref, gf_ref, hf_ref), (qkb_ref, vb_ref, gb_ref, hb_ref))
    for bi, (d, (qk_ref, v_ref, g_ref, h_ref)) in itertools.product(range(qkf_ref.shape[0]), enumerate(dirs)):
        mask = (c_i <= r_i) if d == 0 else (c_i >= r_i)
        g = g_ref[bi] + bias_ref[...]
        gp = jnp.where(lane < N_GATE_COLS // 2, g, _log_sigmoid(g))
        gp_t = gp.T
        bc = jnp.dot(mask.astype(F32), gp, precision=lax.Precision.HIGHEST, preferred_element_type=F32)
        bc_t = bc.T
        end_row = L - 1 if d == 0 else 0
        for h in range(nh):
            ci = (bi * N_DIRS + d) * nh + h
            ch_i, ch_f = d * nh + h, N_GATE_COLS // 2 + d * nh + h
            i_row = gp_t[ch_i:ch_i + 1, :]
            bc_row = bc_t[ch_f:ch_f + 1, :]
            i_col = _lane_col(gp, ch_i)
            bc_col = _lane_col(bc, ch_f)
            m_prev = m_scr[ci, 0:1, 0:1]
            c_prev = c_scr[ci]

            a_col = bc_col + m_prev
            dm = jnp.where(mask, bc_col - bc_row + i_row, -jnp.inf)
            m_t = jnp.maximum(a_col, jnp.max(dm, axis=-1, keepdims=True))
            w_inter = jnp.exp(a_col - m_t)
            w_intra = jnp.exp(dm - m_t)

            q = qk_ref[bi, :, h * dk:(h + 1) * dk]
            k = qk_ref[bi, :, MLSTM_QK_WIDTH + h * dk:MLSTM_QK_WIDTH + (h + 1) * dk]
            v = v_ref[bi, :, h * dv:(h + 1) * dv].astype(F32)
            v_ext = jnp.concatenate([v, ones_blk], axis=1)
            p = (_dot_nt(q, k) * w_intra).astype(BF16)
            num_ext = w_inter * _dot(q, c_prev.astype(BF16)) + _dot(p, v_ext.astype(BF16))
            den = num_ext[:, dv:dv + 1]
            h_out = num_ext[:, :dv] / jnp.maximum(jnp.abs(den), jnp.exp(-m_t))
            h_ref[bi, :, h * dv:(h + 1) * dv] = h_out.astype(h_ref.dtype)

            b_end = bc_col[end_row:end_row + 1, :]
            g_col = b_end - bc_col + i_col
            m_new = jnp.maximum(b_end + m_prev, jnp.max(g_col, axis=0, keepdims=True))
            decay = jnp.exp(b_end + m_prev - m_new)
            ws = jnp.exp(g_col - m_new)
            c_scr[ci] = decay * c_prev + _dot_tn(k, (ws * v_ext).astype(BF16))
            m_scr[ci] = jnp.broadcast_to(m_new, m_scr.shape[1:])


def _mlstm_scan(qk3, z3, gates3, gate_bias, *, v_col0):
    b, s, _ = qk3.shape
    L = MLSTM_CHUNK
    assert s % L == 0
    nc = s // L
    n_chain = b * N_DIRS * N_MLSTM_HEADS
    vb = v_col0 // MLSTM_WIDTH
    qk_w = 2 * MLSTM_QK_WIDTH
    return pl.pallas_call(
        _mlstm_kernel,
        grid=(nc,),
        in_specs=[
            pl.BlockSpec((b, L, qk_w), lambda c: (0, c, 0)),
            pl.BlockSpec((b, L, qk_w), lambda c: (0, nc - 1 - c, 0)),
            pl.BlockSpec((b, L, MLSTM_WIDTH), lambda c: (0, c, vb)),
            pl.BlockSpec((b, L, MLSTM_WIDTH), lambda c: (0, nc - 1 - c, vb)),
            pl.BlockSpec((b, L, LANES), lambda c: (0, c, 0)),
            pl.BlockSpec((b, L, LANES), lambda c: (0, nc - 1 - c, 0)),
            pl.BlockSpec((1, LANES), lambda c: (0, 0)),
        ],
        out_specs=[
            pl.BlockSpec((b, L, MLSTM_WIDTH), lambda c: (0, c, 0)),
            pl.BlockSpec((b, L, MLSTM_WIDTH), lambda c: (0, nc - 1 - c, 0)),
        ],
        out_shape=[jax.ShapeDtypeStruct((b, s, MLSTM_WIDTH), BF16)] * 2,
        scratch_shapes=[
            pltpu.VMEM((n_chain, MLSTM_QK_DIM, MLSTM_V_DIM + LANES), F32),
            pltpu.VMEM((n_chain, SUBLANES, LANES), F32),
        ],
        compiler_params=_cparams(("arbitrary",)),
        name="mlstm_scan",
    )(qk3, qk3, z3, z3, gates3, gates3, gate_bias)


LOG2_E = 1.4426950408889634


def _qk_prep_kernel(aq_ref, ak_ref, av_ref, cos_ref, sin_ref, gq_ref, gk_ref, q_ref, k_ref, vt_ref):
    cos, sin = cos_ref[...], sin_ref[...]
    hd = ATTN_HEAD_DIM
    lane = lax.broadcasted_iota(jnp.int32, cos.shape, 1)
    first_half = (lane % (hd // 2)) < (hd // 4)

    def prep(x, g, scale):
        n = _rms(x.astype(F32), g)
        partner = jnp.where(first_half, pltpu.roll(n, hd - hd // 4, 1), pltpu.roll(n, hd // 4, 1))
        return ((n * cos + partner * sin) * scale).astype(BF16)

    for h in range(N_ATTN_HEADS):
        q_ref[0, :, h * hd:(h + 1) * hd] = prep(aq_ref[0, :, h * hd:(h + 1) * hd], gq_ref[...], hd ** -0.5 * LOG2_E)
    for h in range(N_KV_HEADS):
        k_ref[0, :, h * hd:(h + 1) * hd] = prep(ak_ref[0, :, h * hd:(h + 1) * hd], gk_ref[...], 1.0)
    vt_ref[0] = av_ref[0].astype(F32).T.astype(BF16)


def _qk_prep(z3, cos_t, sin_t, gq, gk, *, aq_col0, ak_col0, av_col0, ts):
    b, s, _ = z3.shape
    return pl.pallas_call(
        _qk_prep_kernel,
        grid=(b, s // ts),
        in_specs=[
            pl.BlockSpec((1, ts, ATTN_WIDTH), lambda i, j: (i, j, aq_col0 // ATTN_WIDTH)),
            pl.BlockSpec((1, ts, KV_WIDTH), lambda i, j: (i, j, ak_col0 // KV_WIDTH)),
            pl.BlockSpec((1, ts, KV_WIDTH), lambda i, j: (i, j, av_col0 // KV_WIDTH)),
            pl.BlockSpec((ts, ATTN_HEAD_DIM), lambda i, j: (j, 0)),
            pl.BlockSpec((ts, ATTN_HEAD_DIM), lambda i, j: (j, 0)),
            pl.BlockSpec((1, ATTN_HEAD_DIM), lambda i, j: (0, 0)),
            pl.BlockSpec((1, ATTN_HEAD_DIM), lambda i, j: (0, 0)),
        ],
        out_specs=[
            pl.BlockSpec((1, ts, ATTN_WIDTH), lambda i, j: (i, j, 0)),
            pl.BlockSpec((1, ts, KV_WIDTH), lambda i, j: (i, j, 0)),
            pl.BlockSpec((1, KV_WIDTH, ts), lambda i, j: (i, 0, j)),
        ],
        out_shape=[jax.ShapeDtypeStruct((b, s, ATTN_WIDTH), BF16), jax.ShapeDtypeStruct((b, s, KV_WIDTH), BF16),
                   jax.ShapeDtypeStruct((b, KV_WIDTH, s), BF16)],
        compiler_params=_cparams(("parallel", "parallel")),
        name="attn_qk_norm_rope",
    )(z3, z3, z3, cos_t, sin_t, gq, gk)


SOFTMAX_SHIFT_LIMIT = 60.0


def _flash_kernel(bound_ref, q_ref, k_ref, vt_ref, o_ref, *, grp, tk):
    hd = ATTN_HEAD_DIM
    tq = q_ref.shape[1]
    s_len = k_ref.shape[1]
    rows = grp * tq
    bound = bound_ref[0, 0]
    q = jnp.concatenate([q_ref[0, :, g * hd:(g + 1) * hd] for g in range(grp)], axis=0)
    chunks = [slice(c * tk, (c + 1) * tk) for c in range(s_len // tk)]

    def store(acc, l):
        o = (acc / l).T
        for g in range(grp):
            o_ref[0, :, g * hd:(g + 1) * hd] = o[g * tq:(g + 1) * tq, :].astype(o_ref.dtype)

    @pl.when(bound <= SOFTMAX_SHIFT_LIMIT)
    def _():
        l = jnp.zeros((1, rows), F32)
        acc = jnp.zeros((hd, rows), F32)
        for ck in chunks:
            p = jnp.exp2(_dot_nt(k_ref[0, ck, :], q) - bound)
            l = l + jnp.sum(p, axis=0, keepdims=True)
            acc = acc + _dot(vt_ref[0, :, ck], p.astype(BF16))
        store(acc, l)

    @pl.when(bound > SOFTMAX_SHIFT_LIMIT)
    def _():
        m = jnp.full((1, rows), -jnp.inf, F32)
        l = jnp.zeros((1, rows), F32)
        acc = jnp.zeros((hd, rows), F32)
        for ck in chunks:
            st = _dot_nt(k_ref[0, ck, :], q)
            m_new = jnp.maximum(m, jnp.max(st, axis=0, keepdims=True))
            alpha = jnp.exp2(m - m_new)
            p = jnp.exp2(st - m_new)
            l = alpha * l + jnp.sum(p, axis=0, keepdims=True)
            acc = alpha * acc + _dot(vt_ref[0, :, ck], p.astype(BF16))
            m = m_new
        store(acc, l)


def _score_bound(gq, gk):
    scale = ATTN_HEAD_DIM ** -0.5 * LOG2_E
    return (1.02 * ATTN_HEAD_DIM * scale * jnp.max(jnp.abs(gq)) * jnp.max(jnp.abs(gk))).reshape(1, 1).astype(F32)


def _flash_gqa(q3, k3, vt3, bound, *, tq, tk):
    b, s, _ = q3.shape
    hd = ATTN_HEAD_DIM
    grp = N_ATTN_HEADS // N_KV_HEADS
    return pl.pallas_call(
        functools.partial(_flash_kernel, grp=grp, tk=tk),
        grid=(b, N_KV_HEADS, s // tq),
        in_specs=[
            pl.BlockSpec(memory_space=pltpu.SMEM),
            pl.BlockSpec((1, tq, grp * hd), lambda i, h, qi: (i, qi, h)),
            pl.BlockSpec((1, s, hd), lambda i, h, qi: (i, 0, h)),
            pl.BlockSpec((1, hd, s), lambda i, h, qi: (i, h, 0)),
        ],
        out_specs=pl.BlockSpec((1, tq, grp * hd), lambda i, h, qi: (i, qi, h)),
        out_shape=jax.ShapeDtypeStruct((b, s, ATTN_WIDTH), BF16),
        compiler_params=_cparams(("parallel", "parallel", "parallel")),
        name="gqa_flash",
    )(bound, q3, k3, vt3)


def _mlstm_out_kernel(hf_ref, hb_ref, mo_ref, ng_ref, hm_ref):
    dv = MLSTM_V_DIM
    for h in range(N_MLSTM_HEADS):
        sl = slice(h * dv, (h + 1) * dv)
        hm = _rms(hf_ref[:, sl].astype(F32) + hb_ref[:, sl].astype(F32), ng_ref[:, sl])
        hm_ref[:, sl] = (hm * jax.nn.sigmoid(mo_ref[:, sl].astype(F32))).astype(BF16)


def _mlstm_out(hf, hb, z, ng, *, mo_col0, tm):
    t, w = hf.shape
    row_spec = pl.BlockSpec((tm, w), lambda m: (m, 0))
    return pl.pallas_call(
        _mlstm_out_kernel,
        grid=(t // tm,),
        in_specs=[row_spec, row_spec, pl.BlockSpec((tm, w), lambda m: (m, mo_col0 // w)),
                  pl.BlockSpec((1, w), lambda m: (0, 0))],
        out_specs=row_spec,
        out_shape=jax.ShapeDtypeStruct((t, w), BF16),
        compiler_params=_cparams(("parallel",)),
        name="mlstm_out_norm",
    )(hf, hb, z, ng)


def _merge_kernel(hm_ref, ha_ref, gm_ref, ga_ref, wm_ref, wa_ref, y_ref, wmb_scr, wab_scr):
    @pl.when(pl.program_id(0) == 0)
    def _():
        wmb_scr[...] = wm_ref[...].astype(BF16)
        wab_scr[...] = wa_ref[...].astype(BF16)

    ym = _dot(hm_ref[...], wmb_scr[...])
    ya = _dot(ha_ref[...], wab_scr[...])
    gate_m = jax.nn.sigmoid(gm_ref[...].astype(F32))
    gate_a = jax.nn.sigmoid(ga_ref[...].astype(F32))
    y_ref[...] = (gate_m * ym + gate_a * ya).astype(y_ref.dtype)


def _merge(hm, ha, z, wm_all, wa_all, layer, *, gm_col0, ga_col0, tm):
    t, w = hm.shape
    d = wm_all.shape[2]
    act_spec = pl.BlockSpec((tm, w), lambda m: (m, 0))
    w_spec = _resident_weight_spec((None, w, d), (layer, 0, 0))
    return pl.pallas_call(
        _merge_kernel,
        grid=(t // tm,),
        in_specs=[
            act_spec,
            act_spec,
            pl.BlockSpec((tm, d), lambda m: (m, gm_col0 // d)),
            pl.BlockSpec((tm, d), lambda m: (m, ga_col0 // d)),
            w_spec,
            w_spec,
        ],
        out_specs=pl.BlockSpec((tm, d), lambda m: (m, 0)),
        out_shape=jax.ShapeDtypeStruct((t, d), BF16),
        scratch_shapes=[pltpu.VMEM((w, d), BF16), pltpu.VMEM((w, d), BF16)],
        compiler_params=_cparams(("arbitrary",)),
        name="branch_merge",
    )(hm, ha, z, z, wm_all, wa_all)


def _resident_weight_spec(shape, index):
    return pl.BlockSpec(shape, lambda *_: index, pipeline_mode=pl.Buffered(1))


def _out_proj_kernel(y_ref, w_ref, x_ref, o_ref, wb_scr):
    @pl.when(pl.program_id(0) == 0)
    def _():
        wb_scr[...] = w_ref[...].astype(BF16)

    o_ref[...] = x_ref[...] + _dot(y_ref[...], wb_scr[...])


def _out_proj(y, w_all, layer, x, *, tm):
    t, d = x.shape
    k = y.shape[1]
    return pl.pallas_call(
        _out_proj_kernel,
        grid=(t // tm,),
        in_specs=[
            pl.BlockSpec((tm, k), lambda m: (m, 0)),
            _resident_weight_spec((None, k, d), (layer, 0, 0)),
            pl.BlockSpec((tm, d), lambda m: (m, 0)),
        ],
        out_specs=pl.BlockSpec((tm, d), lambda m: (m, 0)),
        out_shape=jax.ShapeDtypeStruct((t, d), F32),
        scratch_shapes=[pltpu.VMEM((k, d), BF16)],
        compiler_params=_cparams(("arbitrary",)),
        name="out_proj_residual",
    )(y, w_all, x)


def _swiglu_partial(h, wg_ref, wu_ref, wd_ref):
    a = _dot(h, wg_ref[...].astype(BF16))
    u = _dot(h, wu_ref[...].astype(BF16))
    t = a * jax.nn.sigmoid(a) * u
    return _dot(t.astype(BF16), wd_ref[...].astype(BF16))


def _ffn_kernel(x_ref, g_ref, wg_ref, wu_ref, wd_ref, o_ref, h_scr):
    @pl.when(pl.program_id(1) == 0)
    def _():
        x = x_ref[...]
        h_scr[...] = _rms(x, g_ref[...]).astype(BF16)
        o_ref[...] = x

    o_ref[...] += _swiglu_partial(h_scr[...], wg_ref, wu_ref, wd_ref)


def _ffn(x, g, wg_all, wu_all, wd_all, j, *, tm, tf):
    t, d = x.shape
    ff = wg_all.shape[-1]
    w_in_spec = pl.BlockSpec((None, d, tf), lambda m, f: (j, 0, f))
    return pl.pallas_call(
        _ffn_kernel,
        grid=(t // tm, ff // tf),
        in_specs=[
            pl.BlockSpec((tm, d), lambda m, f: (m, 0)),
            pl.BlockSpec((1, d), lambda m, f: (0, 0)),
            w_in_spec,
            w_in_spec,
            pl.BlockSpec((None, tf, d), lambda m, f: (j, f, 0)),
        ],
        out_specs=pl.BlockSpec((tm, d), lambda m, f: (m, 0)),
        out_shape=jax.ShapeDtypeStruct((t, d), F32),
        scratch_shapes=[pltpu.VMEM((tm, d), BF16)],
        compiler_params=_cparams(("parallel", "arbitrary")),
        name="dense_swiglu",
    )(x, g, wg_all, wu_all, wd_all)


MOE_ROW_TILE = 2304
MOE_SUB_TILE = 256
META_E1, META_E2, META_P1, META_P2, META_R1, META_R2 = range(6)


def _router_kernel(x_ref, g_ref, w_ref, b_ref, meta_ref, cnt_ref, carry_scr):
    @pl.when(pl.program_id(0) == 0)
    def _():
        carry_scr[...] = jnp.zeros_like(carry_scr)

    h = _rms(x_ref[...], g_ref[...])
    logits = jnp.dot(h, w_ref[...], precision=lax.Precision.HIGHEST, preferred_element_type=F32) + b_ref[...]
    tb = logits.shape[0]
    lane = lax.broadcasted_iota(jnp.int32, logits.shape, 1)
    logits = jnp.where(lane < N_EXPERTS, logits, -jnp.inf)
    v1 = jnp.max(logits, axis=-1, keepdims=True)
    i1 = jnp.min(jnp.where(logits == v1, lane, LANES), axis=-1, keepdims=True)
    rest = jnp.where(lane == i1, -jnp.inf, logits)
    v2 = jnp.max(rest, axis=-1, keepdims=True)
    i2 = jnp.min(jnp.where(rest == v2, lane, LANES), axis=-1, keepdims=True)
    e2 = jnp.exp(v2 - v1)
    p1 = 1.0 / (1.0 + e2)
    p2 = e2 / (1.0 + e2)
    sel = ((lane == i1) | (lane == i2)).astype(F32)
    r_i = lax.broadcasted_iota(jnp.int32, (tb, tb), 0)
    c_i = lax.broadcasted_iota(jnp.int32, (tb, tb), 1)
    earlier = (c_i < r_i).astype(BF16)
    rank = carry_scr[0:1, :] + _dot(earlier, sel.astype(BF16))
    r1 = jnp.sum(jnp.where(lane == i1, rank, 0.0), axis=-1, keepdims=True)
    r2 = jnp.sum(jnp.where(lane == i2, rank, 0.0), axis=-1, keepdims=True)
    meta = jnp.zeros_like(logits)
    for idx, val in ((META_E1, i1.astype(F32)), (META_E2, i2.astype(F32)), (META_P1, p1), (META_P2, p2),
                     (META_R1, r1), (META_R2, r2)):
        meta = jnp.where(lane == idx, val, meta)
    meta_ref[...] = meta
    total = carry_scr[0:1, :] + jnp.sum(sel, axis=0, keepdims=True)
    carry_scr[...] = jnp.broadcast_to(total, carry_scr.shape)
    cnt_ref[...] = jnp.broadcast_to(total, cnt_ref.shape)


def _router(x, g, w_pad, b_pad, *, tm):
    t, d = x.shape
    return pl.pallas_call(
        _router_kernel,
        grid=(t // tm,),
        in_specs=[
            pl.BlockSpec((tm, d), lambda m: (m, 0)),
            pl.BlockSpec((1, d), lambda m: (0, 0)),
            pl.BlockSpec((d, LANES), lambda m: (0, 0)),
            pl.BlockSpec((1, LANES), lambda m: (0, 0)),
        ],
        out_specs=[pl.BlockSpec((tm, LANES), lambda m: (m, 0)), pl.BlockSpec((SUBLANES, LANES), lambda m: (0, 0))],
        out_shape=[jax.ShapeDtypeStruct((t, LANES), F32), jax.ShapeDtypeStruct((SUBLANES, LANES), F32)],
        scratch_shapes=[pltpu.VMEM((SUBLANES, LANES), F32)],
        compiler_params=_cparams(("arbitrary",)),
        name="moe_router",
    )(x, g, w_pad, b_pad)


def _moe_plan(meta, cnt, t):
    tile, sub = MOE_ROW_TILE, MOE_SUB_TILE
    n_tiles = 2 * t // tile + N_EXPERTS
    e1 = meta[:, META_E1].astype(jnp.int32)
    e2 = meta[:, META_E2].astype(jnp.int32)
    r1 = meta[:, META_R1].astype(jnp.int32)
    r2 = meta[:, META_R2].astype(jnp.int32)
    counts = cnt[0, :N_EXPERTS].astype(jnp.int32)
    tiles_e = (counts + tile - 1) // tile
    tile_end = jnp.cumsum(tiles_e)
    tile_start = tile_end - tiles_e
    row_start = tile_start * tile
    dest = jnp.concatenate([row_start[e1] + r1, row_start[e2] + r2]).astype(jnp.int32)
    m = jnp.arange(n_tiles, dtype=jnp.int32)
    live = m < tile_end[-1]
    m_eff = jnp.where(live, m, jnp.maximum(tile_end[-1] - 1, 0))
    tile_expert = jnp.minimum(jnp.searchsorted(tile_end, m_eff, side="right"), N_EXPERTS - 1).astype(jnp.int32)
    rows_left = counts[tile_expert] - (m_eff - tile_start[tile_expert]) * tile
    n_sub = jnp.where(live, jnp.clip((rows_left + sub - 1) // sub, 0, tile // sub), 0).astype(jnp.int32)
    sub_base = (jnp.cumsum(n_sub) - n_sub).astype(jnp.int32)
    tok = jnp.arange(t, dtype=jnp.int32)
    src_pos = sub_base[dest // tile] * sub + dest % tile
    n_src = (2 * t // sub + N_EXPERTS) * sub
    src = jnp.zeros((n_src,), jnp.int32).at[src_pos].set(jnp.concatenate([tok, tok]))
    return dest, src, tile_expert, n_sub, sub_base, n_tiles


def _expert_kernel(te_ref, ns_ref, sbase_ref, src_ref, x_hbm, g_ref, wg_ref, wu_ref, wd_ref, ys_hbm,
                   xbuf, h_scr, acc, gsem, zsem, osem):
    del te_ref
    m, f = pl.program_id(0), pl.program_id(1)
    n_m, n_f = pl.num_programs(0), pl.num_programs(1)
    tile, sub = MOE_ROW_TILE, MOE_SUB_TILE
    n_live = ns_ref[m]
    cur = m % 2
    nxt = jnp.minimum(m + 1, n_m - 1)
    n_next = jnp.where(m + 1 < n_m, ns_ref[nxt], 0)
    subs = [(sb, slice(sb * sub, (sb + 1) * sub)) for sb in range(tile // sub)]

    def row_copy(tile_idx, sb, r):
        tok = src_ref[(sbase_ref[tile_idx] + sb) * sub + r]
        slot = sb % 2
        return pltpu.make_async_copy(x_hbm.at[pl.ds(tok, 1)], xbuf.at[slot, pl.ds(r, 1)], gsem.at[slot])

    def land(sb, h_slot):
        slot = sb % 2
        pltpu.make_async_copy(x_hbm.at[pl.ds(0, sub)], xbuf.at[slot], gsem.at[slot]).wait()
        start = sb * sub if isinstance(sb, int) else pl.multiple_of(sb * sub, sub)
        h_scr[h_slot, pl.ds(start, sub), :] = _rms(xbuf[slot], g_ref[...]).astype(BF16)

    def result_copy(sb, rows, tile_idx):
        return pltpu.make_async_copy(acc.at[rows], ys_hbm.at[pl.ds(tile_idx * tile + sb * sub, sub)], osem)

    def zero_copy(sb, rows):
        return pltpu.make_async_copy(acc.at[rows], ys_hbm.at[pl.ds(m * tile + sb * sub, sub)], zsem)

    @pl.when((m == 0) & (f == 0))
    def _():
        def start_all(sb):
            def start(r, carry):
                row_copy(0, sb, r).start()
                return carry

            lax.fori_loop(0, sub, start, 0, unroll=8)

        @pl.when(n_live > 0)
        def _():
            start_all(0)

        for sb, _ in subs:
            if sb + 1 < len(subs):
                @pl.when(sb + 1 < n_live)
                def _():
                    start_all(sb + 1)

            @pl.when(sb < n_live)
            def _():
                land(sb, 0)

    @pl.when((f >= 2) & (f - 2 < n_next))
    def _():
        land(f - 2, 1 - cur)

    @pl.when(f == 0)
    def _():
        @pl.when(m > 0)
        def _():
            for sb, rows in subs:
                @pl.when(sb < ns_ref[m - 1])
                def _():
                    result_copy(sb, rows, m - 1).wait()

        acc[...] = jnp.zeros_like(acc)
        for sb, rows in subs:
            @pl.when(sb >= n_live)
            def _():
                zero_copy(sb, rows).start()

    fetch = (f >= 1) & (f - 1 < n_next)

    def block(rows, with_fetch):
        if with_fetch:
            for r in range(sub):
                row_copy(nxt, f - 1, r).start()
        acc[rows, :] += _swiglu_partial(h_scr[cur, rows, :], wg_ref, wu_ref, wd_ref)

    for pb in range((len(subs) + 1) // 2):
        first = slice(2 * pb * sub, (2 * pb + 1) * sub)
        variants = (True, False) if pb == 0 else (None,)
        for with_fetch in variants:
            gate = True if with_fetch is None else (fetch if with_fetch else jnp.logical_not(fetch))
            if 2 * pb + 2 <= len(subs):
                both = slice(2 * pb * sub, (2 * pb + 2) * sub)

                @pl.when((2 * pb + 2 <= n_live) & gate)
                def _():
                    block(both, bool(with_fetch))

            @pl.when((2 * pb + 1 == n_live) & gate)
            def _():
                block(first, bool(with_fetch))

    @pl.when(f == n_f - 1)
    def _():
        for sb, rows in subs:
            @pl.when(sb < n_live)
            def _():
                result_copy(sb, rows, m).start()

            @pl.when(sb >= n_live)
            def _():
                zero_copy(sb, rows).wait()

        @pl.when(m == n_m - 1)
        def _():
            for sb, rows in subs:
                @pl.when(sb < n_live)
                def _():
                    result_copy(sb, rows, m).wait()


def _experts(x, g, wg_all, wu_all, wd_all, j, tile_expert, n_sub, sub_base, src, n_tiles, *, tf):
    t, d = x.shape
    tile = MOE_ROW_TILE
    ff = wg_all.shape[-1]
    n_f = ff // tf
    assert n_f >= tile // MOE_SUB_TILE + 2

    def f_eff(m, f, ns):
        return jnp.where(ns[m] > 0, f, n_f - 1)

    w_in_spec = pl.BlockSpec((None, None, d, tf), lambda m, f, te, ns, sbase, sr: (j, te[m], 0, f_eff(m, f, ns)))
    w_dn_spec = pl.BlockSpec((None, None, tf, d), lambda m, f, te, ns, sbase, sr: (j, te[m], f_eff(m, f, ns), 0))
    return pl.pallas_call(
        _expert_kernel,
        grid_spec=pltpu.PrefetchScalarGridSpec(
            num_scalar_prefetch=4,
            grid=(n_tiles, n_f),
            in_specs=[
                pl.BlockSpec(memory_space=pl.ANY),
                pl.BlockSpec((1, d), lambda m, f, te, ns, sbase, sr: (0, 0)),
                w_in_spec,
                w_in_spec,
                w_dn_spec,
            ],
            out_specs=pl.BlockSpec(memory_space=pl.ANY),
            scratch_shapes=[
                pltpu.VMEM((2, MOE_SUB_TILE, d), F32),
                pltpu.VMEM((2, tile, d), BF16),
                pltpu.VMEM((tile, d), F32),
                pltpu.SemaphoreType.DMA((2,)),
                pltpu.SemaphoreType.DMA(()),
                pltpu.SemaphoreType.DMA(()),
            ],
        ),
        out_shape=jax.ShapeDtypeStruct((n_tiles * tile, d), F32),
        compiler_params=_cparams(("arbitrary", "arbitrary")),
        name="moe_experts",
    )(tile_expert, n_sub, sub_base, src, x, g, wg_all, wu_all, wd_all)


def _combine_kernel(dest_ref, x_ref, meta_ref, ys_hbm, o_ref, buf, sem, *, tb, t):
    i, n = pl.program_id(0), pl.num_programs(0)

    def gather_start(tile_idx, slot):
        def start(r, carry):
            for k in range(2):
                row = dest_ref[k * t + tile_idx * tb + r]
                pltpu.make_async_copy(ys_hbm.at[pl.ds(row, 1)], buf.at[slot, k, pl.ds(r, 1)], sem.at[slot]).start()
            return carry

        lax.fori_loop(0, tb, start, 0, unroll=8)

    def combine(slot):
        for k in range(2):
            pltpu.make_async_copy(ys_hbm.at[pl.ds(0, tb)], buf.at[slot, k], sem.at[slot]).wait()
        meta = meta_ref[...]
        o_ref[...] = (x_ref[...] + _lane_col(meta, META_P1) * buf[slot, 0]
                      + _lane_col(meta, META_P2) * buf[slot, 1])

    @pl.when(i == 0)
    def _():
        gather_start(0, 0)

    for slot in range(2):
        @pl.when(i % 2 == slot)
        def _():
            @pl.when(i + 1 < n)
            def _():
                gather_start(i + 1, 1 - slot)

            combine(slot)


def _combine(x, meta, ys, dest, *, tb):
    t, d = x.shape
    return pl.pallas_call(
        functools.partial(_combine_kernel, tb=tb, t=t),
        grid_spec=pltpu.PrefetchScalarGridSpec(
            num_scalar_prefetch=1,
            grid=(t // tb,),
            in_specs=[
                pl.BlockSpec((tb, d), lambda i, dst: (i, 0)),
                pl.BlockSpec((tb, LANES), lambda i, dst: (i, 0)),
                pl.BlockSpec(memory_space=pl.ANY),
            ],
            out_specs=pl.BlockSpec((tb, d), lambda i, dst: (i, 0)),
            scratch_shapes=[pltpu.VMEM((2, 2, tb, d), F32), pltpu.SemaphoreType.DMA((2,))],
        ),
        out_shape=jax.ShapeDtypeStruct((t, d), F32),
        compiler_params=_cparams(("arbitrary",)),
        name="moe_combine",
    )(dest, x, meta, ys)


def _moe(x, g, router_w, router_b, wg_all, wu_all, wd_all, j, *, tb_route, tf):
    t, d = x.shape
    w_pad = jnp.pad(router_w, ((0, 0), (0, LANES - N_EXPERTS)))
    b_pad = jnp.pad(router_b, (0, LANES - N_EXPERTS))[None, :]
    meta, cnt = _router(x, g, w_pad, b_pad, tm=tb_route)
    dest, src, tile_expert, n_sub, sub_base, n_tiles = _moe_plan(meta, cnt, t)
    ys = _experts(x, g, wg_all, wu_all, wd_all, j, tile_expert, n_sub, sub_base, src, n_tiles, tf=tf)
    return _combine(x, meta, ys, dest, tb=min(256, t))


def _ple_kernel(x_ref, g_ref, wg_ref, p_ref, wp_ref, gn_ref, o_ref, hn_ref, wgb_scr, wpb_scr):
    @pl.when(pl.program_id(0) == 0)
    def _():
        wgb_scr[...] = wg_ref[...].astype(BF16)
        wpb_scr[...] = wp_ref[...].astype(BF16)

    x = x_ref[...]
    gate = jax.nn.sigmoid(_dot(_rms(x, g_ref[...]).astype(BF16), wgb_scr[...]))
    x_new = x + gate * _dot(p_ref[...].astype(BF16), wpb_scr[...])
    o_ref[...] = x_new
    hn_ref[...] = _rms(x_new, gn_ref[...]).astype(BF16)


def _ple(x, g, wg_all, p_all, wp_all, layer, g_next, *, tm):
    t, d = x.shape
    pd = p_all.shape[-1]
    row_spec = pl.BlockSpec((tm, d), lambda m: (m, 0))
    gain_spec = pl.BlockSpec((1, d), lambda m: (0, 0))
    return pl.pallas_call(
        _ple_kernel,
        grid=(t // tm,),
        in_specs=[
            row_spec,
            gain_spec,
            _resident_weight_spec((None, d, d), (layer, 0, 0)),
            pl.BlockSpec((None, tm, pd), lambda m: (layer, m, 0)),
            _resident_weight_spec((None, pd, d), (layer, 0, 0)),
            gain_spec,
        ],
        out_specs=[row_spec, row_spec],
        out_shape=[jax.ShapeDtypeStruct((t, d), F32), jax.ShapeDtypeStruct((t, d), BF16)],
        scratch_shapes=[pltpu.VMEM((d, d), BF16), pltpu.VMEM((pd, d), BF16)],
        compiler_params=_cparams(("arbitrary",)),
        name="ple_gate",
    )(x, g, wg_all, p_all, wp_all, g_next)


def _rope_tables(seq):
    rows = seq // GRID_W
    row = jnp.broadcast_to(jnp.arange(rows, dtype=F32)[:, None], (rows, GRID_W)).reshape(seq)
    col = jnp.broadcast_to(jnp.arange(GRID_W, dtype=F32)[None, :], (rows, GRID_W)).reshape(seq)
    axis_dim = ATTN_HEAD_DIM // 2
    inv_freq = ROPE_THETA ** (-jnp.arange(0, axis_dim, 2, dtype=F32) / axis_dim)
    ar, ac = row[:, None] * inv_freq, col[:, None] * inv_freq
    cos_t = jnp.concatenate([jnp.cos(ar), jnp.cos(ar), jnp.cos(ac), jnp.cos(ac)], axis=-1)
    sin_t = jnp.concatenate([-jnp.sin(ar), jnp.sin(ar), -jnp.sin(ac), jnp.sin(ac)], axis=-1)
    return cos_t, sin_t


def kernel(x, p, norm_mix_g, w_in, conv_w, conv_b, b_igate, b_fgate, mlstm_norm_g, q_norm_g, k_norm_g, w_mlstm_up, w_attn_up, w_out, norm_ffn_g, ffn_w_gate, ffn_w_up, ffn_w_down, moe_router, moe_router_b, moe_w_gate, moe_w_up, moe_w_down, norm_ple_g, w_ple_gate, w_ple_proj):
    b, s, d = x.shape
    depth = w_in.shape[0]
    t = b * s
    col = _col_layout(d)
    cos_t, sin_t = _rope_tables(s)
    xt = x.reshape(t, d)
    p_all = p.reshape(depth, t, p.shape[-1])
    w_in_t = jnp.swapaxes(w_in, 1, 2)

    tm_big, tm_half = min(1024, t), min(512, t)
    tn = min(512, d)
    tf = 256

    h_mix = _rms_cast(xt, norm_mix_g[0][None, :], tm=tm_half)
    for i in range(depth):
        z, gates = _in_proj(h_mix, w_in_t, i, tm=min(2048, t), tn=tn)
        z3 = z.reshape(b, s, col["total"])
        qk3 = _conv_silu(z3, conv_w[i], conv_b[i][None, :], col0=col["qk"], tc=256)
        gate_bias = jnp.concatenate(
            [b_igate[i].reshape(-1), b_fgate[i].reshape(-1), jnp.zeros((LANES - N_GATE_COLS,), F32)])[None, :]
        hf, hb = _mlstm_scan(qk3, z3, gates.reshape(b, s, LANES), gate_bias, v_col0=col["mv"])
        q3, k3, vt3 = _qk_prep(z3, cos_t, sin_t, q_norm_g[i][None, :], k_norm_g[i][None, :],
                               aq_col0=col["aq"], ak_col0=col["ak"], av_col0=col["av"], ts=min(512, s))
        ha = _flash_gqa(q3, k3, vt3, _score_bound(q_norm_g[i], k_norm_g[i]), tq=min(256, s), tk=min(2048, s))
        hm = _mlstm_out(hf.reshape(t, MLSTM_WIDTH), hb.reshape(t, MLSTM_WIDTH), z,
                        mlstm_norm_g[i].reshape(1, MLSTM_WIDTH), mo_col0=col["mo"], tm=tm_half)
        y = _merge(hm, ha.reshape(t, ATTN_WIDTH), z, w_mlstm_up, w_attn_up, i,
                   gm_col0=col["gm"], ga_col0=col["ga"], tm=tm_half)
        xt = _out_proj(y, w_out, i, xt, tm=tm_half)

        j = i // 2
        g_ffn = norm_ffn_g[i][None, :]
        if i % 2 == 0:
            xt = _ffn(xt, g_ffn, ffn_w_gate, ffn_w_up, ffn_w_down, j, tm=tm_big, tf=tf)
        else:
            xt = _moe(xt, g_ffn, moe_router[j], moe_router_b[j], moe_w_gate, moe_w_up, moe_w_down, j,
                      tb_route=tm_half, tf=tf)

        g_next = norm_mix_g[min(i + 1, depth - 1)][None, :]
        xt, h_mix = _ple(xt, norm_ple_g[i][None, :], w_ple_gate, p_all, w_ple_proj, i, g_next, tm=tm_half)

    return xt.reshape(b, s, d)
```

```python
import functools
import itertools

import jax
import jax.numpy as jnp
from jax import lax
from jax.experimental import pallas as pl
from jax.experimental.pallas import tpu as pltpu

F32 = jnp.float32
BF16 = jnp.bfloat16

GRID_W = 64
N_MLSTM_HEADS = 4
MLSTM_QK_DIM = 128
MLSTM_V_DIM = 256
MLSTM_QK_WIDTH = N_MLSTM_HEADS * MLSTM_QK_DIM
MLSTM_WIDTH = N_MLSTM_HEADS * MLSTM_V_DIM
MLSTM_CHUNK = 256
N_DIRS = 2
N_ATTN_HEADS = 8
N_KV_HEADS = 2
ATTN_HEAD_DIM = 128
ATTN_WIDTH = N_ATTN_HEADS * ATTN_HEAD_DIM
KV_WIDTH = N_KV_HEADS * ATTN_HEAD_DIM
ROPE_THETA = 10000.0
N_EXPERTS = 8
EPS = 1e-6

LANES = 128
SUBLANES = 8
VMEM_LIMIT_BYTES = 56 * 2**20

N_GATE_COLS = 2 * N_DIRS * N_MLSTM_HEADS


def _col_layout(d_model):
    names = ("qk", "mv", "mo", "aq", "gm", "ga", "ak", "av")
    widths = (2 * MLSTM_QK_WIDTH, MLSTM_WIDTH, MLSTM_WIDTH, ATTN_WIDTH, d_model, d_model, KV_WIDTH, KV_WIDTH)
    off, out = 0, {}
    for n, w in zip(names, widths):
        out[n] = off
        off += w
    out["total"] = off
    return out


N_HEAD_COLS = 2 * MLSTM_QK_WIDTH + MLSTM_WIDTH


def _cparams(semantics):
    return pltpu.CompilerParams(dimension_semantics=semantics, vmem_limit_bytes=VMEM_LIMIT_BYTES)


def _rms(x, g):
    ms = jnp.mean(x * x, axis=-1, keepdims=True)
    return x * lax.rsqrt(ms + EPS) * g


def _dot(a, b):
    return jnp.dot(a, b, preferred_element_type=F32)


def _dot_nt(a, b):
    return lax.dot_general(a, b, (((1,), (1,)), ((), ())), preferred_element_type=F32)


def _dot_tn(a, b):
    return lax.dot_general(a, b, (((0,), (0,)), ((), ())), preferred_element_type=F32)


def _rms_cast_kernel(x_ref, g_ref, h_ref):
    h_ref[...] = _rms(x_ref[...], g_ref[...]).astype(BF16)


def _rms_cast(x, g, *, tm):
    t, d = x.shape
    row_spec = pl.BlockSpec((tm, d), lambda m: (m, 0))
    return pl.pallas_call(
        _rms_cast_kernel,
        grid=(t // tm,),
        in_specs=[row_spec, pl.BlockSpec((1, d), lambda m: (0, 0))],
        out_specs=row_spec,
        out_shape=jax.ShapeDtypeStruct((t, d), BF16),
        compiler_params=_cparams(("parallel",)),
        name="input_rms_norm",
    )(x, g)


def _in_proj_kernel(h_ref, wm_ref, wn_ref, wg_ref, o_ref, gates_ref, *, head_blocks, tail_blocks):
    j = pl.program_id(1)
    ng = N_GATE_COLS

    @pl.when(j < head_blocks)
    def _():
        o_ref[...] = _dot_nt(h_ref[...], wm_ref[...].astype(BF16)).astype(o_ref.dtype)

    @pl.when((j >= head_blocks) & (j < head_blocks + tail_blocks))
    def _():
        w = jnp.concatenate([wm_ref[ng:, :], wn_ref[...]], axis=0).astype(BF16)
        o_ref[...] = _dot_nt(h_ref[...], w).astype(o_ref.dtype)

    @pl.when(j == head_blocks + tail_blocks)
    def _():
        pad = jnp.zeros((gates_ref.shape[1] - ng, wg_ref.shape[1]), F32)
        w = jnp.concatenate([wg_ref[...], pad], axis=0).astype(BF16)
        gates_ref[...] = _dot_nt(h_ref[...], w)


def _in_proj(h, wt_all, layer, *, tm, tn):
    t, d = h.shape
    n_cols = wt_all.shape[1]
    ng = N_GATE_COLS
    tail = n_cols - N_HEAD_COLS - ng
    head_blocks, tail_blocks = N_HEAD_COLS // tn, tail // tn
    assert tail % tn == 0 and N_HEAD_COLS % tn == 0 and n_cols % ng == 0
    last_main = head_blocks + tail_blocks - 1
    pre, kv, gates2 = (MLSTM_WIDTH + ATTN_WIDTH) // tn, 2 * KV_WIDTH // tn, 2 * d // tn
    assert (2 * KV_WIDTH) % tn == 0 and pre + kv + gates2 == tail_blocks

    def out_block(j):
        jz = jnp.minimum(j, last_main)
        tb = jz - head_blocks
        moved = jnp.where(tb < pre, tb, jnp.where(tb < pre + kv, tb + gates2, tb - kv))
        return jnp.where(tb >= 0, head_blocks + moved, jz)

    return pl.pallas_call(
        functools.partial(_in_proj_kernel, head_blocks=head_blocks, tail_blocks=tail_blocks),
        grid=(t // tm, head_blocks + tail_blocks + 1),
        in_specs=[
            pl.BlockSpec((tm, d), lambda m, j: (m, 0)),
            pl.BlockSpec((None, tn, d), lambda m, j: (layer, jnp.minimum(j, last_main), 0)),
            pl.BlockSpec((None, ng, d), lambda m, j: (layer, jnp.minimum((j + 1) * (tn // ng), n_cols // ng - 1), 0)),
            pl.BlockSpec((None, ng, d), lambda m, j: (layer, N_HEAD_COLS // ng, 0)),
        ],
        out_specs=[pl.BlockSpec((tm, tn), lambda m, j: (m, out_block(j))),
                   pl.BlockSpec((tm, LANES), lambda m, j: (m, 0))],
        out_shape=[jax.ShapeDtypeStruct((t, N_HEAD_COLS + tail), BF16), jax.ShapeDtypeStruct((t, LANES), F32)],
        compiler_params=_cparams(("parallel", "arbitrary")),
        name="in_proj",
    )(h, wt_all, wt_all, wt_all)


def _conv_silu_kernel(z_ref, w_ref, b_ref, o_ref, *, q_blocks, q_scale):
    x = z_ref[0].astype(F32)
    s = x.shape[0]
    row = lax.broadcasted_iota(jnp.int32, x.shape, 0)
    prev = jnp.where(row == 0, 0.0, pltpu.roll(x, 1, 0))
    nxt = jnp.where(row == s - 1, 0.0, pltpu.roll(x, s - 1, 0))
    y = prev * w_ref[0:1, :] + x * w_ref[1:2, :] + nxt * w_ref[2:3, :] + b_ref[...]
    y = y * jax.nn.sigmoid(y)
    scale = jnp.where(pl.program_id(1) < q_blocks, q_scale, 1.0).astype(F32)
    o_ref[0] = (y * scale).astype(o_ref.dtype)


def _conv_silu(z3, conv_w, conv_b, *, col0, tc):
    b, s, _ = z3.shape
    width = conv_w.shape[1]
    kern = functools.partial(_conv_silu_kernel, q_blocks=MLSTM_QK_WIDTH // tc, q_scale=MLSTM_QK_DIM ** -0.5)
    return pl.pallas_call(
        kern,
        grid=(b, width // tc),
        in_specs=[
            pl.BlockSpec((1, s, tc), lambda i, j: (i, 0, col0 // tc + j)),
            pl.BlockSpec((3, tc), lambda i, j: (0, j)),
            pl.BlockSpec((1, tc), lambda i, j: (0, j)),
        ],
        out_specs=pl.BlockSpec((1, s, tc), lambda i, j: (i, 0, j)),
        out_shape=jax.ShapeDtypeStruct((b, s, width), BF16),
        compiler_params=_cparams(("parallel", "parallel")),
        name="mlstm_conv_silu",
    )(z3, conv_w, conv_b)


def _log_sigmoid(x):
    return jnp.minimum(x, 0.0) - jnp.log1p(jnp.exp(-jnp.abs(x)))


def _lane_col(x, idx):
    lane = lax.broadcasted_iota(jnp.int32, x.shape, 1)
    return jnp.sum(jnp.where(lane == idx, x, 0.0), axis=-1, keepdims=True)


def _mlstm_kernel(qkf_ref, qkb_ref, vf_ref, vb_ref, gf_ref, gb_ref, bias_ref, hf_ref, hb_ref, c_scr, m_scr):
    L = MLSTM_CHUNK
    dk, dv, nh = MLSTM_QK_DIM, MLSTM_V_DIM, N_MLSTM_HEADS

    @pl.when(pl.program_id(0) == 0)
    def _():
        c_scr[...] = jnp.zeros_like(c_scr)
        m_scr[...] = jnp.zeros_like(m_scr)

    r_i = lax.broadcasted_iota(jnp.int32, (L, L), 0)
    c_i = lax.broadcasted_iota(jnp.int32, (L, L), 1)
    lane = lax.broadcasted_iota(jnp.int32, (L, LANES), 1)
    ones_blk = jnp.where(lane == 0, 1.0, 0.0).astype(F32)

    dirs = ((qkf_ref, vf_ref, gf_ref, hf_ref), (qkb_ref, vb_ref, gb_ref, hb_ref))
    for bi, (d, (qk_ref, v_ref, g_ref, h_ref)) in itertools.product(range(qkf_ref.shape[0]), enumerate(dirs)):
        mask = (c_i <= r_i) if d == 0 else (c_i >= r_i)
        g = g_ref[bi] + bias_ref[...]
        gp = jnp.where(lane < N_GATE_COLS // 2, g, _log_sigmoid(g))
        gp_t = gp.T
        bc = jnp.dot(mask.astype(F32), gp, precision=lax.Precision.HIGHEST, preferred_element_type=F32)
        bc_t = bc.T
        end_row = L - 1 if d == 0 else 0
        for h in range(nh):
            ci = (bi * N_DIRS + d) * nh + h
            ch_i, ch_f = d * nh + h, N_GATE_COLS // 2 + d * nh + h
            i_row = gp_t[ch_i:ch_i + 1, :]
            bc_row = bc_t[ch_f:ch_f + 1, :]
            i_col = _lane_col(gp, ch_i)
            bc_col = _lane_col(bc, ch_f)
            m_prev = m_scr[ci, 0:1, 0:1]
            c_prev = c_scr[ci]

            a_col = bc_col + m_prev
            dm = jnp.where(mask, bc_col - bc_row + i_row, -jnp.inf)
            m_t = jnp.maximum(a_col, jnp.max(dm, axis=-1, keepdims=True))
            w_inter = jnp.exp(a_col - m_t)
            w_intra = jnp.exp(dm - m_t)

            q = qk_ref[bi, :, h * dk:(h + 1) * dk]
            k = qk_ref[bi, :, MLSTM_QK_WIDTH + h * dk:MLSTM_QK_WIDTH + (h + 1) * dk]
            v = v_ref[bi, :, h * dv:(h + 1) * dv].astype(F32)
            v_ext = jnp.concatenate([v, ones_blk], axis=1)
            p = (_dot_nt(q, k) * w_intra).astype(BF16)
            num_ext = w_inter * _dot(q, c_prev.astype(BF16)) + _dot(p, v_ext.astype(BF16))
            den = num_ext[:, dv:dv + 1]
            h_out = num_ext[:, :dv] / jnp.maximum(jnp.abs(den), jnp.exp(-m_t))
            h_ref[bi, :, h * dv:(h + 1) * dv] = h_out.astype(h_ref.dtype)

            b_end = bc_col[end_row:end_row + 1, :]
            g_col = b_end - bc_col + i_col
            m_new = jnp.maximum(b_end + m_prev, jnp.max(g_col, axis=0, keepdims=True))
            decay = jnp.exp(b_end + m_prev - m_new)
            ws = jnp.exp(g_col - m_new)
            c_scr[ci] = decay * c_prev + _dot_tn(k, (ws * v_ext).astype(BF16))
            m_scr[ci] = jnp.broadcast_to(m_new, m_scr.shape[1:])


def _mlstm_scan(qk3, z3, gates3, gate_bias, *, v_col0):
    b, s, _ = qk3.shape
    L = MLSTM_CHUNK
    assert s % L == 0
    nc = s // L
    n_chain = b * N_DIRS * N_MLSTM_HEADS
    vb = v_col0 // MLSTM_WIDTH
    qk_w = 2 * MLSTM_QK_WIDTH
    return pl.pallas_call(
        _mlstm_kernel,
        grid=(nc,),
        in_specs=[
            pl.BlockSpec((b, L, qk_w), lambda c: (0, c, 0)),
            pl.BlockSpec((b, L, qk_w), lambda c: (0, nc - 1 - c, 0)),
            pl.BlockSpec((b, L, MLSTM_WIDTH), lambda c: (0, c, vb)),
            pl.BlockSpec((b, L, MLSTM_WIDTH), lambda c: (0, nc - 1 - c, vb)),
            pl.BlockSpec((b, L, LANES), lambda c: (0, c, 0)),
            pl.BlockSpec((b, L, LANES), lambda c: (0, nc - 1 - c, 0)),
            pl.BlockSpec((1, LANES), lambda c: (0, 0)),
        ],
        out_specs=[
            pl.BlockSpec((b, L, MLSTM_WIDTH), lambda c: (0, c, 0)),
            pl.BlockSpec((b, L, MLSTM_WIDTH), lambda c: (0, nc - 1 - c, 0)),
        ],
        out_shape=[jax.ShapeDtypeStruct((b, s, MLSTM_WIDTH), BF16)] * 2,
        scratch_shapes=[
            pltpu.VMEM((n_chain, MLSTM_QK_DIM, MLSTM_V_DIM + LANES), F32),
            pltpu.VMEM((n_chain, SUBLANES, LANES), F32),
        ],
        compiler_params=_cparams(("arbitrary",)),
        name="mlstm_scan",
    )(qk3, qk3, z3, z3, gates3, gates3, gate_bias)


LOG2_E = 1.4426950408889634


def _qk_prep_kernel(aq_ref, ak_ref, av_ref, cos_ref, sin_ref, gq_ref, gk_ref, q_ref, k_ref, vt_ref):
    cos, sin = cos_ref[...], sin_ref[...]
    hd = ATTN_HEAD_DIM
    lane = lax.broadcasted_iota(jnp.int32, cos.shape, 1)
    first_half = (lane % (hd // 2)) < (hd // 4)

    def prep(x, g, scale):
        n = _rms(x.astype(F32), g)
        partner = jnp.where(first_half, pltpu.roll(n, hd - hd // 4, 1), pltpu.roll(n, hd // 4, 1))
        return ((n * cos + partner * sin) * scale).astype(BF16)

    for h in range(N_ATTN_HEADS):
        q_ref[0, :, h * hd:(h + 1) * hd] = prep(aq_ref[0, :, h * hd:(h + 1) * hd], gq_ref[...], hd ** -0.5 * LOG2_E)
    for h in range(N_KV_HEADS):
        k_ref[0, :, h * hd:(h + 1) * hd] = prep(ak_ref[0, :, h * hd:(h + 1) * hd], gk_ref[...], 1.0)
    vt_ref[0] = av_ref[0].astype(F32).T.astype(BF16)


def _qk_prep(z3, cos_t, sin_t, gq, gk, *, aq_col0, ak_col0, av_col0, ts):
    b, s, _ = z3.shape
    return pl.pallas_call(
        _qk_prep_kernel,
        grid=(b, s // ts),
        in_specs=[
            pl.BlockSpec((1, ts, ATTN_WIDTH), lambda i, j: (i, j, aq_col0 // ATTN_WIDTH)),
            pl.BlockSpec((1, ts, KV_WIDTH), lambda i, j: (i, j, ak_col0 // KV_WIDTH)),
            pl.BlockSpec((1, ts, KV_WIDTH), lambda i, j: (i, j, av_col0 // KV_WIDTH)),
            pl.BlockSpec((ts, ATTN_HEAD_DIM), lambda i, j: (j, 0)),
            pl.BlockSpec((ts, ATTN_HEAD_DIM), lambda i, j: (j, 0)),
            pl.BlockSpec((1, ATTN_HEAD_DIM), lambda i, j: (0, 0)),
            pl.BlockSpec((1, ATTN_HEAD_DIM), lambda i, j: (0, 0)),
        ],
        out_specs=[
            pl.BlockSpec((1, ts, ATTN_WIDTH), lambda i, j: (i, j, 0)),
            pl.BlockSpec((1, ts, KV_WIDTH), lambda i, j: (i, j, 0)),
            pl.BlockSpec((1, KV_WIDTH, ts), lambda i, j: (i, 0, j)),
        ],
        out_shape=[jax.ShapeDtypeStruct((b, s, ATTN_WIDTH), BF16), jax.ShapeDtypeStruct((b, s, KV_WIDTH), BF16),
                   jax.ShapeDtypeStruct((b, KV_WIDTH, s), BF16)],
        compiler_params=_cparams(("parallel", "parallel")),
        name="attn_qk_norm_rope",
    )(z3, z3, z3, cos_t, sin_t, gq, gk)


SOFTMAX_SHIFT_LIMIT = 60.0


def _flash_kernel(bound_ref, q_ref, k_ref, vt_ref, o_ref, *, grp, tk):
    hd = ATTN_HEAD_DIM
    tq = q_ref.shape[1]
    s_len = k_ref.shape[1]
    rows = grp * tq
    bound = bound_ref[0, 0]
    q = jnp.concatenate([q_ref[0, :, g * hd:(g + 1) * hd] for g in range(grp)], axis=0)
    chunks = [slice(c * tk, (c + 1) * tk) for c in range(s_len // tk)]

    def store(acc, l):
        o = (acc / l).T
        for g in range(grp):
            o_ref[0, :, g * hd:(g + 1) * hd] = o[g * tq:(g + 1) * tq, :].astype(o_ref.dtype)

    @pl.when(bound <= SOFTMAX_SHIFT_LIMIT)
    def _():
        l = jnp.zeros((1, rows), F32)
        acc = jnp.zeros((hd, rows), F32)
        for ck in chunks:
            p = jnp.exp2(_dot_nt(k_ref[0, ck, :], q) - bound)
            l = l + jnp.sum(p, axis=0, keepdims=True)
            acc = acc + _dot(vt_ref[0, :, ck], p.astype(BF16))
        store(acc, l)

    @pl.when(bound > SOFTMAX_SHIFT_LIMIT)
    def _():
        m = jnp.full((1, rows), -jnp.inf, F32)
        l = jnp.zeros((1, rows), F32)
        acc = jnp.zeros((hd, rows), F32)
        for ck in chunks:
            st = _dot_nt(k_ref[0, ck, :], q)
            m_new = jnp.maximum(m, jnp.max(st, axis=0, keepdims=True))
            alpha = jnp.exp2(m - m_new)
            p = jnp.exp2(st - m_new)
            l = alpha * l + jnp.sum(p, axis=0, keepdims=True)
            acc = alpha * acc + _dot(vt_ref[0, :, ck], p.astype(BF16))
            m = m_new
        store(acc, l)


def _score_bound(gq, gk):
    scale = ATTN_HEAD_DIM ** -0.5 * LOG2_E
    return (1.02 * ATTN_HEAD_DIM * scale * jnp.max(jnp.abs(gq)) * jnp.max(jnp.abs(gk))).reshape(1, 1).astype(F32)


def _flash_gqa(q3, k3, vt3, bound, *, tq, tk):
    b, s, _ = q3.shape
    hd = ATTN_HEAD_DIM
    grp = N_ATTN_HEADS // N_KV_HEADS
    return pl.pallas_call(
        functools.partial(_flash_kernel, grp=grp, tk=tk),
        grid=(b, N_KV_HEADS, s // tq),
        in_specs=[
            pl.BlockSpec(memory_space=pltpu.SMEM),
            pl.BlockSpec((1, tq, grp * hd), lambda i, h, qi: (i, qi, h)),
            pl.BlockSpec((1, s, hd), lambda i, h, qi: (i, 0, h)),
            pl.BlockSpec((1, hd, s), lambda i, h, qi: (i, h, 0)),
        ],
        out_specs=pl.BlockSpec((1, tq, grp * hd), lambda i, h, qi: (i, qi, h)),
        out_shape=jax.ShapeDtypeStruct((b, s, ATTN_WIDTH), BF16),
        compiler_params=_cparams(("parallel", "parallel", "parallel")),
        name="gqa_flash",
    )(bound, q3, k3, vt3)


def _mlstm_out_kernel(hf_ref, hb_ref, mo_ref, ng_ref, hm_ref):
    dv = MLSTM_V_DIM
    for h in range(N_MLSTM_HEADS):
        sl = slice(h * dv, (h + 1) * dv)
        hm = _rms(hf_ref[:, sl].astype(F32) + hb_ref[:, sl].astype(F32), ng_ref[:, sl])
        hm_ref[:, sl] = (hm * jax.nn.sigmoid(mo_ref[:, sl].astype(F32))).astype(BF16)


def _mlstm_out(hf, hb, z, ng, *, mo_col0, tm):
    t, w = hf.shape
    row_spec = pl.BlockSpec((tm, w), lambda m: (m, 0))
    return pl.pallas_call(
        _mlstm_out_kernel,
        grid=(t // tm,),
        in_specs=[row_spec, row_spec, pl.BlockSpec((tm, w), lambda m: (m, mo_col0 // w)),
                  pl.BlockSpec((1, w), lambda m: (0, 0))],
        out_specs=row_spec,
        out_shape=jax.ShapeDtypeStruct((t, w), BF16),
        compiler_params=_cparams(("parallel",)),
        name="mlstm_out_norm",
    )(hf, hb, z, ng)


def _merge_kernel(hm_ref, ha_ref, gm_ref, ga_ref, wm_ref, wa_ref, y_ref, wmb_scr, wab_scr):
    @pl.when(pl.program_id(0) == 0)
    def _():
        wmb_scr[...] = wm_ref[...].astype(BF16)
        wab_scr[...] = wa_ref[...].astype(BF16)

    ym = _dot(hm_ref[...], wmb_scr[...])
    ya = _dot(ha_ref[...], wab_scr[...])
    gate_m = jax.nn.sigmoid(gm_ref[...].astype(F32))
    gate_a = jax.nn.sigmoid(ga_ref[...].astype(F32))
    y_ref[...] = (gate_m * ym + gate_a * ya).astype(y_ref.dtype)


def _merge(hm, ha, z, wm_all, wa_all, layer, *, gm_col0, ga_col0, tm):
    t, w = hm.shape
    d = wm_all.shape[2]
    act_spec = pl.BlockSpec((tm, w), lambda m: (m, 0))
    w_spec = _resident_weight_spec((None, w, d), (layer, 0, 0))
    return pl.pallas_call(
        _merge_kernel,
        grid=(t // tm,),
        in_specs=[
            act_spec,
            act_spec,
            pl.BlockSpec((tm, d), lambda m: (m, gm_col0 // d)),
            pl.BlockSpec((tm, d), lambda m: (m, ga_col0 // d)),
            w_spec,
            w_spec,
        ],
        out_specs=pl.BlockSpec((tm, d), lambda m: (m, 0)),
        out_shape=jax.ShapeDtypeStruct((t, d), BF16),
        scratch_shapes=[pltpu.VMEM((w, d), BF16), pltpu.VMEM((w, d), BF16)],
        compiler_params=_cparams(("arbitrary",)),
        name="branch_merge",
    )(hm, ha, z, z, wm_all, wa_all)


def _resident_weight_spec(shape, index):
    return pl.BlockSpec(shape, lambda *_: index, pipeline_mode=pl.Buffered(1))


def _out_proj_kernel(y_ref, w_ref, x_ref, o_ref, wb_scr):
    @pl.when(pl.program_id(0) == 0)
    def _():
        wb_scr[...] = w_ref[...].astype(BF16)

    o_ref[...] = x_ref[...] + _dot(y_ref[...], wb_scr[...])


def _out_proj(y, w_all, layer, x, *, tm):
    t, d = x.shape
    k = y.shape[1]
    return pl.pallas_call(
        _out_proj_kernel,
        grid=(t // tm,),
        in_specs=[
            pl.BlockSpec((tm, k), lambda m: (m, 0)),
            _resident_weight_spec((None, k, d), (layer, 0, 0)),
            pl.BlockSpec((tm, d), lambda m: (m, 0)),
        ],
        out_specs=pl.BlockSpec((tm, d), lambda m: (m, 0)),
        out_shape=jax.ShapeDtypeStruct((t, d), F32),
        scratch_shapes=[pltpu.VMEM((k, d), BF16)],
        compiler_params=_cparams(("arbitrary",)),
        name="out_proj_residual",
    )(y, w_all, x)


def _swiglu_partial(h, wg_ref, wu_ref, wd_ref):
    a = _dot(h, wg_ref[...].astype(BF16))
    u = _dot(h, wu_ref[...].astype(BF16))
    t = a * jax.nn.sigmoid(a) * u
    return _dot(t.astype(BF16), wd_ref[...].astype(BF16))


def _ffn_kernel(x_ref, g_ref, wg_ref, wu_ref, wd_ref, o_ref, h_scr):
    @pl.when(pl.program_id(1) == 0)
    def _():
        x = x_ref[...]
        h_scr[...] = _rms(x, g_ref[...]).astype(BF16)
        o_ref[...] = x

    o_ref[...] += _swiglu_partial(h_scr[...], wg_ref, wu_ref, wd_ref)


def _ffn(x, g, wg_all, wu_all, wd_all, j, *, tm, tf):
    t, d = x.shape
    ff = wg_all.shape[-1]
    w_in_spec = pl.BlockSpec((None, d, tf), lambda m, f: (j, 0, f))
    return pl.pallas_call(
        _ffn_kernel,
        grid=(t // tm, ff // tf),
        in_specs=[
            pl.BlockSpec((tm, d), lambda m, f: (m, 0)),
            pl.BlockSpec((1, d), lambda m, f: (0, 0)),
            w_in_spec,
            w_in_spec,
            pl.BlockSpec((None, tf, d), lambda m, f: (j, f, 0)),
        ],
        out_specs=pl.BlockSpec((tm, d), lambda m, f: (m, 0)),
        out_shape=jax.ShapeDtypeStruct((t, d), F32),
        scratch_shapes=[pltpu.VMEM((tm, d), BF16)],
        compiler_params=_cparams(("parallel", "arbitrary")),
        name="dense_swiglu",
    )(x, g, wg_all, wu_all, wd_all)


MOE_ROW_TILE = 2560
MOE_SUB_TILE = 256
MOE_BULK_SUBS = 8
META_E1, META_E2, META_P1, META_P2, META_R1, META_R2 = range(6)


def _router_kernel(x_ref, g_ref, w_ref, b_ref, meta_ref, cnt_ref, carry_scr):
    @pl.when(pl.program_id(0) == 0)
    def _():
        carry_scr[...] = jnp.zeros_like(carry_scr)

    h = _rms(x_ref[...], g_ref[...])
    logits = jnp.dot(h, w_ref[...], precision=lax.Precision.HIGHEST, preferred_element_type=F32) + b_ref[...]
    tb = logits.shape[0]
    lane = lax.broadcasted_iota(jnp.int32, logits.shape, 1)
    logits = jnp.where(lane < N_EXPERTS, logits, -jnp.inf)
    v1 = jnp.max(logits, axis=-1, keepdims=True)
    i1 = jnp.min(jnp.where(logits == v1, lane, LANES), axis=-1, keepdims=True)
    rest = jnp.where(lane == i1, -jnp.inf, logits)
    v2 = jnp.max(rest, axis=-1, keepdims=True)
    i2 = jnp.min(jnp.where(rest == v2, lane, LANES), axis=-1, keepdims=True)
    e2 = jnp.exp(v2 - v1)
    p1 = 1.0 / (1.0 + e2)
    p2 = e2 / (1.0 + e2)
    sel = ((lane == i1) | (lane == i2)).astype(F32)
    r_i = lax.broadcasted_iota(jnp.int32, (tb, tb), 0)
    c_i = lax.broadcasted_iota(jnp.int32, (tb, tb), 1)
    earlier = (c_i < r_i).astype(BF16)
    rank = carry_scr[0:1, :] + _dot(earlier, sel.astype(BF16))
    r1 = jnp.sum(jnp.where(lane == i1, rank, 0.0), axis=-1, keepdims=True)
    r2 = jnp.sum(jnp.where(lane == i2, rank, 0.0), axis=-1, keepdims=True)
    meta = jnp.zeros_like(logits)
    for idx, val in ((META_E1, i1.astype(F32)), (META_E2, i2.astype(F32)), (META_P1, p1), (META_P2, p2),
                     (META_R1, r1), (META_R2, r2)):
        meta = jnp.where(lane == idx, val, meta)
    meta_ref[...] = meta
    total = carry_scr[0:1, :] + jnp.sum(sel, axis=0, keepdims=True)
    carry_scr[...] = jnp.broadcast_to(total, carry_scr.shape)
    cnt_ref[...] = jnp.broadcast_to(total, cnt_ref.shape)


def _router(x, g, w_pad, b_pad, *, tm):
    t, d = x.shape
    return pl.pallas_call(
        _router_kernel,
        grid=(t // tm,),
        in_specs=[
            pl.BlockSpec((tm, d), lambda m: (m, 0)),
            pl.BlockSpec((1, d), lambda m: (0, 0)),
            pl.BlockSpec((d, LANES), lambda m: (0, 0)),
            pl.BlockSpec((1, LANES), lambda m: (0, 0)),
        ],
        out_specs=[pl.BlockSpec((tm, LANES), lambda m: (m, 0)), pl.BlockSpec((SUBLANES, LANES), lambda m: (0, 0))],
        out_shape=[jax.ShapeDtypeStruct((t, LANES), F32), jax.ShapeDtypeStruct((SUBLANES, LANES), F32)],
        scratch_shapes=[pltpu.VMEM((SUBLANES, LANES), F32)],
        compiler_params=_cparams(("arbitrary",)),
        name="moe_router",
    )(x, g, w_pad, b_pad)


def _moe_plan(meta, cnt, t):
    tile, sub = MOE_ROW_TILE, MOE_SUB_TILE
    n_tiles = 2 * t // tile + N_EXPERTS
    e1 = meta[:, META_E1].astype(jnp.int32)
    e2 = meta[:, META_E2].astype(jnp.int32)
    r1 = meta[:, META_R1].astype(jnp.int32)
    r2 = meta[:, META_R2].astype(jnp.int32)
    counts = cnt[0, :N_EXPERTS].astype(jnp.int32)
    tiles_e = (counts + tile - 1) // tile
    tile_end = jnp.cumsum(tiles_e)
    tile_start = tile_end - tiles_e
    row_start = tile_start * tile
    dest = jnp.concatenate([row_start[e1] + r1, row_start[e2] + r2]).astype(jnp.int32)
    m = jnp.arange(n_tiles, dtype=jnp.int32)
    live = m < tile_end[-1]
    m_eff = jnp.where(live, m, jnp.maximum(tile_end[-1] - 1, 0))
    tile_expert = jnp.minimum(jnp.searchsorted(tile_end, m_eff, side="right"), N_EXPERTS - 1).astype(jnp.int32)
    rows_left = counts[tile_expert] - (m_eff - tile_start[tile_expert]) * tile
    n_sub = jnp.where(live, jnp.clip((rows_left + sub - 1) // sub, 0, tile // sub), 0).astype(jnp.int32)
    sub_base = (jnp.cumsum(n_sub) - n_sub).astype(jnp.int32)
    tok = jnp.arange(t, dtype=jnp.int32)
    src_pos = sub_base[dest // tile] * sub + dest % tile
    n_src = (2 * t // sub + N_EXPERTS) * sub
    src = jnp.zeros((n_src,), jnp.int32).at[src_pos].set(jnp.concatenate([tok, tok]))
    return dest, src, tile_expert, n_sub, sub_base, n_tiles


def _expert_kernel(te_ref, ns_ref, sbase_ref, src_ref, x_hbm, g_ref, wg_ref, wu_ref, wd_ref, ys_hbm,
                   xbuf, h_scr, acc, gsem, zsem, osem):
    del te_ref
    m, f = pl.program_id(0), pl.program_id(1)
    n_m, n_f = pl.num_programs(0), pl.num_programs(1)
    tile, sub = MOE_ROW_TILE, MOE_SUB_TILE
    n_live = ns_ref[m]
    subs = [(sb, slice(sb * sub, (sb + 1) * sub)) for sb in range(tile // sub)]

    def gather_start(sb, slot):
        def start(r, carry):
            tok = src_ref[(sbase_ref[m] + sb) * sub + r]
            pltpu.make_async_copy(x_hbm.at[pl.ds(tok, 1)], xbuf.at[slot, pl.ds(r, 1)], gsem.at[slot]).start()
            return carry

        lax.fori_loop(0, sub, start, 0, unroll=8)

    def result_copy(sb, rows, tile_idx):
        return pltpu.make_async_copy(acc.at[rows], ys_hbm.at[pl.ds(tile_idx * tile + sb * sub, sub)], osem)

    def zero_copy(sb):
        return pltpu.make_async_copy(xbuf.at[0], ys_hbm.at[pl.ds(m * tile + sb * sub, sub)], zsem)

    @pl.when(f == 0)
    def _():
        @pl.when(n_live > 0)
        def _():
            gather_start(0, 0)

        for sb, rows in subs:
            slot = sb % 2
            if sb + 1 < len(subs):
                @pl.when(sb + 1 < n_live)
                def _():
                    gather_start(sb + 1, 1 - slot)

            @pl.when(sb < n_live)
            def _():
                pltpu.make_async_copy(x_hbm.at[pl.ds(0, sub)], xbuf.at[slot], gsem.at[slot]).wait()
                h_scr[rows, :] = _rms(xbuf[slot], g_ref[...]).astype(BF16)

        @pl.when(m > 0)
        def _():
            for sb, rows in subs:
                @pl.when(sb < ns_ref[m - 1])
                def _():
                    result_copy(sb, rows, m - 1).wait()

        for sb, rows in subs:
            @pl.when(sb < n_live)
            def _():
                acc[rows, :] = jnp.zeros((sub, acc.shape[1]), F32)

    @pl.when(f == 1)
    def _():
        xbuf[0] = jnp.zeros(xbuf.shape[1:], F32)
        for sb, rows in subs:
            @pl.when(sb >= n_live)
            def _():
                zero_copy(sb).start()

    bulk = MOE_BULK_SUBS if MOE_BULK_SUBS <= len(subs) else 0
    if bulk:
        bulk_rows = slice(0, bulk * sub)

        @pl.when(n_live >= bulk)
        def _():
            acc[bulk_rows, :] += _swiglu_partial(h_scr[bulk_rows, :], wg_ref, wu_ref, wd_ref)

    for pb in range(len(subs) // 2):
        both = slice(2 * pb * sub, (2 * pb + 2) * sub)
        first = slice(2 * pb * sub, (2 * pb + 1) * sub)
        in_bulk = bulk and 2 * pb + 2 <= bulk

        @pl.when((2 * pb + 2 <= n_live) & ((n_live < bulk) if in_bulk else True))
        def _():
            acc[both, :] += _swiglu_partial(h_scr[both, :], wg_ref, wu_ref, wd_ref)

        @pl.when(2 * pb + 1 == n_live)
        def _():
            acc[first, :] += _swiglu_partial(h_scr[first, :], wg_ref, wu_ref, wd_ref)

    @pl.when(f == n_f - 1)
    def _():
        for sb, rows in subs:
            @pl.when(sb < n_live)
            def _():
                result_copy(sb, rows, m).start()

            @pl.when(sb >= n_live)
            def _():
                zero_copy(sb).wait()

        @pl.when(m == n_m - 1)
        def _():
            for sb, rows in subs:
                @pl.when(sb < n_live)
                def _():
                    result_copy(sb, rows, m).wait()


def _experts(x, g, wg_all, wu_all, wd_all, j, tile_expert, n_sub, sub_base, src, n_tiles, *, tf):
    t, d = x.shape
    tile = MOE_ROW_TILE
    ff = wg_all.shape[-1]
    n_f = ff // tf

    def f_eff(m, f, ns):
        return jnp.where(ns[m] > 0, f, n_f - 1)

    w_in_spec = pl.BlockSpec((None, None, d, tf), lambda m, f, te, ns, sbase, sr: (j, te[m], 0, f_eff(m, f, ns)))
    w_dn_spec = pl.BlockSpec((None, None, tf, d), lambda m, f, te, ns, sbase, sr: (j, te[m], f_eff(m, f, ns), 0))
    return pl.pallas_call(
        _expert_kernel,
        grid_spec=pltpu.PrefetchScalarGridSpec(
            num_scalar_prefetch=4,
            grid=(n_tiles, n_f),
            in_specs=[
                pl.BlockSpec(memory_space=pl.ANY),
                pl.BlockSpec((1, d), lambda m, f, te, ns, sbase, sr: (0, 0)),
                w_in_spec,
                w_in_spec,
                w_dn_spec,
            ],
            out_specs=pl.BlockSpec(memory_space=pl.ANY),
            scratch_shapes=[
                pltpu.VMEM((2, MOE_SUB_TILE, d), F32),
                pltpu.VMEM((tile, d), BF16),
                pltpu.VMEM((tile, d), F32),
                pltpu.SemaphoreType.DMA((2,)),
                pltpu.SemaphoreType.DMA(()),
                pltpu.SemaphoreType.DMA(()),
            ],
        ),
        out_shape=jax.ShapeDtypeStruct((n_tiles * tile, d), F32),
        compiler_params=_cparams(("arbitrary", "arbitrary")),
        name="moe_experts",
    )(tile_expert, n_sub, sub_base, src, x, g, wg_all, wu_all, wd_all)


def _combine_kernel(dest_ref, x_ref, meta_ref, ys_hbm, o_ref, buf, sem, *, tb, t):
    i, n = pl.program_id(0), pl.num_programs(0)

    def gather_start(tile_idx, slot):
        def start(r, carry):
            for k in range(2):
                row = dest_ref[k * t + tile_idx * tb + r]
                pltpu.make_async_copy(ys_hbm.at[pl.ds(row, 1)], buf.at[slot, k, pl.ds(r, 1)], sem.at[slot]).start()
            return carry

        lax.fori_loop(0, tb, start, 0, unroll=8)

    def combine(slot):
        for k in range(2):
            pltpu.make_async_copy(ys_hbm.at[pl.ds(0, tb)], buf.at[slot, k], sem.at[slot]).wait()
        meta = meta_ref[...]
        o_ref[...] = (x_ref[...] + _lane_col(meta, META_P1) * buf[slot, 0]
                      + _lane_col(meta, META_P2) * buf[slot, 1])

    @pl.when(i == 0)
    def _():
        gather_start(0, 0)

    for slot in range(2):
        @pl.when(i % 2 == slot)
        def _():
            @pl.when(i + 1 < n)
            def _():
                gather_start(i + 1, 1 - slot)

            combine(slot)


def _combine(x, meta, ys, dest, *, tb):
    t, d = x.shape
    return pl.pallas_call(
        functools.partial(_combine_kernel, tb=tb, t=t),
        grid_spec=pltpu.PrefetchScalarGridSpec(
            num_scalar_prefetch=1,
            grid=(t // tb,),
            in_specs=[
                pl.BlockSpec((tb, d), lambda i, dst: (i, 0)),
                pl.BlockSpec((tb, LANES), lambda i, dst: (i, 0)),
                pl.BlockSpec(memory_space=pl.ANY),
            ],
            out_specs=pl.BlockSpec((tb, d), lambda i, dst: (i, 0)),
            scratch_shapes=[pltpu.VMEM((2, 2, tb, d), F32), pltpu.SemaphoreType.DMA((2,))],
        ),
        out_shape=jax.ShapeDtypeStruct((t, d), F32),
        compiler_params=_cparams(("arbitrary",)),
        name="moe_combine",
    )(dest, x, meta, ys)


def _moe(x, g, router_w, router_b, wg_all, wu_all, wd_all, j, *, tb_route, tf):
    t, d = x.shape
    w_pad = jnp.pad(router_w, ((0, 0), (0, LANES - N_EXPERTS)))
    b_pad = jnp.pad(router_b, (0, LANES - N_EXPERTS))[None, :]
    meta, cnt = _router(x, g, w_pad, b_pad, tm=tb_route)
    dest, src, tile_expert, n_sub, sub_base, n_tiles = _moe_plan(meta, cnt, t)
    ys = _experts(x, g, wg_all, wu_all, wd_all, j, tile_expert, n_sub, sub_base, src, n_tiles, tf=tf)
    return _combine(x, meta, ys, dest, tb=min(256, t))


def _ple_kernel(x_ref, g_ref, wg_ref, p_ref, wp_ref, gn_ref, o_ref, hn_ref, wgb_scr, wpb_scr):
    @pl.when(pl.program_id(0) == 0)
    def _():
        wgb_scr[...] = wg_ref[...].astype(BF16)
        wpb_scr[...] = wp_ref[...].astype(BF16)

    x = x_ref[...]
    gate = jax.nn.sigmoid(_dot(_rms(x, g_ref[...]).astype(BF16), wgb_scr[...]))
    x_new = x + gate * _dot(p_ref[...].astype(BF16), wpb_scr[...])
    o_ref[...] = x_new
    hn_ref[...] = _rms(x_new, gn_ref[...]).astype(BF16)


def _ple(x, g, wg_all, p_all, wp_all, layer, g_next, *, tm):
    t, d = x.shape
    pd = p_all.shape[-1]
    row_spec = pl.BlockSpec((tm, d), lambda m: (m, 0))
    gain_spec = pl.BlockSpec((1, d), lambda m: (0, 0))
    return pl.pallas_call(
        _ple_kernel,
        grid=(t // tm,),
        in_specs=[
            row_spec,
            gain_spec,
            _resident_weight_spec((None, d, d), (layer, 0, 0)),
            pl.BlockSpec((None, tm, pd), lambda m: (layer, m, 0)),
            _resident_weight_spec((None, pd, d), (layer, 0, 0)),
            gain_spec,
        ],
        out_specs=[row_spec, row_spec],
        out_shape=[jax.ShapeDtypeStruct((t, d), F32), jax.ShapeDtypeStruct((t, d), BF16)],
        scratch_shapes=[pltpu.VMEM((d, d), BF16), pltpu.VMEM((pd, d), BF16)],
        compiler_params=_cparams(("arbitrary",)),
        name="ple_gate",
    )(x, g, wg_all, p_all, wp_all, g_next)


def _rope_tables(seq):
    rows = seq // GRID_W
    row = jnp.broadcast_to(jnp.arange(rows, dtype=F32)[:, None], (rows, GRID_W)).reshape(seq)
    col = jnp.broadcast_to(jnp.arange(GRID_W, dtype=F32)[None, :], (rows, GRID_W)).reshape(seq)
    axis_dim = ATTN_HEAD_DIM // 2
    inv_freq = ROPE_THETA ** (-jnp.arange(0, axis_dim, 2, dtype=F32) / axis_dim)
    ar, ac = row[:, None] * inv_freq, col[:, None] * inv_freq
    cos_t = jnp.concatenate([jnp.cos(ar), jnp.cos(ar), jnp.cos(ac), jnp.cos(ac)], axis=-1)
    sin_t = jnp.concatenate([-jnp.sin(ar), jnp.sin(ar), -jnp.sin(ac), jnp.sin(ac)], axis=-1)
    return cos_t, sin_t


def kernel(x, p, norm_mix_g, w_in, conv_w, conv_b, b_igate, b_fgate, mlstm_norm_g, q_norm_g, k_norm_g, w_mlstm_up, w_attn_up, w_out, norm_ffn_g, ffn_w_gate, ffn_w_up, ffn_w_down, moe_router, moe_router_b, moe_w_gate, moe_w_up, moe_w_down, norm_ple_g, w_ple_gate, w_ple_proj):
    b, s, d = x.shape
    depth = w_in.shape[0]
    t = b * s
    col = _col_layout(d)
    cos_t, sin_t = _rope_tables(s)
    xt = x.reshape(t, d)
    p_all = p.reshape(depth, t, p.shape[-1])
    w_in_t = jnp.swapaxes(w_in, 1, 2)

    tm_big, tm_half = min(1024, t), min(512, t)
    tn = min(512, d)
    tf = 256

    h_mix = _rms_cast(xt, norm_mix_g[0][None, :], tm=tm_half)
    for i in range(depth):
        z, gates = _in_proj(h_mix, w_in_t, i, tm=min(2048, t), tn=tn)
        z3 = z.reshape(b, s, col["total"])
        qk3 = _conv_silu(z3, conv_w[i], conv_b[i][None, :], col0=col["qk"], tc=256)
        gate_bias = jnp.concatenate(
            [b_igate[i].reshape(-1), b_fgate[i].reshape(-1), jnp.zeros((LANES - N_GATE_COLS,), F32)])[None, :]
        hf, hb = _mlstm_scan(qk3, z3, gates.reshape(b, s, LANES), gate_bias, v_col0=col["mv"])
        q3, k3, vt3 = _qk_prep(z3, cos_t, sin_t, q_norm_g[i][None, :], k_norm_g[i][None, :],
                               aq_col0=col["aq"], ak_col0=col["ak"], av_col0=col["av"], ts=min(512, s))
        ha = _flash_gqa(q3, k3, vt3, _score_bound(q_norm_g[i], k_norm_g[i]), tq=min(256, s), tk=min(2048, s))
        hm = _mlstm_out(hf.reshape(t, MLSTM_WIDTH), hb.reshape(t, MLSTM_WIDTH), z,
                        mlstm_norm_g[i].reshape(1, MLSTM_WIDTH), mo_col0=col["mo"], tm=tm_half)
        y = _merge(hm, ha.reshape(t, ATTN_WIDTH), z, w_mlstm_up, w_attn_up, i,
                   gm_col0=col["gm"], ga_col0=col["ga"], tm=tm_half)
        xt = _out_proj(y, w_out, i, xt, tm=tm_half)

        j = i // 2
        g_ffn = norm_ffn_g[i][None, :]
        if i % 2 == 0:
            xt = _ffn(xt, g_ffn, ffn_w_gate, ffn_w_up, ffn_w_down, j, tm=tm_big, tf=tf)
        else:
            xt = _moe(xt, g_ffn, moe_router[j], moe_router_b[j], moe_w_gate, moe_w_up, moe_w_down, j,
                      tb_route=tm_half, tf=tf)

        g_next = norm_mix_g[min(i + 1, depth - 1)][None, :]
        xt, h_mix = _ple(xt, norm_ple_g[i][None, :], w_ple_gate, p_all, w_ple_proj, i, g_next, tm=tm_half)

    return xt.reshape(b, s, d)
```

```python
import functools
import itertools

import jax
import jax.numpy as jnp
from jax import lax
from jax.experimental import pallas as pl
from jax.experimental.pallas import tpu as pltpu

F32 = jnp.float32
BF16 = jnp.bfloat16

GRID_W = 64
N_MLSTM_HEADS = 4
MLSTM_QK_DIM = 128
MLSTM_V_DIM = 256
MLSTM_QK_WIDTH = N_MLSTM_HEADS * MLSTM_QK_DIM
MLSTM_WIDTH = N_MLSTM_HEADS * MLSTM_V_DIM
MLSTM_CHUNK = 256
N_DIRS = 2
N_ATTN_HEADS = 8
N_KV_HEADS = 2
ATTN_HEAD_DIM = 128
ATTN_WIDTH = N_ATTN_HEADS * ATTN_HEAD_DIM
KV_WIDTH = N_KV_HEADS * ATTN_HEAD_DIM
ROPE_THETA = 10000.0
N_EXPERTS = 8
EPS = 1e-6

LANES = 128
SUBLANES = 8
VMEM_LIMIT_BYTES = 56 * 2**20

N_GATE_COLS = 2 * N_DIRS * N_MLSTM_HEADS


def _col_layout(d_model):
    names = ("qk", "mv", "mo", "aq", "gm", "ga", "ak", "av")
    widths = (2 * MLSTM_QK_WIDTH, MLSTM_WIDTH, MLSTM_WIDTH, ATTN_WIDTH, d_model, d_model, KV_WIDTH, KV_WIDTH)
    off, out = 0, {}
    for n, w in zip(names, widths):
        out[n] = off
        off += w
    out["total"] = off
    return out


N_HEAD_COLS = 2 * MLSTM_QK_WIDTH + MLSTM_WIDTH


def _cparams(semantics):
    return pltpu.CompilerParams(dimension_semantics=semantics, vmem_limit_bytes=VMEM_LIMIT_BYTES)


def _rms(x, g):
    ms = jnp.mean(x * x, axis=-1, keepdims=True)
    return x * lax.rsqrt(ms + EPS) * g


def _dot(a, b):
    return jnp.dot(a, b, preferred_element_type=F32)


def _dot_nt(a, b):
    return lax.dot_general(a, b, (((1,), (1,)), ((), ())), preferred_element_type=F32)


def _dot_tn(a, b):
    return lax.dot_general(a, b, (((0,), (0,)), ((), ())), preferred_element_type=F32)


def _rms_cast_kernel(x_ref, g_ref, h_ref):
    h_ref[...] = _rms(x_ref[...], g_ref[...]).astype(BF16)


def _rms_cast(x, g, *, tm):
    t, d = x.shape
    row_spec = pl.BlockSpec((tm, d), lambda m: (m, 0))
    return pl.pallas_call(
        _rms_cast_kernel,
        grid=(t // tm,),
        in_specs=[row_spec, pl.BlockSpec((1, d), lambda m: (0, 0))],
        out_specs=row_spec,
        out_shape=jax.ShapeDtypeStruct((t, d), BF16),
        compiler_params=_cparams(("parallel",)),
        name="input_rms_norm",
    )(x, g)


def _in_proj_kernel(h_ref, wm_ref, wn_ref, wg_ref, o_ref, gates_ref, *, head_blocks, tail_blocks):
    j = pl.program_id(1)
    ng = N_GATE_COLS

    @pl.when(j < head_blocks)
    def _():
        o_ref[...] = _dot_nt(h_ref[...], wm_ref[...].astype(BF16)).astype(o_ref.dtype)

    @pl.when((j >= head_blocks) & (j < head_blocks + tail_blocks))
    def _():
        w = jnp.concatenate([wm_ref[ng:, :], wn_ref[...]], axis=0).astype(BF16)
        o_ref[...] = _dot_nt(h_ref[...], w).astype(o_ref.dtype)

    @pl.when(j == head_blocks + tail_blocks)
    def _():
        pad = jnp.zeros((gates_ref.shape[1] - ng, wg_ref.shape[1]), F32)
        w = jnp.concatenate([wg_ref[...], pad], axis=0).astype(BF16)
        gates_ref[...] = _dot_nt(h_ref[...], w)


def _in_proj(h, wt_all, layer, *, tm, tn):
    t, d = h.shape
    n_cols = wt_all.shape[1]
    ng = N_GATE_COLS
    tail = n_cols - N_HEAD_COLS - ng
    head_blocks, tail_blocks = N_HEAD_COLS // tn, tail // tn
    assert tail % tn == 0 and N_HEAD_COLS % tn == 0 and n_cols % ng == 0
    last_main = head_blocks + tail_blocks - 1
    pre, kv, gates2 = (MLSTM_WIDTH + ATTN_WIDTH) // tn, 2 * KV_WIDTH // tn, 2 * d // tn
    assert (2 * KV_WIDTH) % tn == 0 and pre + kv + gates2 == tail_blocks

    def out_block(j):
        jz = jnp.minimum(j, last_main)
        tb = jz - head_blocks
        moved = jnp.where(tb < pre, tb, jnp.where(tb < pre + kv, tb + gates2, tb - kv))
        return jnp.where(tb >= 0, head_blocks + moved, jz)

    return pl.pallas_call(
        functools.partial(_in_proj_kernel, head_blocks=head_blocks, tail_blocks=tail_blocks),
        grid=(t // tm, head_blocks + tail_blocks + 1),
        in_specs=[
            pl.BlockSpec((tm, d), lambda m, j: (m, 0)),
            pl.BlockSpec((None, tn, d), lambda m, j: (layer, jnp.minimum(j, last_main), 0)),
            pl.BlockSpec((None, ng, d), lambda m, j: (layer, jnp.minimum((j + 1) * (tn // ng), n_cols // ng - 1), 0)),
            pl.BlockSpec((None, ng, d), lambda m, j: (layer, N_HEAD_COLS // ng, 0)),
        ],
        out_specs=[pl.BlockSpec((tm, tn), lambda m, j: (m, out_block(j))),
                   pl.BlockSpec((tm, LANES), lambda m, j: (m, 0))],
        out_shape=[jax.ShapeDtypeStruct((t, N_HEAD_COLS + tail), BF16), jax.ShapeDtypeStruct((t, LANES), F32)],
        compiler_params=_cparams(("parallel", "arbitrary")),
        name="in_proj",
    )(h, wt_all, wt_all, wt_all)


def _conv_silu_kernel(z_ref, w_ref, b_ref, o_ref, *, q_blocks, q_scale):
    x = z_ref[0].astype(F32)
    s = x.shape[0]
    row = lax.broadcasted_iota(jnp.int32, x.shape, 0)
    prev = jnp.where(row == 0, 0.0, pltpu.roll(x, 1, 0))
    nxt = jnp.where(row == s - 1, 0.0, pltpu.roll(x, s - 1, 0))
    y = prev * w_ref[0:1, :] + x * w_ref[1:2, :] + nxt * w_ref[2:3, :] + b_ref[...]
    y = y * jax.nn.sigmoid(y)
    scale = jnp.where(pl.program_id(1) < q_blocks, q_scale, 1.0).astype(F32)
    o_ref[0] = (y * scale).astype(o_ref.dtype)


def _conv_silu(z3, conv_w, conv_b, *, col0, tc):
    b, s, _ = z3.shape
    width = conv_w.shape[1]
    kern = functools.partial(_conv_silu_kernel, q_blocks=MLSTM_QK_WIDTH // tc, q_scale=MLSTM_QK_DIM ** -0.5)
    return pl.pallas_call(
        kern,
        grid=(b, width // tc),
        in_specs=[
            pl.BlockSpec((1, s, tc), lambda i, j: (i, 0, col0 // tc + j)),
            pl.BlockSpec((3, tc), lambda i, j: (0, j)),
            pl.BlockSpec((1, tc), lambda i, j: (0, j)),
        ],
        out_specs=pl.BlockSpec((1, s, tc), lambda i, j: (i, 0, j)),
        out_shape=jax.ShapeDtypeStruct((b, s, width), BF16),
        compiler_params=_cparams(("parallel", "parallel")),
        name="mlstm_conv_silu",
    )(z3, conv_w, conv_b)


def _log_sigmoid(x):
    return jnp.minimum(x, 0.0) - jnp.log1p(jnp.exp(-jnp.abs(x)))


def _lane_col(x, idx):
    lane = lax.broadcasted_iota(jnp.int32, x.shape, 1)
    return jnp.sum(jnp.where(lane == idx, x, 0.0), axis=-1, keepdims=True)


def _mlstm_kernel(qkf_ref, qkb_ref, vf_ref, vb_ref, gf_ref, gb_ref, bias_ref, hf_ref, hb_ref, c_scr, m_scr):
    L = MLSTM_CHUNK
    dk, dv, nh = MLSTM_QK_DIM, MLSTM_V_DIM, N_MLSTM_HEADS

    @pl.when(pl.program_id(0) == 0)
    def _():
        c_scr[...] = jnp.zeros_like(c_scr)
        m_scr[...] = jnp.zeros_like(m_scr)

    r_i = lax.broadcasted_iota(jnp.int32, (L, L), 0)
    c_i = lax.broadcasted_iota(jnp.int32, (L, L), 1)
    lane = lax.broadcasted_iota(jnp.int32, (L, LANES), 1)
    ones_blk = jnp.where(lane == 0, 1.0, 0.0).astype(F32)

    dirs = ((qkf_ref, vf_ref, gf_ref, hf_ref), (qkb_ref, vb_ref, gb_ref, hb_ref))
    for bi, (d, (qk_ref, v_ref, g_ref, h_ref)) in itertools.product(range(qkf_ref.shape[0]), enumerate(dirs)):
        mask = (c_i <= r_i) if d == 0 else (c_i >= r_i)
        g = g_ref[bi] + bias_ref[...]
        gp = jnp.where(lane < N_GATE_COLS // 2, g, _log_sigmoid(g))
        gp_t = gp.T
        bc = jnp.dot(mask.astype(F32), gp, precision=lax.Precision.HIGHEST, preferred_element_type=F32)
        bc_t = bc.T
        end_row = L - 1 if d == 0 else 0
        for h in range(nh):
            ci = (bi * N_DIRS + d) * nh + h
            ch_i, ch_f = d * nh + h, N_GATE_COLS // 2 + d * nh + h
            i_row = gp_t[ch_i:ch_i + 1, :]
            bc_row = bc_t[ch_f:ch_f + 1, :]
            i_col = _lane_col(gp, ch_i)
            bc_col = _lane_col(bc, ch_f)
            m_prev = m_scr[ci, 0:1, 0:1]
            c_prev = c_scr[ci]

            a_col = bc_col + m_prev
            dm = jnp.where(mask, bc_col - bc_row + i_row, -jnp.inf)
            m_t = jnp.maximum(a_col, jnp.max(dm, axis=-1, keepdims=True))
            w_inter = jnp.exp(a_col - m_t)
            w_intra = jnp.exp(dm - m_t)

            q = qk_ref[bi, :, h * dk:(h + 1) * dk]
            k = qk_ref[bi, :, MLSTM_QK_WIDTH + h * dk:MLSTM_QK_WIDTH + (h + 1) * dk]
            v = v_ref[bi, :, h * dv:(h + 1) * dv].astype(F32)
            v_ext = jnp.concatenate([v, ones_blk], axis=1)
            p = (_dot_nt(q, k) * w_intra).astype(BF16)
            num_ext = w_inter * _dot(q, c_prev.astype(BF16)) + _dot(p, v_ext.astype(BF16))
            den = num_ext[:, dv:dv + 1]
            h_out = num_ext[:, :dv] / jnp.maximum(jnp.abs(den), jnp.exp(-m_t))
            h_ref[bi, :, h * dv:(h + 1) * dv] = h_out.astype(h_ref.dtype)

            b_end = bc_col[end_row:end_row + 1, :]
            g_col = b_end - bc_col + i_col
            m_new = jnp.maximum(b_end + m_prev, jnp.max(g_col, axis=0, keepdims=True))
            decay = jnp.exp(b_end + m_prev - m_new)
            ws = jnp.exp(g_col - m_new)
            c_scr[ci] = decay * c_prev + _dot_tn(k, (ws * v_ext).astype(BF16))
            m_scr[ci] = jnp.broadcast_to(m_new, m_scr.shape[1:])


def _mlstm_scan(qk3, z3, gates3, gate_bias, *, v_col0):
    b, s, _ = qk3.shape
    L = MLSTM_CHUNK
    assert s % L == 0
    nc = s // L
    n_chain = b * N_DIRS * N_MLSTM_HEADS
    vb = v_col0 // MLSTM_WIDTH
    qk_w = 2 * MLSTM_QK_WIDTH
    return pl.pallas_call(
        _mlstm_kernel,
        grid=(nc,),
        in_specs=[
            pl.BlockSpec((b, L, qk_w), lambda c: (0, c, 0)),
            pl.BlockSpec((b, L, qk_w), lambda c: (0, nc - 1 - c, 0)),
            pl.BlockSpec((b, L, MLSTM_WIDTH), lambda c: (0, c, vb)),
            pl.BlockSpec((b, L, MLSTM_WIDTH), lambda c: (0, nc - 1 - c, vb)),
            pl.BlockSpec((b, L, LANES), lambda c: (0, c, 0)),
            pl.BlockSpec((b, L, LANES), lambda c: (0, nc - 1 - c, 0)),
            pl.BlockSpec((1, LANES), lambda c: (0, 0)),
        ],
        out_specs=[
            pl.BlockSpec((b, L, MLSTM_WIDTH), lambda c: (0, c, 0)),
            pl.BlockSpec((b, L, MLSTM_WIDTH), lambda c: (0, nc - 1 - c, 0)),
        ],
        out_shape=[jax.ShapeDtypeStruct((b, s, MLSTM_WIDTH), BF16)] * 2,
        scratch_shapes=[
            pltpu.VMEM((n_chain, MLSTM_QK_DIM, MLSTM_V_DIM + LANES), F32),
            pltpu.VMEM((n_chain, SUBLANES, LANES), F32),
        ],
        compiler_params=_cparams(("arbitrary",)),
        name="mlstm_scan",
    )(qk3, qk3, z3, z3, gates3, gates3, gate_bias)


LOG2_E = 1.4426950408889634


def _qk_prep_kernel(aq_ref, ak_ref, av_ref, cos_ref, sin_ref, gq_ref, gk_ref, q_ref, k_ref, vt_ref):
    cos, sin = cos_ref[...], sin_ref[...]
    hd = ATTN_HEAD_DIM
    lane = lax.broadcasted_iota(jnp.int32, cos.shape, 1)
    first_half = (lane % (hd // 2)) < (hd // 4)

    def prep(x, g, scale):
        n = _rms(x.astype(F32), g)
        partner = jnp.where(first_half, pltpu.roll(n, hd - hd // 4, 1), pltpu.roll(n, hd // 4, 1))
        return ((n * cos + partner * sin) * scale).astype(BF16)

    for h in range(N_ATTN_HEADS):
        q_ref[0, :, h * hd:(h + 1) * hd] = prep(aq_ref[0, :, h * hd:(h + 1) * hd], gq_ref[...], hd ** -0.5 * LOG2_E)
    for h in range(N_KV_HEADS):
        k_ref[0, :, h * hd:(h + 1) * hd] = prep(ak_ref[0, :, h * hd:(h + 1) * hd], gk_ref[...], 1.0)
    vt_ref[0] = av_ref[0].astype(F32).T.astype(BF16)


def _qk_prep(z3, cos_t, sin_t, gq, gk, *, aq_col0, ak_col0, av_col0, ts):
    b, s, _ = z3.shape
    return pl.pallas_call(
        _qk_prep_kernel,
        grid=(b, s // ts),
        in_specs=[
            pl.BlockSpec((1, ts, ATTN_WIDTH), lambda i, j: (i, j, aq_col0 // ATTN_WIDTH)),
            pl.BlockSpec((1, ts, KV_WIDTH), lambda i, j: (i, j, ak_col0 // KV_WIDTH)),
            pl.BlockSpec((1, ts, KV_WIDTH), lambda i, j: (i, j, av_col0 // KV_WIDTH)),
            pl.BlockSpec((ts, ATTN_HEAD_DIM), lambda i, j: (j, 0)),
            pl.BlockSpec((ts, ATTN_HEAD_DIM), lambda i, j: (j, 0)),
            pl.BlockSpec((1, ATTN_HEAD_DIM), lambda i, j: (0, 0)),
            pl.BlockSpec((1, ATTN_HEAD_DIM), lambda i, j: (0, 0)),
        ],
        out_specs=[
            pl.BlockSpec((1, ts, ATTN_WIDTH), lambda i, j: (i, j, 0)),
            pl.BlockSpec((1, ts, KV_WIDTH), lambda i, j: (i, j, 0)),
            pl.BlockSpec((1, KV_WIDTH, ts), lambda i, j: (i, 0, j)),
        ],
        out_shape=[jax.ShapeDtypeStruct((b, s, ATTN_WIDTH), BF16), jax.ShapeDtypeStruct((b, s, KV_WIDTH), BF16),
                   jax.ShapeDtypeStruct((b, KV_WIDTH, s), BF16)],
        compiler_params=_cparams(("parallel", "parallel")),
        name="attn_qk_norm_rope",
    )(z3, z3, z3, cos_t, sin_t, gq, gk)


SOFTMAX_SHIFT_LIMIT = 60.0


def _flash_kernel(bound_ref, q_ref, k_ref, vt_ref, o_ref, *, grp, tk):
    hd = ATTN_HEAD_DIM
    tq = q_ref.shape[1]
    s_len = k_ref.shape[1]
    rows = grp * tq
    bound = bound_ref[0, 0]
    q = jnp.concatenate([q_ref[0, :, g * hd:(g + 1) * hd] for g in range(grp)], axis=0)
    chunks = [slice(c * tk, (c + 1) * tk) for c in range(s_len // tk)]

    def store(acc, l):
        o = (acc / l).T
        for g in range(grp):
            o_ref[0, :, g * hd:(g + 1) * hd] = o[g * tq:(g + 1) * tq, :].astype(o_ref.dtype)

    @pl.when(bound <= SOFTMAX_SHIFT_LIMIT)
    def _():
        l = jnp.zeros((1, rows), F32)
        acc = jnp.zeros((hd, rows), F32)
        for ck in chunks:
            p = jnp.exp2(_dot_nt(k_ref[0, ck, :], q) - bound)
            l = l + jnp.sum(p, axis=0, keepdims=True)
            acc = acc + _dot(vt_ref[0, :, ck], p.astype(BF16))
        store(acc, l)

    @pl.when(bound > SOFTMAX_SHIFT_LIMIT)
    def _():
        m = jnp.full((1, rows), -jnp.inf, F32)
        l = jnp.zeros((1, rows), F32)
        acc = jnp.zeros((hd, rows), F32)
        for ck in chunks:
            st = _dot_nt(k_ref[0, ck, :], q)
            m_new = jnp.maximum(m, jnp.max(st, axis=0, keepdims=True))
            alpha = jnp.exp2(m - m_new)
            p = jnp.exp2(st - m_new)
            l = alpha * l + jnp.sum(p, axis=0, keepdims=True)
            acc = alpha * acc + _dot(vt_ref[0, :, ck], p.astype(BF16))
            m = m_new
        store(acc, l)


def _score_bound(gq, gk):
    scale = ATTN_HEAD_DIM ** -0.5 * LOG2_E
    return (1.02 * ATTN_HEAD_DIM * scale * jnp.max(jnp.abs(gq)) * jnp.max(jnp.abs(gk))).reshape(1, 1).astype(F32)


def _flash_gqa(q3, k3, vt3, bound, *, tq, tk):
    b, s, _ = q3.shape
    hd = ATTN_HEAD_DIM
    grp = N_ATTN_HEADS // N_KV_HEADS
    return pl.pallas_call(
        functools.partial(_flash_kernel, grp=grp, tk=tk),
        grid=(b, N_KV_HEADS, s // tq),
        in_specs=[
            pl.BlockSpec(memory_space=pltpu.SMEM),
            pl.BlockSpec((1, tq, grp * hd), lambda i, h, qi: (i, qi, h)),
            pl.BlockSpec((1, s, hd), lambda i, h, qi: (i, 0, h)),
            pl.BlockSpec((1, hd, s), lambda i, h, qi: (i, h, 0)),
        ],
        out_specs=pl.BlockSpec((1, tq, grp * hd), lambda i, h, qi: (i, qi, h)),
        out_shape=jax.ShapeDtypeStruct((b, s, ATTN_WIDTH), BF16),
        compiler_params=_cparams(("parallel", "parallel", "parallel")),
        name="gqa_flash",
    )(bound, q3, k3, vt3)


def _mlstm_out_kernel(hf_ref, hb_ref, mo_ref, ng_ref, hm_ref):
    dv = MLSTM_V_DIM
    for h in range(N_MLSTM_HEADS):
        sl = slice(h * dv, (h + 1) * dv)
        hm = _rms(hf_ref[:, sl].astype(F32) + hb_ref[:, sl].astype(F32), ng_ref[:, sl])
        hm_ref[:, sl] = (hm * jax.nn.sigmoid(mo_ref[:, sl].astype(F32))).astype(BF16)


def _mlstm_out(hf, hb, z, ng, *, mo_col0, tm):
    t, w = hf.shape
    row_spec = pl.BlockSpec((tm, w), lambda m: (m, 0))
    return pl.pallas_call(
        _mlstm_out_kernel,
        grid=(t // tm,),
        in_specs=[row_spec, row_spec, pl.BlockSpec((tm, w), lambda m: (m, mo_col0 // w)),
                  pl.BlockSpec((1, w), lambda m: (0, 0))],
        out_specs=row_spec,
        out_shape=jax.ShapeDtypeStruct((t, w), BF16),
        compiler_params=_cparams(("parallel",)),
        name="mlstm_out_norm",
    )(hf, hb, z, ng)


def _merge_kernel(hm_ref, ha_ref, gm_ref, ga_ref, wm_ref, wa_ref, y_ref, wmb_scr, wab_scr):
    @pl.when(pl.program_id(0) == 0)
    def _():
        wmb_scr[...] = wm_ref[...].astype(BF16)
        wab_scr[...] = wa_ref[...].astype(BF16)

    ym = _dot(hm_ref[...], wmb_scr[...])
    ya = _dot(ha_ref[...], wab_scr[...])
    gate_m = jax.nn.sigmoid(gm_ref[...].astype(F32))
    gate_a = jax.nn.sigmoid(ga_ref[...].astype(F32))
    y_ref[...] = (gate_m * ym + gate_a * ya).astype(y_ref.dtype)


def _merge(hm, ha, z, wm_all, wa_all, layer, *, gm_col0, ga_col0, tm):
    t, w = hm.shape
    d = wm_all.shape[2]
    act_spec = pl.BlockSpec((tm, w), lambda m: (m, 0))
    w_spec = _resident_weight_spec((None, w, d), (layer, 0, 0))
    return pl.pallas_call(
        _merge_kernel,
        grid=(t // tm,),
        in_specs=[
            act_spec,
            act_spec,
            pl.BlockSpec((tm, d), lambda m: (m, gm_col0 // d)),
            pl.BlockSpec((tm, d), lambda m: (m, ga_col0 // d)),
            w_spec,
            w_spec,
        ],
        out_specs=pl.BlockSpec((tm, d), lambda m: (m, 0)),
        out_shape=jax.ShapeDtypeStruct((t, d), BF16),
        scratch_shapes=[pltpu.VMEM((w, d), BF16), pltpu.VMEM((w, d), BF16)],
        compiler_params=_cparams(("arbitrary",)),
        name="branch_merge",
    )(hm, ha, z, z, wm_all, wa_all)


def _resident_weight_spec(shape, index):
    return pl.BlockSpec(shape, lambda *_: index, pipeline_mode=pl.Buffered(1))


def _out_proj_kernel(y_ref, w_ref, x_ref, o_ref, wb_scr):
    @pl.when(pl.program_id(0) == 0)
    def _():
        wb_scr[...] = w_ref[...].astype(BF16)

    o_ref[...] = x_ref[...] + _dot(y_ref[...], wb_scr[...])


def _out_proj(y, w_all, layer, x, *, tm):
    t, d = x.shape
    k = y.shape[1]
    return pl.pallas_call(
        _out_proj_kernel,
        grid=(t // tm,),
        in_specs=[
            pl.BlockSpec((tm, k), lambda m: (m, 0)),
            _resident_weight_spec((None, k, d), (layer, 0, 0)),
            pl.BlockSpec((tm, d), lambda m: (m, 0)),
        ],
        out_specs=pl.BlockSpec((tm, d), lambda m: (m, 0)),
        out_shape=jax.ShapeDtypeStruct((t, d), F32),
        scratch_shapes=[pltpu.VMEM((k, d), BF16)],
        compiler_params=_cparams(("arbitrary",)),
        name="out_proj_residual",
    )(y, w_all, x)


def _swiglu_partial(h, wg_ref, wu_ref, wd_ref):
    a = _dot(h, wg_ref[...].astype(BF16))
    u = _dot(h, wu_ref[...].astype(BF16))
    t = a * jax.nn.sigmoid(a) * u
    return _dot(t.astype(BF16), wd_ref[...].astype(BF16))


def _ffn_kernel(x_ref, g_ref, wg_ref, wu_ref, wd_ref, o_ref, h_scr):
    @pl.when(pl.program_id(1) == 0)
    def _():
        x = x_ref[...]
        h_scr[...] = _rms(x, g_ref[...]).astype(BF16)
        o_ref[...] = x

    o_ref[...] += _swiglu_partial(h_scr[...], wg_ref, wu_ref, wd_ref)


def _ffn(x, g, wg_all, wu_all, wd_all, j, *, tm, tf):
    t, d = x.shape
    ff = wg_all.shape[-1]
    w_in_spec = pl.BlockSpec((None, d, tf), lambda m, f: (j, 0, f))
    return pl.pallas_call(
        _ffn_kernel,
        grid=(t // tm, ff // tf),
        in_specs=[
            pl.BlockSpec((tm, d), lambda m, f: (m, 0)),
            pl.BlockSpec((1, d), lambda m, f: (0, 0)),
            w_in_spec,
            w_in_spec,
            pl.BlockSpec((None, tf, d), lambda m, f: (j, f, 0)),
        ],
        out_specs=pl.BlockSpec((tm, d), lambda m, f: (m, 0)),
        out_shape=jax.ShapeDtypeStruct((t, d), F32),
        scratch_shapes=[pltpu.VMEM((tm, d), BF16)],
        compiler_params=_cparams(("parallel", "arbitrary")),
        name="dense_swiglu",
    )(x, g, wg_all, wu_all, wd_all)


MOE_ROW_TILE = 2560
MOE_SUB_TILE = 256
MOE_BULK_SUBS = 8
META_E1, META_E2, META_P1, META_P2, META_R1, META_R2 = range(6)


def _router_kernel(x_ref, g_ref, w_ref, b_ref, meta_ref, cnt_ref, carry_scr):
    @pl.when(pl.program_id(0) == 0)
    def _():
        carry_scr[...] = jnp.zeros_like(carry_scr)

    h = _rms(x_ref[...], g_ref[...])
    logits = jnp.dot(h, w_ref[...], precision=lax.Precision.HIGHEST, preferred_element_type=F32) + b_ref[...]
    tb = logits.shape[0]
    lane = lax.broadcasted_iota(jnp.int32, logits.shape, 1)
    logits = jnp.where(lane < N_EXPERTS, logits, -jnp.inf)
    v1 = jnp.max(logits, axis=-1, keepdims=True)
    i1 = jnp.min(jnp.where(logits == v1, lane, LANES), axis=-1, keepdims=True)
    rest = jnp.where(lane == i1, -jnp.inf, logits)
    v2 = jnp.max(rest, axis=-1, keepdims=True)
    i2 = jnp.min(jnp.where(rest == v2, lane, LANES), axis=-1, keepdims=True)
    e2 = jnp.exp(v2 - v1)
    p1 = 1.0 / (1.0 + e2)
    p2 = e2 / (1.0 + e2)
    sel = ((lane == i1) | (lane == i2)).astype(F32)
    r_i = lax.broadcasted_iota(jnp.int32, (tb, tb), 0)
    c_i = lax.broadcasted_iota(jnp.int32, (tb, tb), 1)
    earlier = (c_i < r_i).astype(BF16)
    rank = carry_scr[0:1, :] + _dot(earlier, sel.astype(BF16))
    r1 = jnp.sum(jnp.where(lane == i1, rank, 0.0), axis=-1, keepdims=True)
    r2 = jnp.sum(jnp.where(lane == i2, rank, 0.0), axis=-1, keepdims=True)
    meta = jnp.zeros_like(logits)
    for idx, val in ((META_E1, i1.astype(F32)), (META_E2, i2.astype(F32)), (META_P1, p1), (META_P2, p2),
                     (META_R1, r1), (META_R2, r2)):
        meta = jnp.where(lane == idx, val, meta)
    meta_ref[...] = meta
    total = carry_scr[0:1, :] + jnp.sum(sel, axis=0, keepdims=True)
    carry_scr[...] = jnp.broadcast_to(total, carry_scr.shape)
    cnt_ref[...] = jnp.broadcast_to(total, cnt_ref.shape)


def _router(x, g, w_pad, b_pad, *, tm):
    t, d = x.shape
    return pl.pallas_call(
        _router_kernel,
        grid=(t // tm,),
        in_specs=[
            pl.BlockSpec((tm, d), lambda m: (m, 0)),
            pl.BlockSpec((1, d), lambda m: (0, 0)),
            pl.BlockSpec((d, LANES), lambda m: (0, 0)),
            pl.BlockSpec((1, LANES), lambda m: (0, 0)),
        ],
        out_specs=[pl.BlockSpec((tm, LANES), lambda m: (m, 0)), pl.BlockSpec((SUBLANES, LANES), lambda m: (0, 0))],
        out_shape=[jax.ShapeDtypeStruct((t, LANES), F32), jax.ShapeDtypeStruct((SUBLANES, LANES), F32)],
        scratch_shapes=[pltpu.VMEM((SUBLANES, LANES), F32)],
        compiler_params=_cparams(("arbitrary",)),
        name="moe_router",
    )(x, g, w_pad, b_pad)


def _moe_plan(meta, cnt, t):
    tile, sub = MOE_ROW_TILE, MOE_SUB_TILE
    n_tiles = 2 * t // tile + N_EXPERTS
    e1 = meta[:, META_E1].astype(jnp.int32)
    e2 = meta[:, META_E2].astype(jnp.int32)
    r1 = meta[:, META_R1].astype(jnp.int32)
    r2 = meta[:, META_R2].astype(jnp.int32)
    counts = cnt[0, :N_EXPERTS].astype(jnp.int32)
    tiles_e = (counts + tile - 1) // tile
    tile_end = jnp.cumsum(tiles_e)
    tile_start = tile_end - tiles_e
    row_start = tile_start * tile
    dest = jnp.concatenate([row_start[e1] + r1, row_start[e2] + r2]).astype(jnp.int32)
    m = jnp.arange(n_tiles, dtype=jnp.int32)
    live = m < tile_end[-1]
    m_eff = jnp.where(live, m, jnp.maximum(tile_end[-1] - 1, 0))
    tile_expert = jnp.minimum(jnp.searchsorted(tile_end, m_eff, side="right"), N_EXPERTS - 1).astype(jnp.int32)
    rows_left = counts[tile_expert] - (m_eff - tile_start[tile_expert]) * tile
    n_sub = jnp.where(live, jnp.clip((rows_left + sub - 1) // sub, 0, tile // sub), 0).astype(jnp.int32)
    sub_base = (jnp.cumsum(n_sub) - n_sub).astype(jnp.int32)
    tok = jnp.arange(t, dtype=jnp.int32)
    src_pos = sub_base[dest // tile] * sub + dest % tile
    n_src = (2 * t // sub + N_EXPERTS) * sub
    src = jnp.zeros((n_src,), jnp.int32).at[src_pos].set(jnp.concatenate([tok, tok]))
    return dest, src, tile_expert, n_sub, sub_base, n_tiles


def _expert_kernel(te_ref, ns_ref, sbase_ref, src_ref, x_hbm, g_ref, wg_ref, wu_ref, wd_ref, ys_hbm,
                   xbuf, h_scr, acc, gsem, zsem, osem):
    del te_ref
    m, f = pl.program_id(0), pl.program_id(1)
    n_m, n_f = pl.num_programs(0), pl.num_programs(1)
    tile, sub = MOE_ROW_TILE, MOE_SUB_TILE
    n_live = ns_ref[m]
    subs = [(sb, slice(sb * sub, (sb + 1) * sub)) for sb in range(tile // sub)]

    def gather_start(sb, slot):
        def start(i, carry):
            for prio in range(2):
                r = 2 * i + prio
                tok = src_ref[(sbase_ref[m] + sb) * sub + r]
                pltpu.make_async_copy(
                    x_hbm.at[pl.ds(tok, 1)], xbuf.at[slot, pl.ds(r, 1)], gsem.at[slot]).start(priority=prio)
            return carry

        lax.fori_loop(0, sub // 2, start, 0, unroll=4)

    def result_copy(sb, rows, tile_idx):
        return pltpu.make_async_copy(acc.at[rows], ys_hbm.at[pl.ds(tile_idx * tile + sb * sub, sub)], osem)

    def zero_copy(sb):
        return pltpu.make_async_copy(xbuf.at[0], ys_hbm.at[pl.ds(m * tile + sb * sub, sub)], zsem)

    @pl.when(f == 0)
    def _():
        @pl.when(n_live > 0)
        def _():
            gather_start(0, 0)

        for sb, rows in subs:
            slot = sb % 2
            if sb + 1 < len(subs):
                @pl.when(sb + 1 < n_live)
                def _():
                    gather_start(sb + 1, 1 - slot)

            @pl.when(sb < n_live)
            def _():
                pltpu.make_async_copy(x_hbm.at[pl.ds(0, sub)], xbuf.at[slot], gsem.at[slot]).wait()
                h_scr[rows, :] = _rms(xbuf[slot], g_ref[...]).astype(BF16)

        @pl.when(m > 0)
        def _():
            for sb, rows in subs:
                @pl.when(sb < ns_ref[m - 1])
                def _():
                    result_copy(sb, rows, m - 1).wait()

        for sb, rows in subs:
            @pl.when(sb < n_live)
            def _():
                acc[rows, :] = jnp.zeros((sub, acc.shape[1]), F32)

    @pl.when(f == 1)
    def _():
        xbuf[0] = jnp.zeros(xbuf.shape[1:], F32)
        for sb, rows in subs:
            @pl.when(sb >= n_live)
            def _():
                zero_copy(sb).start()

    bulk = MOE_BULK_SUBS if MOE_BULK_SUBS <= len(subs) else 0
    if bulk:
        bulk_rows = slice(0, bulk * sub)

        @pl.when(n_live >= bulk)
        def _():
            acc[bulk_rows, :] += _swiglu_partial(h_scr[bulk_rows, :], wg_ref, wu_ref, wd_ref)

    for pb in range(len(subs) // 2):
        both = slice(2 * pb * sub, (2 * pb + 2) * sub)
        first = slice(2 * pb * sub, (2 * pb + 1) * sub)
        in_bulk = bulk and 2 * pb + 2 <= bulk

        @pl.when((2 * pb + 2 <= n_live) & ((n_live < bulk) if in_bulk else True))
        def _():
            acc[both, :] += _swiglu_partial(h_scr[both, :], wg_ref, wu_ref, wd_ref)

        @pl.when(2 * pb + 1 == n_live)
        def _():
            acc[first, :] += _swiglu_partial(h_scr[first, :], wg_ref, wu_ref, wd_ref)

    @pl.when(f == n_f - 1)
    def _():
        for sb, rows in subs:
            @pl.when(sb < n_live)
            def _():
                result_copy(sb, rows, m).start()

            @pl.when(sb >= n_live)
            def _():
                zero_copy(sb).wait()

        @pl.when(m == n_m - 1)
        def _():
            for sb, rows in subs:
                @pl.when(sb < n_live)
                def _():
                    result_copy(sb, rows, m).wait()


def _experts(x, g, wg_all, wu_all, wd_all, j, tile_expert, n_sub, sub_base, src, n_tiles, *, tf):
    t, d = x.shape
    tile = MOE_ROW_TILE
    ff = wg_all.shape[-1]
    n_f = ff // tf

    def f_eff(m, f, ns):
        return jnp.where(ns[m] > 0, f, n_f - 1)

    w_in_spec = pl.BlockSpec((None, None, d, tf), lambda m, f, te, ns, sbase, sr: (j, te[m], 0, f_eff(m, f, ns)))
    w_dn_spec = pl.BlockSpec((None, None, tf, d), lambda m, f, te, ns, sbase, sr: (j, te[m], f_eff(m, f, ns), 0))
    return pl.pallas_call(
        _expert_kernel,
        grid_spec=pltpu.PrefetchScalarGridSpec(
            num_scalar_prefetch=4,
            grid=(n_tiles, n_f),
            in_specs=[
                pl.BlockSpec(memory_space=pl.ANY),
                pl.BlockSpec((1, d), lambda m, f, te, ns, sbase, sr: (0, 0)),
                w_in_spec,
                w_in_spec,
                w_dn_spec,
            ],
            out_specs=pl.BlockSpec(memory_space=pl.ANY),
            scratch_shapes=[
                pltpu.VMEM((2, MOE_SUB_TILE, d), F32),
                pltpu.VMEM((tile, d), BF16),
                pltpu.VMEM((tile, d), F32),
                pltpu.SemaphoreType.DMA((2,)),
                pltpu.SemaphoreType.DMA(()),
                pltpu.SemaphoreType.DMA(()),
            ],
        ),
        out_shape=jax.ShapeDtypeStruct((n_tiles * tile, d), F32),
        compiler_params=_cparams(("arbitrary", "arbitrary")),
        name="moe_experts",
    )(tile_expert, n_sub, sub_base, src, x, g, wg_all, wu_all, wd_all)


def _combine_kernel(dest_ref, x_ref, meta_ref, ys_hbm, o_ref, buf, sem, *, tb, t):
    i, n = pl.program_id(0), pl.num_programs(0)

    def gather_start(tile_idx, slot):
        def start(r, carry):
            for k in range(2):
                row = dest_ref[k * t + tile_idx * tb + r]
                pltpu.make_async_copy(
                    ys_hbm.at[pl.ds(row, 1)], buf.at[slot, k, pl.ds(r, 1)], sem.at[slot]).start(priority=k)
            return carry

        lax.fori_loop(0, tb, start, 0, unroll=8)

    def combine(slot):
        for k in range(2):
            pltpu.make_async_copy(ys_hbm.at[pl.ds(0, tb)], buf.at[slot, k], sem.at[slot]).wait()
        meta = meta_ref[...]
        o_ref[...] = (x_ref[...] + _lane_col(meta, META_P1) * buf[slot, 0]
                      + _lane_col(meta, META_P2) * buf[slot, 1])

    @pl.when(i == 0)
    def _():
        gather_start(0, 0)

    for slot in range(2):
        @pl.when(i % 2 == slot)
        def _():
            @pl.when(i + 1 < n)
            def _():
                gather_start(i + 1, 1 - slot)

            combine(slot)


def _combine(x, meta, ys, dest, *, tb):
    t, d = x.shape
    return pl.pallas_call(
        functools.partial(_combine_kernel, tb=tb, t=t),
        grid_spec=pltpu.PrefetchScalarGridSpec(
            num_scalar_prefetch=1,
            grid=(t // tb,),
            in_specs=[
                pl.BlockSpec((tb, d), lambda i, dst: (i, 0)),
                pl.BlockSpec((tb, LANES), lambda i, dst: (i, 0)),
                pl.BlockSpec(memory_space=pl.ANY),
            ],
            out_specs=pl.BlockSpec((tb, d), lambda i, dst: (i, 0)),
            scratch_shapes=[pltpu.VMEM((2, 2, tb, d), F32), pltpu.SemaphoreType.DMA((2,))],
        ),
        out_shape=jax.ShapeDtypeStruct((t, d), F32),
        compiler_params=_cparams(("arbitrary",)),
        name="moe_combine",
    )(dest, x, meta, ys)


def _moe(x, g, router_w, router_b, wg_all, wu_all, wd_all, j, *, tb_route, tf):
    t, d = x.shape
    w_pad = jnp.pad(router_w, ((0, 0), (0, LANES - N_EXPERTS)))
    b_pad = jnp.pad(router_b, (0, LANES - N_EXPERTS))[None, :]
    meta, cnt = _router(x, g, w_pad, b_pad, tm=tb_route)
    dest, src, tile_expert, n_sub, sub_base, n_tiles = _moe_plan(meta, cnt, t)
    ys = _experts(x, g, wg_all, wu_all, wd_all, j, tile_expert, n_sub, sub_base, src, n_tiles, tf=tf)
    return _combine(x, meta, ys, dest, tb=min(256, t))


def _ple_kernel(x_ref, g_ref, wg_ref, p_ref, wp_ref, gn_ref, o_ref, hn_ref, wgb_scr, wpb_scr):
    @pl.when(pl.program_id(0) == 0)
    def _():
        wgb_scr[...] = wg_ref[...].astype(BF16)
        wpb_scr[...] = wp_ref[...].astype(BF16)

    x = x_ref[...]
    gate = jax.nn.sigmoid(_dot(_rms(x, g_ref[...]).astype(BF16), wgb_scr[...]))
    x_new = x + gate * _dot(p_ref[...].astype(BF16), wpb_scr[...])
    o_ref[...] = x_new
    hn_ref[...] = _rms(x_new, gn_ref[...]).astype(BF16)


def _ple(x, g, wg_all, p_all, wp_all, layer, g_next, *, tm):
    t, d = x.shape
    pd = p_all.shape[-1]
    row_spec = pl.BlockSpec((tm, d), lambda m: (m, 0))
    gain_spec = pl.BlockSpec((1, d), lambda m: (0, 0))
    return pl.pallas_call(
        _ple_kernel,
        grid=(t // tm,),
        in_specs=[
            row_spec,
            gain_spec,
            _resident_weight_spec((None, d, d), (layer, 0, 0)),
            pl.BlockSpec((None, tm, pd), lambda m: (layer, m, 0)),
            _resident_weight_spec((None, pd, d), (layer, 0, 0)),
            gain_spec,
        ],
        out_specs=[row_spec, row_spec],
        out_shape=[jax.ShapeDtypeStruct((t, d), F32), jax.ShapeDtypeStruct((t, d), BF16)],
        scratch_shapes=[pltpu.VMEM((d, d), BF16), pltpu.VMEM((pd, d), BF16)],
        compiler_params=_cparams(("arbitrary",)),
        name="ple_gate",
    )(x, g, wg_all, p_all, wp_all, g_next)


def _rope_tables(seq):
    rows = seq // GRID_W
    row = jnp.broadcast_to(jnp.arange(rows, dtype=F32)[:, None], (rows, GRID_W)).reshape(seq)
    col = jnp.broadcast_to(jnp.arange(GRID_W, dtype=F32)[None, :], (rows, GRID_W)).reshape(seq)
    axis_dim = ATTN_HEAD_DIM // 2
    inv_freq = ROPE_THETA ** (-jnp.arange(0, axis_dim, 2, dtype=F32) / axis_dim)
    ar, ac = row[:, None] * inv_freq, col[:, None] * inv_freq
    cos_t = jnp.concatenate([jnp.cos(ar), jnp.cos(ar), jnp.cos(ac), jnp.cos(ac)], axis=-1)
    sin_t = jnp.concatenate([-jnp.sin(ar), jnp.sin(ar), -jnp.sin(ac), jnp.sin(ac)], axis=-1)
    return cos_t, sin_t


def kernel(x, p, norm_mix_g, w_in, conv_w, conv_b, b_igate, b_fgate, mlstm_norm_g, q_norm_g, k_norm_g, w_mlstm_up, w_attn_up, w_out, norm_ffn_g, ffn_w_gate, ffn_w_up, ffn_w_down, moe_router, moe_router_b, moe_w_gate, moe_w_up, moe_w_down, norm_ple_g, w_ple_gate, w_ple_proj):
    b, s, d = x.shape
    depth = w_in.shape[0]
    t = b * s
    col = _col_layout(d)
    cos_t, sin_t = _rope_tables(s)
    xt = x.reshape(t, d)
    p_all = p.reshape(depth, t, p.shape[-1])
    w_in_t = jnp.swapaxes(w_in, 1, 2)

    tm_big, tm_half = min(1024, t), min(512, t)
    tn = min(512, d)
    tf = 256

    h_mix = _rms_cast(xt, norm_mix_g[0][None, :], tm=tm_half)
    for i in range(depth):
        z, gates = _in_proj(h_mix, w_in_t, i, tm=min(2048, t), tn=tn)
        z3 = z.reshape(b, s, col["total"])
        qk3 = _conv_silu(z3, conv_w[i], conv_b[i][None, :], col0=col["qk"], tc=256)
        gate_bias = jnp.concatenate(
            [b_igate[i].reshape(-1), b_fgate[i].reshape(-1), jnp.zeros((LANES - N_GATE_COLS,), F32)])[None, :]
        hf, hb = _mlstm_scan(qk3, z3, gates.reshape(b, s, LANES), gate_bias, v_col0=col["mv"])
        q3, k3, vt3 = _qk_prep(z3, cos_t, sin_t, q_norm_g[i][None, :], k_norm_g[i][None, :],
                               aq_col0=col["aq"], ak_col0=col["ak"], av_col0=col["av"], ts=min(512, s))
        ha = _flash_gqa(q3, k3, vt3, _score_bound(q_norm_g[i], k_norm_g[i]), tq=min(256, s), tk=min(2048, s))
        hm = _mlstm_out(hf.reshape(t, MLSTM_WIDTH), hb.reshape(t, MLSTM_WIDTH), z,
                        mlstm_norm_g[i].reshape(1, MLSTM_WIDTH), mo_col0=col["mo"], tm=tm_half)
        y = _merge(hm, ha.reshape(t, ATTN_WIDTH), z, w_mlstm_up, w_attn_up, i,
                   gm_col0=col["gm"], ga_col0=col["ga"], tm=tm_half)
        xt = _out_proj(y, w_out, i, xt, tm=tm_half)

        j = i // 2
        g_ffn = norm_ffn_g[i][None, :]
        if i % 2 == 0:
            xt = _ffn(xt, g_ffn, ffn_w_gate, ffn_w_up, ffn_w_down, j, tm=tm_big, tf=tf)
        else:
            xt = _moe(xt, g_ffn, moe_router[j], moe_router_b[j], moe_w_gate, moe_w_up, moe_w_down, j,
                      tb_route=tm_half, tf=tf)

        g_next = norm_mix_g[min(i + 1, depth - 1)][None, :]
        xt, h_mix = _ple(xt, norm_ple_g[i][None, :], w_ple_gate, p_all, w_ple_proj, i, g_next, tm=tm_half)

    return xt.reshape(b, s, d)
```
